```python
import math
import jax
import jax.numpy as jnp
from jax import lax
import numpy as np

D_MODEL = 4096
BATCH = 8
SEQ = 2048
DEPTH = 2
DEC_BATCH = 8
DEC_SEQ = 64
PAST_LEN = 1024

CHUNK = 64
N_MEM = 256
N_BRANCH = 4
BRANCH_W = D_MODEL // 4
A_HEAD_DIM = 64
A_HEADS = BRANCH_W // A_HEAD_DIM
A_LORA_W = 64
A_LORA_A = 64
A_SHIFT_W = 3 * BRANCH_W + A_LORA_W + A_LORA_A
B_HEAD_DIM = 128
B_HEADS = BRANCH_W // B_HEAD_DIM
Q_BLOCK = 128
C_CHUNK = 128
C_GROUP_DIM = 128
C_GROUPS = BRANCH_W // C_GROUP_DIM
M_HEADS = 4
M_HEAD_DIM = BRANCH_W // M_HEADS
NORM_EPS = 1e-6
GN_EPS = 64e-5

IN_SIZES = (BRANCH_W, BRANCH_W, BRANCH_W, A_LORA_W, A_LORA_A, BRANCH_W,
            BRANCH_W, BRANCH_W, BRANCH_W, BRANCH_W,
            BRANCH_W, BRANCH_W, BRANCH_W,
            BRANCH_W)
N_IN = sum(IN_SIZES)
IN_SPLITS = tuple(sum(IN_SIZES[:i + 1]) for i in range(len(IN_SIZES) - 1))
A_SPLITS = IN_SPLITS[:4]

kernel_name = 'hybrid_streaming_rwkv7_stickbreak_chunkmlp_step'


def rmsnorm(x, g):
    xf = x.astype(jnp.float32)
    y = xf * lax.rsqrt(jnp.mean(xf * xf, axis=-1, keepdims=True) + NORM_EPS)
    return (y * g.astype(jnp.float32)).astype(x.dtype)


def layernorm(x, w, b):
    xf = x.astype(jnp.float32)
    mu = jnp.mean(xf, axis=-1, keepdims=True)
    var = jnp.mean(jnp.square(xf - mu), axis=-1, keepdims=True)
    return ((xf - mu) * lax.rsqrt(var + NORM_EPS) * w + b).astype(x.dtype)


def rwkv7_scan(s0, r, decay, k, v, kk, a):
    xs = tuple(jnp.moveaxis(u, 1, 0) for u in (r, decay, k, v, kk, a))

    def step(s, inp):
        r_t, d_t, k_t, v_t, kk_t, a_t = inp
        s_kk = jnp.einsum('bhvk,bhk->bhv', s, kk_t)
        s = (s * d_t[:, :, None, :]
             - s_kk[..., None] * (kk_t * a_t)[:, :, None, :]
             + v_t[..., None] * k_t[:, :, None, :])
        y = jnp.einsum('bhvk,bhk->bhv', s, r_t)
        return s, y

    s_fin, y = lax.scan(step, s0.astype(jnp.float32), xs)
    return jnp.moveaxis(y, 0, 1), s_fin


def rwkv7_branch(pa, z, shift_prev, s_prev, mu, w0, w_up, a0, a_up, k_k, k_a, r_k, gn_w, gn_b):
    bsz, t, _ = pa.shape
    f32 = jnp.float32
    prev = jnp.concatenate([shift_prev.astype(pa.dtype), pa[:, :-1]], axis=1)
    xs = pa + mu * (prev - pa)
    r, k, v, wd, ad = jnp.split(xs, A_SPLITS, axis=-1)
    w_log = -jax.nn.softplus(-(w0 + jnp.tanh(wd) @ w_up).astype(f32)) - 0.5
    decay = jnp.exp(-jnp.exp(w_log))
    a = jax.nn.sigmoid((a0 + ad @ a_up).astype(f32))
    hd = lambda u: u.astype(f32).reshape(bsz, t, A_HEADS, A_HEAD_DIM)
    r, k, v, a, decay = hd(r), hd(k), hd(v), hd(a), hd(decay)
    kk = k * k_k.astype(f32)
    kk = kk / jnp.maximum(jnp.sqrt(jnp.sum(kk * kk, axis=-1, keepdims=True)), 1e-12)
    k = k * (1.0 + (a - 1.0) * k_a.astype(f32))
    y, s_new = rwkv7_scan(s_prev, r, decay, k, v, kk, a)
    mean = jnp.mean(y, axis=-1, keepdims=True)
    var = jnp.mean(jnp.square(y - mean), axis=-1, keepdims=True)
    y = (y - mean) * lax.rsqrt(var + GN_EPS) * gn_w.astype(f32) + gn_b.astype(f32)
    y = y + jnp.sum(r * k * r_k.astype(f32), axis=-1, keepdims=True) * v
    o = y.reshape(bsz, t, BRANCH_W).astype(pa.dtype) * jax.nn.silu(z)
    return o, pa[:, -1:], s_new


def stick_breaking(q, k, v, q_pos, k_pos):
    z = jnp.einsum('bqhd,bkhd->bhqk', q, k).astype(jnp.float32) / math.sqrt(q.shape[-1])
    mask = k_pos[None, :] < q_pos[:, None]
    log_beta = jax.nn.log_sigmoid(z)
    log_1m = jnp.where(mask, jax.nn.log_sigmoid(-z), 0.0)
    suffix = lax.cumsum(log_1m, axis=3, reverse=True) - log_1m
    att = jnp.where(mask, jnp.exp(log_beta + suffix), 0.0)
    return jnp.einsum('bhqk,bkhd->bqhd', att.astype(v.dtype), v)


def stick_breaking_blocks(q, k, v):
    bsz, t, nh, hd = q.shape
    nb = t // Q_BLOCK
    qb = jnp.moveaxis(q.reshape(bsz, nb, Q_BLOCK, nh, hd), 1, 0)
    k_pos = jnp.arange(t)

    def one_block(args):
        q_blk, i = args
        q_pos = i * Q_BLOCK + jnp.arange(Q_BLOCK)
        return stick_breaking(q_blk, k, v, q_pos, k_pos)

    out = lax.map(one_block, (qb, jnp.arange(nb)))
    return jnp.moveaxis(out, 0, 1).reshape(bsz, t, nh, hd)


def chunk_mlp(u, v, ws, bs, ln_w, ln_b):
    bsz, t, _ = u.shape
    vn = layernorm(v, ln_w, ln_b)
    clen = min(t, C_CHUNK)
    nc = t // clen
    mask = jnp.tril(jnp.ones((clen, clen), dtype=bool))
    wm = jnp.where(mask, ws[:, :clen, :clen], 0.0)
    vg = vn.reshape(bsz, nc, clen, C_GROUPS, C_GROUP_DIM)
    s = jnp.einsum('gts,bcsgd->bctgd', wm.astype(vg.dtype), vg) + bs[:, :clen].T[None, None, :, :, None]
    return u * s.reshape(bsz, t, BRANCH_W), vn


def memory_kv(mem, g, w_kv):
    bsz, n_mem, _ = mem.shape
    kv = rmsnorm(mem, g) @ w_kv
    k, v = jnp.split(kv, 2, axis=-1)
    shp = (bsz, n_mem, M_HEADS, M_HEAD_DIM)
    return k.reshape(shp), v.reshape(shp)


def memory_attend(q, mk, mv):
    s = jnp.einsum('bthd,bmhd->bhtm', q, mk).astype(jnp.float32) / math.sqrt(M_HEAD_DIM)
    p = jax.nn.softmax(s, axis=-1)
    return jnp.einsum('bhtm,bmhd->bthd', p.astype(mv.dtype), mv)


def trunk_layer(x, mem_k, mem_v, shift_prev, s_prev, past_k, past_v, p):
    bsz, t, _ = x.shape
    h = rmsnorm(x, p['g_pre'])
    proj = h @ p['w_in']
    pa, a_z, b_q, b_k, b_v, b_z, c_u, c_v, c_z, m_q = jnp.split(proj, IN_SPLITS[4:], axis=-1)

    o_a, shift_new, s_new = rwkv7_branch(pa, a_z, shift_prev, s_prev, p['a_mu'], p['a_w0'], p['a_w_up'],
                                         p['a_a0'], p['a_a_up'], p['a_k_k'], p['a_k_a'], p['a_r_k'],
                                         p['a_gn_w'], p['a_gn_b'])

    bh = lambda u: u.reshape(bsz, t, B_HEADS, B_HEAD_DIM)
    q, k_new, v_new = bh(b_q), bh(b_k), bh(b_v)
    if past_k is None:
        o_b = stick_breaking_blocks(q, k_new, v_new)
    else:
        n_past = past_k.shape[1]
        k_all = jnp.concatenate([past_k.astype(k_new.dtype), k_new], axis=1)
        v_all = jnp.concatenate([past_v.astype(v_new.dtype), v_new], axis=1)
        o_b = stick_breaking(q, k_all, v_all, n_past + jnp.arange(t), jnp.arange(n_past + t))
    o_b = o_b.reshape(bsz, t, BRANCH_W) * jax.nn.silu(b_z)

    o_c, vn_c = chunk_mlp(c_u, c_v, p['c_ws'], p['c_bs'], p['c_ln_w'], p['c_ln_b'])
    o_c = o_c * jax.nn.silu(c_z)

    o_m = memory_attend(m_q.reshape(bsz, t, M_HEADS, M_HEAD_DIM), mem_k.astype(x.dtype),
                        mem_v.astype(x.dtype)).reshape(bsz, t, BRANCH_W)

    branches = (o_a, o_b, o_c, o_m)
    merged = sum(jax.nn.sigmoid(h @ p['w_gate'][n] + p['b_gate'][n]) * (o @ p['w_br'][n])
                 for n, o in enumerate(branches))
    out = merged @ p['w_out']
    x = x + rmsnorm(out, p['g_post'])
    return x, shift_new, s_new, k_new, v_new, vn_c


def setup_inputs(seed: int = 0) -> dict:
    key = jax.random.key(seed)
    ks = iter(jax.random.split(key, 40))
    f32 = jnp.float32
    nrm = lambda shape, scale: scale * jax.random.normal(next(ks), shape, f32)
    uni = lambda shape, lo, hi: jax.random.uniform(next(ks), shape, f32, lo, hi)
    W = BRANCH_W
    L = DEPTH
    inp = {}
    inp['x_prompt'] = nrm((BATCH, SEQ, D_MODEL), 1.0)
    inp['x_sample'] = nrm((DEC_BATCH, DEC_SEQ, D_MODEL), 1.0)
    inp['cache_mem_k'] = nrm((L, DEC_BATCH, N_MEM, M_HEADS, M_HEAD_DIM), 1.0)
    inp['cache_mem_v'] = nrm((L, DEC_BATCH, N_MEM, M_HEADS, M_HEAD_DIM), 1.0)
    inp['cache_sb_k'] = nrm((L, DEC_BATCH, PAST_LEN, B_HEADS, B_HEAD_DIM), 1.0)
    inp['cache_sb_v'] = nrm((L, DEC_BATCH, PAST_LEN, B_HEADS, B_HEAD_DIM), 1.0)
    inp['state_rwkv'] = nrm((L, DEC_BATCH, A_HEADS, A_HEAD_DIM, A_HEAD_DIM), 0.5)
    inp['state_shift'] = nrm((L, DEC_BATCH, 1, A_SHIFT_W), 1.0)
    inp['mem_prompt'] = nrm((BATCH, N_MEM, D_MODEL), 1.0)
    inp['g_pre'] = 1.0 + nrm((L, D_MODEL), 0.02)
    inp['g_post'] = 1.0 + nrm((L, D_MODEL), 0.02)
    inp['w_in'] = nrm((L, D_MODEL, N_IN), D_MODEL ** -0.5)
    inp['a_mu'] = uni((L, A_SHIFT_W), 0.0, 1.0)
    inp['a_w0'] = uni((L, W), -5.0, 0.0)
    inp['a_w_up'] = nrm((L, A_LORA_W, W), 0.1 * A_LORA_W ** -0.5)
    inp['a_a0'] = nrm((L, W), 0.1)
    inp['a_a_up'] = nrm((L, A_LORA_A, W), 0.5 * A_LORA_A ** -0.5)
    inp['a_k_k'] = 0.85 + nrm((L, A_HEADS, A_HEAD_DIM), 0.05)
    inp['a_k_a'] = 1.0 + nrm((L, A_HEADS, A_HEAD_DIM), 0.05)
    inp['a_r_k'] = nrm((L, A_HEADS, A_HEAD_DIM), 0.1)
    inp['a_gn_w'] = 1.0 + nrm((L, A_HEADS, A_HEAD_DIM), 0.02)
    inp['a_gn_b'] = nrm((L, A_HEADS, A_HEAD_DIM), 0.02)
    inp['c_ws'] = nrm((L, C_GROUPS, C_CHUNK, C_CHUNK), C_CHUNK ** -0.5)
    inp['c_bs'] = 1.0 + nrm((L, C_GROUPS, C_CHUNK), 0.02)
    inp['c_ln_w'] = 1.0 + nrm((L, W), 0.02)
    inp['c_ln_b'] = nrm((L, W), 0.02)
    inp['g_mem'] = 1.0 + nrm((L, D_MODEL), 0.02)
    inp['w_mem_kv'] = nrm((L, D_MODEL, 2 * W), D_MODEL ** -0.5)
    inp['w_gate'] = nrm((L, N_BRANCH, D_MODEL, D_MODEL), D_MODEL ** -0.5)
    inp['b_gate'] = nrm((L, N_BRANCH, D_MODEL), 0.02)
    inp['w_br'] = nrm((L, N_BRANCH, W, D_MODEL), W ** -0.5)
    inp['w_out'] = nrm((L, D_MODEL, D_MODEL), D_MODEL ** -0.5)
    return inp


def reference(x_prompt, x_sample, cache_mem_k, cache_mem_v, cache_sb_k, cache_sb_v, state_rwkv,
              state_shift, mem_prompt, g_pre, g_post, w_in, a_mu, a_w0, a_w_up, a_a0, a_a_up, a_k_k,
              a_k_a, a_r_k, a_gn_w, a_gn_b, c_ws, c_bs, c_ln_w, c_ln_b, g_mem, w_mem_kv, w_gate,
              b_gate, w_br, w_out):
    bp = x_prompt.shape[0]
    yp, ys = x_prompt, x_sample
    mem_k_p, mem_v_p, sbk_p, sbv_p, rw_p, sh_p = [], [], [], [], [], []
    sbk_s, sbv_s, rw_s, sh_s, cv_s = [], [], [], [], []
    for l in range(DEPTH):
        lp = {'g_pre': g_pre[l], 'g_post': g_post[l], 'w_in': w_in[l], 'a_mu': a_mu[l],
              'a_w0': a_w0[l], 'a_w_up': a_w_up[l], 'a_a0': a_a0[l], 'a_a_up': a_a_up[l],
              'a_k_k': a_k_k[l], 'a_k_a': a_k_a[l], 'a_r_k': a_r_k[l], 'a_gn_w': a_gn_w[l],
              'a_gn_b': a_gn_b[l], 'c_ws': c_ws[l], 'c_bs': c_bs[l], 'c_ln_w': c_ln_w[l],
              'c_ln_b': c_ln_b[l], 'w_gate': w_gate[l], 'b_gate': b_gate[l], 'w_br': w_br[l],
              'w_out': w_out[l]}
        mk, mv = memory_kv(mem_prompt, g_mem[l], w_mem_kv[l])
        shift0 = jnp.zeros((bp, 1, A_SHIFT_W), yp.dtype)
        s0 = jnp.zeros((bp, A_HEADS, A_HEAD_DIM, A_HEAD_DIM), jnp.float32)
        yp, sh, st, kn, vn, _ = trunk_layer(yp, mk, mv, shift0, s0, None, None, lp)
        mem_k_p.append(mk); mem_v_p.append(mv); sbk_p.append(kn); sbv_p.append(vn)
        rw_p.append(st); sh_p.append(sh)
        ys, sh, st, kn, vn, cvn = trunk_layer(ys, cache_mem_k[l], cache_mem_v[l], state_shift[l],
                                              state_rwkv[l], cache_sb_k[l], cache_sb_v[l], lp)
        sbk_s.append(kn); sbv_s.append(vn); rw_s.append(st); sh_s.append(sh); cv_s.append(cvn)
    return (yp, ys,
            jnp.stack(mem_k_p), jnp.stack(mem_v_p), jnp.stack(sbk_p), jnp.stack(sbv_p),
            jnp.stack(rw_p), jnp.stack(sh_p),
            jnp.stack(sbk_s), jnp.stack(sbv_s), jnp.stack(rw_s), jnp.stack(sh_s), jnp.stack(cv_s))
```

```python
import functools
import math

import jax
import jax.numpy as jnp
from jax import lax
from jax.experimental import pallas as pl
from jax.experimental.pallas import tpu as pltpu

F32 = jnp.float32
BF16 = jnp.bfloat16

NORM_EPS = 1e-6
GN_EPS = 64e-5
HEAD_A = 64
GROUP_A = 4
SLAB = HEAD_A * GROUP_A
HEAD_B = 128
HEAD_M = 256
GROUP_C = 128
RWKV_CHUNK = 64
VMEM_LIMIT_BYTES = 56 * 1024 * 1024


def _params(*sem):
    return pltpu.CompilerParams(dimension_semantics=sem, vmem_limit_bytes=VMEM_LIMIT_BYTES)


def _softplus(x):
    return jnp.maximum(x, 0.0) + jnp.log1p(jnp.exp(-jnp.abs(x)))


def _split2(x):
    hi = x.astype(BF16)
    lo = (x - hi.astype(F32)).astype(BF16)
    return hi, lo


def _dot(a, b):
    return jnp.dot(a, b, preferred_element_type=F32)


def _dot_nt(a, b):
    return lax.dot_general(a, b, (((1,), (1,)), ((), ())), preferred_element_type=F32)


def _rmsnorm_kernel(x_ref, g_ref, o_ref):
    x = x_ref[...]
    ms = jnp.mean(x * x, axis=-1, keepdims=True)
    o_ref[...] = (x * lax.rsqrt(ms + NORM_EPS) * g_ref[...]).astype(o_ref.dtype)


def rmsnorm_rows(x, g, tm):
    m, d = x.shape
    return pl.pallas_call(
        _rmsnorm_kernel,
        grid=(m // tm,),
        in_specs=[pl.BlockSpec((tm, d), lambda i: (i, 0)), pl.BlockSpec((1, d), lambda i: (0, 0))],
        out_specs=pl.BlockSpec((tm, d), lambda i: (i, 0)),
        out_shape=jax.ShapeDtypeStruct((m, d), BF16),
        compiler_params=_params("parallel"),
        name="rmsnorm_rows",
    )(x, g.reshape(1, d))


def _post_kernel(x_ref, y_ref, g_ref, o_ref):
    y = y_ref[...]
    ms = jnp.mean(y * y, axis=-1, keepdims=True)
    o_ref[...] = x_ref[...] + y * lax.rsqrt(ms + NORM_EPS) * g_ref[...]


def post_norm_residual(x, y, g, tm):
    m, d = x.shape
    row = pl.BlockSpec((tm, d), lambda i: (i, 0))
    return pl.pallas_call(
        _post_kernel,
        grid=(m // tm,),
        in_specs=[row, row, pl.BlockSpec((1, d), lambda i: (0, 0))],
        out_specs=row,
        out_shape=jax.ShapeDtypeStruct((m, d), F32),
        compiler_params=_params("parallel"),
        name="post_norm_residual",
    )(x, y, g.reshape(1, d))


def _matmul_kernel(x_ref, w_ref, o_ref):
    o_ref[...] = _dot(x_ref[...], w_ref[...]).astype(o_ref.dtype)


def matmul(x, w, tm, tn, out_dtype=F32):
    m, k = x.shape
    n = w.shape[1]
    return pl.pallas_call(
        _matmul_kernel,
        grid=(n // tn, m // tm),
        in_specs=[pl.BlockSpec((tm, k), lambda j, i: (i, 0)), pl.BlockSpec((k, tn), lambda j, i: (0, j))],
        out_specs=pl.BlockSpec((tm, tn), lambda j, i: (i, j)),
        out_shape=jax.ShapeDtypeStruct((m, n), out_dtype),
        compiler_params=_params("parallel", "parallel"),
        name="matmul",
    )(x, w)


def _merge_kernel(h_ref, oa_ref, ob_ref, oc_ref, om_ref, wg_ref, bg_ref, wbr_ref, out_ref):
    h = h_ref[...]
    acc = None
    for n, o_ref in enumerate((oa_ref, ob_ref, oc_ref, om_ref)):
        gate = jax.nn.sigmoid(_dot(h, wg_ref[n]) + bg_ref[n])
        term = gate * _dot(o_ref[...], wbr_ref[n])
        acc = term if acc is None else acc + term
    out_ref[...] = acc.astype(out_ref.dtype)


def gated_merge(h, branches, wg, bg, wbr, tm, tn):
    m, d = h.shape
    nb, w, _ = wbr.shape
    resident = dict(pipeline_mode=pl.Buffered(1))
    o_spec = pl.BlockSpec((tm, w), lambda j, i: (i, 0))
    return pl.pallas_call(
        _merge_kernel,
        grid=(d // tn, m // tm),
        in_specs=[pl.BlockSpec((tm, d), lambda j, i: (i, 0)), o_spec, o_spec, o_spec, o_spec,
                  pl.BlockSpec((nb, d, tn), lambda j, i: (0, 0, j), **resident),
                  pl.BlockSpec((nb, 1, tn), lambda j, i: (0, 0, j)),
                  pl.BlockSpec((nb, w, tn), lambda j, i: (0, 0, j), **resident)],
        out_specs=pl.BlockSpec((tm, tn), lambda j, i: (i, j)),
        out_shape=jax.ShapeDtypeStruct((m, d), BF16),
        compiler_params=_params("parallel", "parallel"),
        name="gated_merge",
    )(h, *branches, wg, bg.reshape(nb, 1, d), wbr)


def _memattn_kernel(q_ref, mk_ref, mv_ref, o_ref, *, heads):
    scale = 1.0 / math.sqrt(HEAD_M)
    for hd in range(heads):
        sl = slice(hd * HEAD_M, (hd + 1) * HEAD_M)
        q = q_ref[:, sl].astype(BF16)
        s = _dot_nt(q, mk_ref[0, :, sl]) * scale
        s = s - jnp.max(s, axis=-1, keepdims=True)
        p = jnp.exp(s)
        p = p / jnp.sum(p, axis=-1, keepdims=True)
        o_ref[:, sl] = _dot(p.astype(BF16), mv_ref[0, :, sl]).astype(o_ref.dtype)


def memory_attention(proj, q_col, mk, mv, nseq, t, tq):
    n_mem, w = mk.shape[1:]
    per_seq = t // tq
    mem_spec = pl.BlockSpec((1, n_mem, w), lambda i: (i // per_seq, 0, 0))
    return pl.pallas_call(
        functools.partial(_memattn_kernel, heads=w // HEAD_M),
        grid=(nseq * per_seq,),
        in_specs=[pl.BlockSpec((tq, w), lambda i: (i, q_col)), mem_spec, mem_spec],
        out_specs=pl.BlockSpec((tq, w), lambda i: (i, 0)),
        out_shape=jax.ShapeDtypeStruct((nseq * t, w), BF16),
        compiler_params=_params("parallel"),
        name="memory_attention",
    )(proj, mk, mv)


def _cmlp_kernel(u_ref, v_ref, z_ref, ws_ref, bs_ref, lnw_ref, lnb_ref, o_ref, vn_ref, *, clen, chunks):
    v = v_ref[...]
    mu = jnp.mean(v, axis=-1, keepdims=True)
    var = jnp.mean(jnp.square(v - mu), axis=-1, keepdims=True)
    vn = (v - mu) * lax.rsqrt(var + NORM_EPS) * lnw_ref[...] + lnb_ref[...]
    vn_ref[...] = vn
    vn16 = vn.astype(BF16)
    groups = vn.shape[1] // GROUP_C
    row = lax.broadcasted_iota(jnp.int32, (clen, clen), 0)
    col = lax.broadcasted_iota(jnp.int32, (clen, clen), 1)
    for g in range(groups):
        wm = jnp.where(row >= col, ws_ref[g], 0.0).astype(BF16)
        cs = slice(g * GROUP_C, (g + 1) * GROUP_C)
        for c in range(chunks):
            rs = slice(c * clen, (c + 1) * clen)
            s = _dot(wm, vn16[rs, cs]) + bs_ref[:, cs]
            o_ref[rs, cs] = (u_ref[rs, cs] * s * jax.nn.silu(z_ref[rs, cs])).astype(o_ref.dtype)


def chunk_mlp(proj, u_col, ws, bs_full, ln_w, ln_b, rows, clen, chunks):
    w = ln_w.shape[0]
    tm = clen * chunks
    col = lambda c: pl.BlockSpec((tm, w), lambda i: (i, c))
    const2 = lambda a: pl.BlockSpec(a.shape, lambda i: (0, 0))
    out = pl.BlockSpec((tm, w), lambda i: (i, 0))
    return pl.pallas_call(
        functools.partial(_cmlp_kernel, clen=clen, chunks=chunks),
        grid=(rows // tm,),
        in_specs=[col(u_col), col(u_col + 1), col(u_col + 2),
                  pl.BlockSpec(ws.shape, lambda i: (0, 0, 0)), const2(bs_full),
                  pl.BlockSpec((1, w), lambda i: (0, 0)), pl.BlockSpec((1, w), lambda i: (0, 0))],
        out_specs=[out, out],
        out_shape=[jax.ShapeDtypeStruct((rows, w), BF16), jax.ShapeDtypeStruct((rows, w), F32)],
        compiler_params=_params("parallel"),
        name="chunk_mlp",
    )(proj, proj, proj, ws, bs_full, ln_w.reshape(1, w), ln_b.reshape(1, w))


def _sb_tile(q16, k, v, carry, acc, ucat, mask):
    tk = k.shape[0]
    z = _dot_nt(q16, k.astype(BF16)) * (1.0 / math.sqrt(HEAD_B))
    l1p = jnp.log1p(jnp.exp(-jnp.abs(z)))
    log_beta = jnp.minimum(z, 0.0) - l1p
    log_1m = -jnp.maximum(z, 0.0) - l1p
    if mask is not None:
        log_1m = jnp.where(mask, log_1m, 0.0)
    hi, lo = _split2(log_1m)
    cs = _dot(jnp.concatenate([hi, lo], axis=1), ucat)
    if tk >= 128:
        carry_b = jnp.concatenate([carry] * (tk // 128), axis=1)
    else:
        carry_b = carry[:, :tk]
    att = jnp.exp(log_beta + cs[:, :tk] + carry_b)
    if mask is not None:
        att = jnp.where(mask, att, 0.0)
    acc = acc + _dot(att.astype(BF16), v.astype(BF16))
    carry = carry + cs[:, max(tk, 128):]
    return carry, acc


def _sb_kernel(q_ref, z_ref, kn_ref, vn_ref, kp_ref, vp_ref, ud_ref, uf_ref, o_ref, *, tq, tk, static_past):
    qi = pl.program_id(2)
    q16 = q_ref[...].astype(BF16)
    row = lax.broadcasted_iota(jnp.int32, (tq, tq), 0)
    col = lax.broadcasted_iota(jnp.int32, (tq, tq), 1)
    dstart = pl.multiple_of(qi * tq, tq)
    zero = jnp.zeros((tq, 128), F32)
    carry, acc = _sb_tile(q16, kn_ref[pl.ds(dstart, tq), :], vn_ref[pl.ds(dstart, tq), :], zero, zero,
                          ud_ref[...], col < row)

    def body(step, state):
        n_full = static_past if static_past is not None else qi
        start = pl.multiple_of((n_full - 1 - step) * tk, tk)
        return _sb_tile(q16, kp_ref[pl.ds(start, tk), :], vp_ref[pl.ds(start, tk), :], state[0], state[1],
                        uf_ref[...], None)

    n_loop = static_past if static_past is not None else qi
    carry, acc = lax.fori_loop(0, n_loop, body, (carry, acc))
    o_ref[...] = (acc * jax.nn.silu(z_ref[...])).astype(o_ref.dtype)


def _suffix_matrix(tk):
    s_later = lax.broadcasted_iota(jnp.int32, (tk, tk), 0)
    s_here = lax.broadcasted_iota(jnp.int32, (tk, tk), 1)
    parts = [(s_later > s_here).astype(BF16)]
    if tk < 128:
        parts.append(jnp.zeros((tk, 128 - tk), BF16))
    u = jnp.concatenate(parts + [jnp.ones((tk, 128), BF16)], axis=1)
    return jnp.concatenate([u, u], axis=0)


def stick_breaking(proj, q_col, z_col, k_col, v_col, past_k, past_v, nseq, t, heads, tq, tk):
    nq = t // tq
    own = past_k is None
    if own:
        past_k, past_v, pk_col, pv_col, n_past, static_past = proj, proj, k_col, v_col, t, None
    else:
        n_past = past_k.shape[0] // nseq
        pk_col, pv_col, static_past = 0, 0, n_past // tk
    blk = lambda col: pl.BlockSpec((tq, HEAD_B), lambda b, h, i: (b * nq + i, col + h))
    seq = lambda rows, col: pl.BlockSpec((rows, HEAD_B), lambda b, h, i: (b, col + h))
    const = lambda a: pl.BlockSpec(a.shape, lambda b, h, i: (0, 0))
    ud, uf = _suffix_matrix(tq), _suffix_matrix(tk)
    return pl.pallas_call(
        functools.partial(_sb_kernel, tq=tq, tk=tk, static_past=static_past),
        grid=(nseq, heads, nq),
        in_specs=[blk(q_col), blk(z_col), seq(t, k_col), seq(t, v_col), seq(n_past, pk_col), seq(n_past, pv_col),
                  const(ud), const(uf)],
        out_specs=pl.BlockSpec((tq, HEAD_B), lambda b, h, i: (b * nq + i, h)),
        out_shape=jax.ShapeDtypeStruct((nseq * t, heads * HEAD_B), BF16),
        compiler_params=_params("parallel", "parallel", "arbitrary"),
        name="stick_breaking",
    )(proj, proj, proj, proj, past_k, past_v, ud, uf)


def _block_stack(x, lane_masks):
    return jnp.concatenate([jnp.where(m, x, 0.0) for m in lane_masks], axis=0)


def _rwkv_kernel(x_ref, lora_ref, z_ref, sh_main_ref, sh_lora_ref, s0_ref,
                 mu_main_ref, mu_lora_ref, w0_ref, wup_ref, a0_ref, aup_ref, kk_ref, ka_ref, rk_ref,
                 gnw_ref, gnb_ref, e_ref, bd_ref, ltri_ref,
                 o_ref, sout_ref, shm_out_ref, shl_out_ref,
                 xbuf, lbuf, sbd, ybuf, *, chunk, width):
    c = pl.program_id(1)
    n_chunks = pl.num_programs(1)
    C, W = chunk, width
    n_slabs = W // SLAB

    @pl.when(c == 0)
    def _init():
        xbuf[7:8, :] = sh_main_ref[0]
        lbuf[7:8, :] = sh_lora_ref[0]
        sbd[...] = jnp.zeros_like(sbd)
        for hd in range(W // HEAD_A):
            g, j = divmod(hd, GROUP_A)
            sbd[g, j * HEAD_A:(j + 1) * HEAD_A, j * HEAD_A:(j + 1) * HEAD_A] = s0_ref[0, hd]

    x = x_ref[...]
    xbuf[8:8 + C, :] = x
    xs = x + mu_main_ref[...] * (xbuf[7:7 + C, :] - x)
    xbuf[7:8, :] = x[C - 1:C, :]
    lo_x = lora_ref[...]
    lbuf[8:8 + C, :] = lo_x
    lo_s = lo_x + mu_lora_ref[...] * (lbuf[7:7 + C, :] - lo_x)
    lbuf[7:8, :] = lo_x[C - 1:C, :]

    r, k, v = xs[:, :W], xs[:, W:2 * W], xs[:, 2 * W:]
    w_pre = w0_ref[...] + _dot(jnp.tanh(lo_s).astype(BF16), wup_ref[...])
    ld = -jnp.exp(-_softplus(-w_pre) - 0.5)
    a = jax.nn.sigmoid(a0_ref[...] + _dot(lo_s.astype(BF16), aup_ref[...]))

    e_mat = e_ref[...]

    def seg_sum(val):
        parts = []
        for g in range(n_slabs):
            hi, lo = _split2(val[:, g * SLAB:(g + 1) * SLAB])
            parts.append(_dot(hi, e_mat) + _dot(lo, e_mat))
        return jnp.concatenate(parts, axis=1)

    kk = k * kk_ref[...]
    kk = kk / jnp.maximum(jnp.sqrt(seg_sum(kk * kk)), 1e-12)
    kmod = k * (1.0 + (a - 1.0) * ka_ref[...])

    ld_hi = ld.astype(BF16)
    ld_r1 = ld - ld_hi.astype(F32)
    ld_mid = ld_r1.astype(BF16)
    ld_lo = (ld_r1 - ld_mid.astype(F32)).astype(BF16)
    ltri = ltri_ref[...]
    lp = _dot(ltri, ld_hi) + _dot(ltri, ld_mid) + _dot(ltri, ld_lo)
    lp_end = lp[C - 1:C, :]
    e_neg = jnp.exp(-lp)
    kka = kk * a
    kap = kk * jnp.exp(lp - ld)
    bet = kka * e_neg
    kt = kmod * e_neg
    rt = r * jnp.exp(lp)
    e_end = jnp.exp(lp_end - lp)
    kt_end = kmod * e_end
    bet_end = kka * e_end
    dec_end = jnp.exp(lp_end)

    lane = lax.broadcasted_iota(jnp.int32, (1, SLAB), 1)
    lane_masks = [(lane >= j * HEAD_A) & (lane < (j + 1) * HEAD_A) for j in range(GROUP_A)]
    t_row = lax.broadcasted_iota(jnp.int32, (C, GROUP_A * C), 0)
    s_col = lax.broadcasted_iota(jnp.int32, (C, GROUP_A * C), 1) & (C - 1)
    strict, incl = s_col < t_row, s_col <= t_row
    stack = lambda val: _block_stack(val, lane_masks)
    nh = GROUP_A * C

    for g in range(n_slabs):
        sl = slice(g * SLAB, (g + 1) * SLAB)
        lhs = jnp.concatenate([kap[:, sl], rt[:, sl]], axis=0).astype(BF16)
        rhs = jnp.concatenate([stack(bet[:, sl]), stack(kt[:, sl])], axis=0).astype(BF16)
        sc = _dot_nt(lhs, rhs)
        a_low = jnp.where(strict, sc[:C, :nh], 0.0)
        b_low = jnp.where(strict, sc[:C, nh:], 0.0)
        a_inc = jnp.where(incl, sc[C:, :nh], 0.0)
        b_inc = jnp.where(incl, sc[C:, nh:], 0.0)
        s_g = sbd[g]
        ls = _dot_nt(lhs, s_g.astype(BF16))
        vm = v[:, sl]
        vm_stack = stack(vm).astype(BF16)
        xw = ls[:C] + _dot(b_low.astype(BF16), vm_stack)
        p = -a_low
        pows = [p]
        for _ in range(5):
            p = _dot(p.astype(BF16), stack(p).astype(BF16))
            pows.append(p)
        for p in reversed(pows):
            xw = xw + _dot(p.astype(BF16), stack(xw).astype(BF16))
        y = ls[C:] + _dot(jnp.concatenate([a_inc, b_inc], axis=1).astype(BF16),
                          jnp.concatenate([stack(-xw).astype(BF16), vm_stack], axis=0))
        ybuf[:, sl] = y
        upd = _dot(jnp.concatenate([vm, -xw], axis=0).T.astype(BF16),
                   jnp.concatenate([kt_end[:, sl], bet_end[:, sl]], axis=0).astype(BF16))
        sbd[g] = s_g * dec_end[:, sl] + upd * bd_ref[...]

    y = ybuf[...]
    inv_n = 1.0 / HEAD_A
    mean = seg_sum(y) * inv_n
    yc = y - mean
    var = seg_sum(yc * yc) * inv_n
    y = yc * lax.rsqrt(var + GN_EPS) * gnw_ref[...] + gnb_ref[...]
    y = y + seg_sum(r * kmod * rk_ref[...]) * v
    o_ref[...] = (y * jax.nn.silu(z_ref[...])).astype(o_ref.dtype)

    @pl.when(c == n_chunks - 1)
    def _fin():
        shm_out_ref[0] = xbuf[7:8, :]
        shl_out_ref[0] = lbuf[7:8, :]
        for hd in range(W // HEAD_A):
            g, j = divmod(hd, GROUP_A)
            sout_ref[0, hd] = sbd[g, j * HEAD_A:(j + 1) * HEAD_A, j * HEAD_A:(j + 1) * HEAD_A]


def rwkv7(proj, lora, z_col, shift_main, shift_lora, s0, prm, nseq, t):
    W = prm["a_w0"].shape[0]
    C = RWKV_CHUNK
    nc = t // C
    n_heads = W // HEAD_A
    lw = lora.shape[1]
    half = lw // 2
    zpad = jnp.zeros((half, W), F32)
    wup = jnp.concatenate([prm["a_w_up"], zpad], axis=0).astype(BF16)
    aup = jnp.concatenate([zpad, prm["a_a_up"]], axis=0).astype(BF16)
    idx = jnp.arange(SLAB) // HEAD_A
    same = idx[:, None] == idx[None, :]
    ti = jnp.arange(C)
    ltri = (ti[:, None] >= ti[None, :]).astype(BF16)
    row1 = lambda a: a.reshape(1, -1)
    vec = lambda n: pl.BlockSpec((1, n), lambda b, c: (0, 0))
    full = lambda a: pl.BlockSpec(a.shape, lambda b, c: (0, 0))
    per_seq = lambda n: pl.BlockSpec((1, 1, n), lambda b, c: (b, 0, 0))
    state = pl.BlockSpec((1, n_heads, HEAD_A, HEAD_A), lambda b, c: (b, 0, 0, 0))
    e_mat, bd_mask = same.astype(BF16), same.astype(F32)
    return pl.pallas_call(
        functools.partial(_rwkv_kernel, chunk=C, width=W),
        grid=(nseq, nc),
        in_specs=[pl.BlockSpec((C, 3 * W), lambda b, c: (b * nc + c, 0)),
                  pl.BlockSpec((C, lw), lambda b, c: (b * nc + c, 0)),
                  pl.BlockSpec((C, W), lambda b, c: (b * nc + c, z_col)),
                  per_seq(3 * W), per_seq(lw), state,
                  vec(3 * W), vec(lw), vec(W), full(wup), vec(W), full(aup), vec(W), vec(W), vec(W), vec(W), vec(W),
                  full(e_mat), full(bd_mask), full(ltri)],
        out_specs=[pl.BlockSpec((C, W), lambda b, c: (b * nc + c, 0)), state, per_seq(3 * W), per_seq(lw)],
        out_shape=[jax.ShapeDtypeStruct((nseq * t, W), BF16),
                   jax.ShapeDtypeStruct((nseq, n_heads, HEAD_A, HEAD_A), F32),
                   jax.ShapeDtypeStruct((nseq, 1, 3 * W), F32),
                   jax.ShapeDtypeStruct((nseq, 1, lw), F32)],
        scratch_shapes=[pltpu.VMEM((C + 8, 3 * W), F32), pltpu.VMEM((C + 8, lw), F32),
                        pltpu.VMEM((W // SLAB, SLAB, SLAB), F32), pltpu.VMEM((C, W), F32)],
        compiler_params=_params("parallel", "arbitrary"),
        name="rwkv7",
    )(proj, lora, proj, shift_main, shift_lora, s0,
      row1(prm["a_mu"][:3 * W]), row1(prm["a_mu"][3 * W:]), row1(prm["a_w0"]), wup, row1(prm["a_a0"]), aup,
      row1(prm["a_k_k"]), row1(prm["a_k_a"]), row1(prm["a_r_k"]), row1(prm["a_gn_w"]), row1(prm["a_gn_b"]),
      e_mat, bd_mask, ltri)


COL_AZ, COL_BQ, COL_BK, COL_BV, COL_BZ, COL_CU, COL_MQ = 3, 4, 5, 6, 7, 8, 11


def trunk_layer(x, nseq, t, mk, mv, shift_prev, s_prev, past_k, past_v, prm):
    m, d = x.shape
    W = d // 4
    tm = min(512, m)
    h = rmsnorm_rows(x, prm["g_pre"], min(256, m))
    proj = matmul(h, prm["w_in_main"], tm, 1024)
    lora = matmul(h, prm["w_in_lora"], tm, prm["w_in_lora"].shape[1])

    o_a, s_new, sh_main, sh_lora = rwkv7(proj, lora, COL_AZ, shift_prev[..., :3 * W], shift_prev[..., 3 * W:],
                                         s_prev, prm, nseq, t)
    shift_new = jnp.concatenate([sh_main, sh_lora], axis=-1)

    heads_b = W // HEAD_B
    hb = lambda col: col * heads_b
    if past_k is None:
        o_b = stick_breaking(proj, hb(COL_BQ), hb(COL_BZ), hb(COL_BK), hb(COL_BV), None, None,
                             nseq, t, heads_b, 256, 256)
    else:
        o_b = stick_breaking(proj, hb(COL_BQ), hb(COL_BZ), hb(COL_BK), hb(COL_BV), past_k, past_v,
                             nseq, t, heads_b, t, 256)

    clen = min(t, prm["c_ws"].shape[1])
    groups = prm["c_ws"].shape[0]
    bs_full = jnp.repeat(prm["c_bs"][:, :clen].T, W // groups, axis=1)
    o_c, vn_c = chunk_mlp(proj, COL_CU, prm["c_ws"][:, :clen, :clen], bs_full, prm["c_ln_w"], prm["c_ln_b"],
                          m, clen, max(1, min(512, t) // clen))

    o_m = memory_attention(proj, COL_MQ, mk, mv, nseq, t, min(512, t))

    merged = gated_merge(h, (o_a, o_b, o_c, o_m), prm["w_gate"], prm["b_gate"], prm["w_br"], tm, 512)
    out = matmul(merged, prm["w_out"], tm, 1024)
    x_new = post_norm_residual(x, out, prm["g_post"], min(256, m))
    k_new = proj[:, COL_BK * W:(COL_BK + 1) * W]
    v_new = proj[:, COL_BV * W:(COL_BV + 1) * W]
    return x_new, shift_new, s_new, k_new, v_new, vn_c


def kernel(x_prompt, x_sample, cache_mem_k, cache_mem_v, cache_sb_k, cache_sb_v, state_rwkv, state_shift, mem_prompt, g_pre, g_post, w_in, a_mu, a_w0, a_w_up, a_a0, a_a_up, a_k_k, a_k_a, a_r_k, a_gn_w, a_gn_b, c_ws, c_bs, c_ln_w, c_ln_b, g_mem, w_mem_kv, w_gate, b_gate, w_br, w_out):
    bp, tp, d = x_prompt.shape
    bs_, ts, _ = x_sample.shape
    depth = w_in.shape[0]
    W = d // 4
    n_mem = mem_prompt.shape[1]
    m_heads = cache_mem_k.shape[3]
    b_heads = cache_sb_k.shape[3]
    a_heads = state_rwkv.shape[2]
    shift_w = state_shift.shape[-1]
    n_past = cache_sb_k.shape[2]
    lora_lo, lora_hi = 3 * W, shift_w

    yp = x_prompt.reshape(bp * tp, d)
    ys = x_sample.reshape(bs_ * ts, d)
    mem_rows = mem_prompt.reshape(bp * n_mem, d)
    outs = [[] for _ in range(11)]
    for l in range(depth):
        prm = {
            "g_pre": g_pre[l], "g_post": g_post[l],
            "w_in_main": jnp.concatenate([w_in[l][:, :lora_lo], w_in[l][:, lora_hi:]], axis=1).astype(BF16),
            "w_in_lora": w_in[l][:, lora_lo:lora_hi].astype(BF16),
            "a_mu": a_mu[l], "a_w0": a_w0[l], "a_w_up": a_w_up[l], "a_a0": a_a0[l], "a_a_up": a_a_up[l],
            "a_k_k": a_k_k[l].reshape(-1), "a_k_a": a_k_a[l].reshape(-1), "a_r_k": a_r_k[l].reshape(-1),
            "a_gn_w": a_gn_w[l].reshape(-1), "a_gn_b": a_gn_b[l].reshape(-1),
            "c_ws": c_ws[l], "c_bs": c_bs[l], "c_ln_w": c_ln_w[l], "c_ln_b": c_ln_b[l],
            "w_gate": w_gate[l].astype(BF16), "b_gate": b_gate[l], "w_br": w_br[l].astype(BF16),
            "w_out": w_out[l].astype(BF16),
        }
        kv = matmul(rmsnorm_rows(mem_rows, g_mem[l], 256), w_mem_kv[l].astype(BF16), 512, 1024)
        mk = kv[:, :W].reshape(bp, n_mem, W)
        mv = kv[:, W:].reshape(bp, n_mem, W)
        shift0 = jnp.zeros((bp, 1, shift_w), F32)
        s0 = jnp.zeros((bp, a_heads, HEAD_A, HEAD_A), F32)
        yp, sh, st, kn, vn, _ = trunk_layer(yp, bp, tp, mk.astype(BF16), mv.astype(BF16), shift0, s0, None, None, prm)
        outs[0].append(mk.reshape(bp, n_mem, m_heads, W // m_heads))
        outs[1].append(mv.reshape(bp, n_mem, m_heads, W // m_heads))
        outs[2].append(kn.reshape(bp, tp, b_heads, HEAD_B))
        outs[3].append(vn.reshape(bp, tp, b_heads, HEAD_B))
        outs[4].append(st)
        outs[5].append(sh)
        ys, sh, st, kn, vn, cvn = trunk_layer(
            ys, bs_, ts, cache_mem_k[l].reshape(bs_, n_mem, W).astype(BF16),
            cache_mem_v[l].reshape(bs_, n_mem, W).astype(BF16), state_shift[l], state_rwkv[l],
            cache_sb_k[l].reshape(bs_ * n_past, W), cache_sb_v[l].reshape(bs_ * n_past, W), prm)
        outs[6].append(kn.reshape(bs_, ts, b_heads, HEAD_B))
        outs[7].append(vn.reshape(bs_, ts, b_heads, HEAD_B))
        outs[8].append(st)
        outs[9].append(sh)
        outs[10].append(cvn.reshape(bs_, ts, W))
    return (yp.reshape(bp, tp, d), ys.reshape(bs_, ts, d)) + tuple(jnp.stack(o) for o in outs)
```

```python
import functools
import math

import jax
import jax.numpy as jnp
from jax import lax
from jax.experimental import pallas as pl
from jax.experimental.pallas import tpu as pltpu

F32 = jnp.float32
BF16 = jnp.bfloat16

NORM_EPS = 1e-6
GN_EPS = 64e-5
HEAD_A = 64
GROUP_A = 4
SLAB = HEAD_A * GROUP_A
HEAD_B = 128
HEAD_M = 256
GROUP_C = 128
RWKV_CHUNK = 64
VMEM_LIMIT_BYTES = 56 * 1024 * 1024


def _params(*sem):
    return pltpu.CompilerParams(dimension_semantics=sem, vmem_limit_bytes=VMEM_LIMIT_BYTES)


def _softplus(x):
    return jnp.maximum(x, 0.0) + jnp.log1p(jnp.exp(-jnp.abs(x)))


def _split2(x):
    hi = x.astype(BF16)
    lo = (x - hi.astype(F32)).astype(BF16)
    return hi, lo


def _dot(a, b):
    return jnp.dot(a, b, preferred_element_type=F32)


def _dot_nt(a, b):
    return lax.dot_general(a, b, (((1,), (1,)), ((), ())), preferred_element_type=F32)


def _rmsnorm_kernel(x_ref, g_ref, o_ref):
    x = x_ref[...]
    ms = jnp.mean(x * x, axis=-1, keepdims=True)
    o_ref[...] = (x * lax.rsqrt(ms + NORM_EPS) * g_ref[...]).astype(o_ref.dtype)


def rmsnorm_rows(x, g, tm):
    m, d = x.shape
    return pl.pallas_call(
        _rmsnorm_kernel,
        grid=(m // tm,),
        in_specs=[pl.BlockSpec((tm, d), lambda i: (i, 0)), pl.BlockSpec((1, d), lambda i: (0, 0))],
        out_specs=pl.BlockSpec((tm, d), lambda i: (i, 0)),
        out_shape=jax.ShapeDtypeStruct((m, d), BF16),
        compiler_params=_params("parallel"),
        name="rmsnorm_rows",
    )(x, g.reshape(1, d))


def _post_kernel(x_ref, y_ref, g_ref, o_ref):
    y = y_ref[...]
    ms = jnp.mean(y * y, axis=-1, keepdims=True)
    o_ref[...] = x_ref[...] + y * lax.rsqrt(ms + NORM_EPS) * g_ref[...]


def post_norm_residual(x, y, g, tm):
    m, d = x.shape
    row = pl.BlockSpec((tm, d), lambda i: (i, 0))
    return pl.pallas_call(
        _post_kernel,
        grid=(m // tm,),
        in_specs=[row, row, pl.BlockSpec((1, d), lambda i: (0, 0))],
        out_specs=row,
        out_shape=jax.ShapeDtypeStruct((m, d), F32),
        compiler_params=_params("parallel"),
        name="post_norm_residual",
    )(x, y, g.reshape(1, d))


def _matmul_kernel(x_ref, w_ref, o_ref):
    o_ref[...] = _dot(x_ref[...], w_ref[...]).astype(o_ref.dtype)


def matmul(x, w, tm, tn, out_dtype=F32):
    m, k = x.shape
    n = w.shape[1]
    return pl.pallas_call(
        _matmul_kernel,
        grid=(n // tn, m // tm),
        in_specs=[pl.BlockSpec((tm, k), lambda j, i: (i, 0)), pl.BlockSpec((k, tn), lambda j, i: (0, j))],
        out_specs=pl.BlockSpec((tm, tn), lambda j, i: (i, j)),
        out_shape=jax.ShapeDtypeStruct((m, n), out_dtype),
        compiler_params=_params("parallel", "parallel"),
        name="matmul",
    )(x, w)


def _merge_kernel(h_ref, oa_ref, ob_ref, oc_ref, om_ref, wg_ref, bg_ref, wbr_ref, out_ref):
    h = h_ref[...]
    acc = None
    for n, o_ref in enumerate((oa_ref, ob_ref, oc_ref, om_ref)):
        gate = jax.nn.sigmoid(_dot(h, wg_ref[n]) + bg_ref[n])
        term = gate * _dot(o_ref[...], wbr_ref[n])
        acc = term if acc is None else acc + term
    out_ref[...] = acc.astype(out_ref.dtype)


def gated_merge(h, branches, wg, bg, wbr, tm, tn):
    m, d = h.shape
    nb, w, _ = wbr.shape
    resident = dict(pipeline_mode=pl.Buffered(1))
    o_spec = pl.BlockSpec((tm, w), lambda j, i: (i, 0))
    return pl.pallas_call(
        _merge_kernel,
        grid=(d // tn, m // tm),
        in_specs=[pl.BlockSpec((tm, d), lambda j, i: (i, 0)), o_spec, o_spec, o_spec, o_spec,
                  pl.BlockSpec((nb, d, tn), lambda j, i: (0, 0, j), **resident),
                  pl.BlockSpec((nb, 1, tn), lambda j, i: (0, 0, j)),
                  pl.BlockSpec((nb, w, tn), lambda j, i: (0, 0, j), **resident)],
        out_specs=pl.BlockSpec((tm, tn), lambda j, i: (i, j)),
        out_shape=jax.ShapeDtypeStruct((m, d), BF16),
        compiler_params=_params("parallel", "parallel"),
        name="gated_merge",
    )(h, *branches, wg, bg.reshape(nb, 1, d), wbr)


def _memattn_kernel(q_ref, mk_ref, mv_ref, o_ref, *, heads):
    scale = 1.0 / math.sqrt(HEAD_M)
    for hd in range(heads):
        sl = slice(hd * HEAD_M, (hd + 1) * HEAD_M)
        q = q_ref[:, sl].astype(BF16)
        s = _dot_nt(q, mk_ref[0, :, sl]) * scale
        s = s - jnp.max(s, axis=-1, keepdims=True)
        p = jnp.exp(s)
        p = p / jnp.sum(p, axis=-1, keepdims=True)
        o_ref[:, sl] = _dot(p.astype(BF16), mv_ref[0, :, sl]).astype(o_ref.dtype)


def memory_attention(proj, q_col, mk, mv, nseq, t, tq):
    n_mem, w = mk.shape[1:]
    per_seq = t // tq
    mem_spec = pl.BlockSpec((1, n_mem, w), lambda i: (i // per_seq, 0, 0))
    return pl.pallas_call(
        functools.partial(_memattn_kernel, heads=w // HEAD_M),
        grid=(nseq * per_seq,),
        in_specs=[pl.BlockSpec((tq, w), lambda i: (i, q_col)), mem_spec, mem_spec],
        out_specs=pl.BlockSpec((tq, w), lambda i: (i, 0)),
        out_shape=jax.ShapeDtypeStruct((nseq * t, w), BF16),
        compiler_params=_params("parallel"),
        name="memory_attention",
    )(proj, mk, mv)


def _cmlp_kernel(u_ref, v_ref, z_ref, ws_ref, bs_ref, lnw_ref, lnb_ref, o_ref, vn_ref, *, clen, chunks):
    v = v_ref[...]
    mu = jnp.mean(v, axis=-1, keepdims=True)
    var = jnp.mean(jnp.square(v - mu), axis=-1, keepdims=True)
    vn = (v - mu) * lax.rsqrt(var + NORM_EPS) * lnw_ref[...] + lnb_ref[...]
    vn_ref[...] = vn
    vn16 = vn.astype(BF16)
    groups = vn.shape[1] // GROUP_C
    row = lax.broadcasted_iota(jnp.int32, (clen, clen), 0)
    col = lax.broadcasted_iota(jnp.int32, (clen, clen), 1)
    for g in range(groups):
        wm = jnp.where(row >= col, ws_ref[g], 0.0).astype(BF16)
        cs = slice(g * GROUP_C, (g + 1) * GROUP_C)
        for c in range(chunks):
            rs = slice(c * clen, (c + 1) * clen)
            s = _dot(wm, vn16[rs, cs]) + bs_ref[:, cs]
            o_ref[rs, cs] = (u_ref[rs, cs] * s * jax.nn.silu(z_ref[rs, cs])).astype(o_ref.dtype)


def chunk_mlp(proj, u_col, ws, bs_full, ln_w, ln_b, rows, clen, chunks):
    w = ln_w.shape[0]
    tm = clen * chunks
    col = lambda c: pl.BlockSpec((tm, w), lambda i: (i, c))
    const2 = lambda a: pl.BlockSpec(a.shape, lambda i: (0, 0))
    out = pl.BlockSpec((tm, w), lambda i: (i, 0))
    return pl.pallas_call(
        functools.partial(_cmlp_kernel, clen=clen, chunks=chunks),
        grid=(rows // tm,),
        in_specs=[col(u_col), col(u_col + 1), col(u_col + 2),
                  pl.BlockSpec(ws.shape, lambda i: (0, 0, 0)), const2(bs_full),
                  pl.BlockSpec((1, w), lambda i: (0, 0)), pl.BlockSpec((1, w), lambda i: (0, 0))],
        out_specs=[out, out],
        out_shape=[jax.ShapeDtypeStruct((rows, w), BF16), jax.ShapeDtypeStruct((rows, w), F32)],
        compiler_params=_params("parallel"),
        name="chunk_mlp",
    )(proj, proj, proj, ws, bs_full, ln_w.reshape(1, w), ln_b.reshape(1, w))


SB_HEADS_PER_STEP = 2
SB_BLOCK = 256
SB_GROUP = 2


def _sb_group(q16, k16, v16, tk, u, masks, carry):
    tq = q16.shape[0]
    nb = k16.shape[0] // tk
    z2 = _dot_nt(q16, k16) * (math.log2(math.e) / math.sqrt(HEAD_B))
    neg_l1m = jnp.maximum(z2, 0.0) + jnp.log2(1.0 + jnp.exp2(-jnp.abs(z2)))
    log_beta = z2 - neg_l1m
    nl = []
    for j in range(nb):
        nl_j = neg_l1m[:, j * tk:(j + 1) * tk]
        nl.append(jnp.where(masks[j], nl_j, 0.0) if j in masks else nl_j)
    lhs = jnp.concatenate([jnp.concatenate(_split2(nl_j), axis=1) for nl_j in nl], axis=0)
    cs = _dot(lhs, u)
    att = [None] * nb
    for j in reversed(range(nb)):
        cs_j = cs[j * tq:(j + 1) * tq]
        att_j = jnp.exp2(log_beta[:, j * tk:(j + 1) * tk] - cs_j - carry)
        att[j] = (jnp.where(masks[j], att_j, 0.0) if j in masks else att_j).astype(BF16)
        carry = carry + (cs_j[:, 0:1] + nl[j][:, 0:1])
    return _dot(jnp.concatenate(att, axis=1), v16), carry


def _head_slices(ref):
    return [slice(i * HEAD_B, (i + 1) * HEAD_B) for i in range(ref.shape[1] // HEAD_B)]


def _sb_prompt_kernel(*refs, tq, nq):
    q_ref, z_ref, k_ref, v_ref, u_ref = refs[:5]
    o_ref, kout_ref, vout_ref = refs[-3:]
    qi = pl.program_id(2)

    @pl.when(qi == 0)
    def _emit_kv():
        kout_ref[...] = k_ref[...]
        vout_ref[...] = v_ref[...]

    row = lax.broadcasted_iota(jnp.int32, (tq, tq), 0)
    col = lax.broadcasted_iota(jnp.int32, (tq, tq), 1)
    q_pos = qi * tq + row
    for grp in range(1, pl.cdiv(nq, SB_GROUP) + 1):
        nb = min(grp * SB_GROUP, nq)

        @pl.when(qi // SB_GROUP + 1 == grp)
        def _sweep():
            masks = {j: (j * tq + col) < q_pos for j in range((grp - 1) * SB_GROUP, nb)}
            for s in _head_slices(q_ref):
                out, _ = _sb_group(q_ref[:, s].astype(BF16), k_ref[0:nb * tq, s].astype(BF16),
                                   v_ref[0:nb * tq, s].astype(BF16), tq, u_ref[...], masks, jnp.zeros((tq, 1), F32))
                o_ref[:, s] = (out * jax.nn.silu(z_ref[:, s])).astype(o_ref.dtype)


def _sb_sample_kernel(*refs, tk):
    q_ref, z_ref, kn_ref, vn_ref, kp_ref, vp_ref, ud_ref, uf_ref = refs[:8]
    o_ref, kout_ref, vout_ref = refs[-3:]
    kout_ref[...] = kn_ref[...]
    vout_ref[...] = vn_ref[...]
    tq = q_ref.shape[0]
    row = lax.broadcasted_iota(jnp.int32, (tq, tq), 0)
    col = lax.broadcasted_iota(jnp.int32, (tq, tq), 1)
    for s in _head_slices(q_ref):
        q16 = q_ref[:, s].astype(BF16)
        out_new, carry = _sb_group(q16, kn_ref[:, s].astype(BF16), vn_ref[:, s].astype(BF16), tq, ud_ref[...],
                                   {0: col < row}, jnp.zeros((tq, 1), F32))
        out_past, _ = _sb_group(q16, kp_ref[:, s].astype(BF16), vp_ref[:, s].astype(BF16), tk, uf_ref[...],
                                {}, carry)
        o_ref[:, s] = ((out_new + out_past) * jax.nn.silu(z_ref[:, s])).astype(o_ref.dtype)


def _suffix_matrix(tk):
    s_later = lax.broadcasted_iota(jnp.int32, (tk, tk), 0)
    s_here = lax.broadcasted_iota(jnp.int32, (tk, tk), 1)
    u = (s_later > s_here).astype(BF16)
    return jnp.concatenate([u, u], axis=0)


def stick_breaking(proj, q_col, z_col, k_col, v_col, past_k, past_v, nseq, t, heads, layer, depth, kv_out):
    hp = SB_HEADS_PER_STEP
    wide = hp * HEAD_B
    tq = min(SB_BLOCK, t)
    nq = t // tq
    blk = lambda col: pl.BlockSpec((tq, wide), lambda b, h, i: (b * nq + i, col // hp + h))
    seq = lambda rows, col: pl.BlockSpec((rows, wide), lambda b, h, i: (b, col // hp + h))
    const = lambda a: pl.BlockSpec(a.shape, lambda b, h, i: (0, 0))
    if past_k is None:
        u = _suffix_matrix(tq)
        body = functools.partial(_sb_prompt_kernel, tq=tq, nq=nq)
        in_specs = [blk(q_col), blk(z_col), seq(t, k_col), seq(t, v_col), const(u)]
        args = [proj, proj, proj, proj, u]
    else:
        assert nq == 1
        n_past = past_k.shape[0] // nseq
        ud, uf = _suffix_matrix(tq), _suffix_matrix(SB_BLOCK)
        body = functools.partial(_sb_sample_kernel, tk=SB_BLOCK)
        in_specs = [blk(q_col), blk(z_col), seq(t, k_col), seq(t, v_col), seq(n_past, 0), seq(n_past, 0),
                    const(ud), const(uf)]
        args = [proj, proj, proj, proj, past_k, past_v, ud, uf]
    aliases = {}
    if kv_out is not None:
        aliases = {len(args): 1, len(args) + 1: 2}
        in_specs = in_specs + [pl.BlockSpec(memory_space=pl.ANY)] * 2
        args = args + list(kv_out)
    kv_shape = jax.ShapeDtypeStruct((depth, nseq * t, heads * HEAD_B), F32)
    kv_spec = pl.BlockSpec((None, t, wide), lambda b, h, i: (layer, b, h))
    o_b, k_all, v_all = pl.pallas_call(
        body,
        grid=(nseq, heads // hp, nq),
        in_specs=in_specs,
        out_specs=[pl.BlockSpec((tq, wide), lambda b, h, i: (b * nq + i, h)), kv_spec, kv_spec],
        out_shape=[jax.ShapeDtypeStruct((nseq * t, heads * HEAD_B), BF16), kv_shape, kv_shape],
        input_output_aliases=aliases,
        compiler_params=_params("parallel", "parallel", "arbitrary"),
        name="stick_breaking",
    )(*args)
    return o_b, (k_all, v_all)


def _block_stack(x, lane_masks):
    return jnp.concatenate([jnp.where(m, x, 0.0) for m in lane_masks], axis=0)


def _rwkv_kernel(x_ref, lora_ref, z_ref, sh_main_ref, sh_lora_ref, s0_ref,
                 mu_main_ref, mu_lora_ref, w0_ref, wup_ref, a0_ref, aup_ref, kk_ref, ka_ref, rk_ref,
                 gnw_ref, gnb_ref, e_ref, bd_ref, ltri_ref,
                 o_ref, sout_ref, shm_out_ref, shl_out_ref,
                 xbuf, lbuf, sbd, *, chunk, width):
    c = pl.program_id(1)
    n_chunks = pl.num_programs(1)
    C, W = chunk, width
    n_slabs = W // SLAB

    @pl.when(c == 0)
    def _init():
        xbuf[7:8, :] = sh_main_ref[0]
        lbuf[7:8, :] = sh_lora_ref[0]
        sbd[...] = jnp.zeros_like(sbd)
        for hd in range(W // HEAD_A):
            g, j = divmod(hd, GROUP_A)
            sbd[g, j * HEAD_A:(j + 1) * HEAD_A, j * HEAD_A:(j + 1) * HEAD_A] = s0_ref[0, hd]

    x = x_ref[...]
    xbuf[8:8 + C, :] = x
    xs = x + mu_main_ref[...] * (xbuf[7:7 + C, :] - x)
    xbuf[7:8, :] = x[C - 1:C, :]
    lo_x = lora_ref[...]
    lbuf[8:8 + C, :] = lo_x
    lo_s = lo_x + mu_lora_ref[...] * (lbuf[7:7 + C, :] - lo_x)
    lbuf[7:8, :] = lo_x[C - 1:C, :]

    r, k, v = xs[:, :W], xs[:, W:2 * W], xs[:, 2 * W:]
    w_pre = w0_ref[...] + _dot(jnp.tanh(lo_s).astype(BF16), wup_ref[...])
    ld = -jnp.exp(-_softplus(-w_pre) - 0.5)
    a = jax.nn.sigmoid(a0_ref[...] + _dot(lo_s.astype(BF16), aup_ref[...]))

    e_mat = e_ref[...]

    def seg_sum(val):
        parts = []
        for g in range(n_slabs):
            hi, lo = _split2(val[:, g * SLAB:(g + 1) * SLAB])
            parts.append(_dot(hi, e_mat) + _dot(lo, e_mat))
        return jnp.concatenate(parts, axis=1)

    kk = k * kk_ref[...]
    kk = kk / jnp.maximum(jnp.sqrt(seg_sum(kk * kk)), 1e-12)
    kmod = k * (1.0 + (a - 1.0) * ka_ref[...])

    ld_hi = ld.astype(BF16)
    ld_r1 = ld - ld_hi.astype(F32)
    ld_mid = ld_r1.astype(BF16)
    ld_lo = (ld_r1 - ld_mid.astype(F32)).astype(BF16)
    ltri = ltri_ref[...]
    lp = _dot(ltri, ld_hi) + _dot(ltri, ld_mid) + _dot(ltri, ld_lo)
    lp_end = lp[C - 1:C, :]
    e_neg = jnp.exp(-lp)
    kka = kk * a
    kap = kk * jnp.exp(lp - ld)
    bet = kka * e_neg
    kt = kmod * e_neg
    rt = r * jnp.exp(lp)
    e_end = jnp.exp(lp_end - lp)
    kt_end = kmod * e_end
    bet_end = kka * e_end
    dec_end = jnp.exp(lp_end)

    lane = lax.broadcasted_iota(jnp.int32, (1, SLAB), 1)
    lane_masks = [(lane >= j * HEAD_A) & (lane < (j + 1) * HEAD_A) for j in range(GROUP_A)]
    t_row = lax.broadcasted_iota(jnp.int32, (C, GROUP_A * C), 0)
    s_col = lax.broadcasted_iota(jnp.int32, (C, GROUP_A * C), 1) & (C - 1)
    strict, incl = s_col < t_row, s_col <= t_row
    stack = lambda val16: _block_stack(val16, lane_masks)
    nh = GROUP_A * C
    slabs = [slice(g * SLAB, (g + 1) * SLAB) for g in range(n_slabs)]
    per_slab = lambda fn: [fn(g, sl) for g, sl in enumerate(slabs)]

    kap16, rt16, bet16, kt16, v16 = (val.astype(BF16) for val in (kap, rt, bet, kt, v))
    s_old = per_slab(lambda g, sl: sbd[g])
    lhs = per_slab(lambda g, sl: jnp.concatenate([kap16[:, sl], rt16[:, sl]], axis=0))
    rhs = per_slab(lambda g, sl: jnp.concatenate([stack(bet16[:, sl]), stack(kt16[:, sl])], axis=0))
    sc = per_slab(lambda g, sl: _dot_nt(lhs[g], rhs[g]))
    ls = per_slab(lambda g, sl: _dot_nt(lhs[g], s_old[g].astype(BF16)))
    v_stack = per_slab(lambda g, sl: stack(v16[:, sl]))
    p16 = per_slab(lambda g, sl: jnp.where(strict, -sc[g][:C, :nh], 0.0).astype(BF16))
    xw = per_slab(lambda g, sl: ls[g][:C]
                  + _dot(jnp.where(strict, sc[g][:C, nh:], 0.0).astype(BF16), v_stack[g]))
    for step in range(6):
        xw = per_slab(lambda g, sl: xw[g] + _dot(p16[g], stack(xw[g].astype(BF16))))
        if step < 5:
            p16 = per_slab(lambda g, sl: _dot(p16[g], stack(p16[g])).astype(BF16))
    ab_inc = per_slab(lambda g, sl: jnp.concatenate(
        [jnp.where(incl, sc[g][C:, :nh], 0.0), jnp.where(incl, sc[g][C:, nh:], 0.0)], axis=1).astype(BF16))
    y = per_slab(lambda g, sl: ls[g][C:] + _dot(
        ab_inc[g], jnp.concatenate([stack((-xw[g]).astype(BF16)), v_stack[g]], axis=0)))
    upd = per_slab(lambda g, sl: _dot(
        jnp.concatenate([v[:, sl], -xw[g]], axis=0).T.astype(BF16),
        jnp.concatenate([kt_end[:, sl], bet_end[:, sl]], axis=0).astype(BF16)))
    for g, sl in enumerate(slabs):
        sbd[g] = s_old[g] * dec_end[:, sl] + upd[g] * bd_ref[...]

    y = jnp.concatenate(y, axis=1)
    inv_n = 1.0 / HEAD_A
    mean = seg_sum(y) * inv_n
    yc = y - mean
    var = seg_sum(yc * yc) * inv_n
    y = yc * lax.rsqrt(var + GN_EPS) * gnw_ref[...] + gnb_ref[...]
    y = y + seg_sum(r * kmod * rk_ref[...]) * v
    o_ref[...] = (y * jax.nn.silu(z_ref[...])).astype(o_ref.dtype)

    @pl.when(c == n_chunks - 1)
    def _fin():
        shm_out_ref[0] = xbuf[7:8, :]
        shl_out_ref[0] = lbuf[7:8, :]
        for hd in range(W // HEAD_A):
            g, j = divmod(hd, GROUP_A)
            sout_ref[0, hd] = sbd[g, j * HEAD_A:(j + 1) * HEAD_A, j * HEAD_A:(j + 1) * HEAD_A]


def rwkv7(proj, lora, z_col, shift_main, shift_lora, s0, prm, nseq, t):
    W = prm["a_w0"].shape[0]
    C = RWKV_CHUNK
    nc = t // C
    n_heads = W // HEAD_A
    lw = lora.shape[1]
    half = lw // 2
    zpad = jnp.zeros((half, W), F32)
    wup = jnp.concatenate([prm["a_w_up"], zpad], axis=0).astype(BF16)
    aup = jnp.concatenate([zpad, prm["a_a_up"]], axis=0).astype(BF16)
    idx = jnp.arange(SLAB) // HEAD_A
    same = idx[:, None] == idx[None, :]
    ti = jnp.arange(C)
    ltri = (ti[:, None] >= ti[None, :]).astype(BF16)
    row1 = lambda a: a.reshape(1, -1)
    vec = lambda n: pl.BlockSpec((1, n), lambda b, c: (0, 0))
    full = lambda a: pl.BlockSpec(a.shape, lambda b, c: (0, 0))
    per_seq = lambda n: pl.BlockSpec((1, 1, n), lambda b, c: (b, 0, 0))
    state = pl.BlockSpec((1, n_heads, HEAD_A, HEAD_A), lambda b, c: (b, 0, 0, 0))
    e_mat, bd_mask = same.astype(BF16), same.astype(F32)
    return pl.pallas_call(
        functools.partial(_rwkv_kernel, chunk=C, width=W),
        grid=(nseq, nc),
        in_specs=[pl.BlockSpec((C, 3 * W), lambda b, c: (b * nc + c, 0)),
                  pl.BlockSpec((C, lw), lambda b, c: (b * nc + c, 0)),
                  pl.BlockSpec((C, W), lambda b, c: (b * nc + c, z_col)),
                  per_seq(3 * W), per_seq(lw), state,
                  vec(3 * W), vec(lw), vec(W), full(wup), vec(W), full(aup), vec(W), vec(W), vec(W), vec(W), vec(W),
                  full(e_mat), full(bd_mask), full(ltri)],
        out_specs=[pl.BlockSpec((C, W), lambda b, c: (b * nc + c, 0)), state, per_seq(3 * W), per_seq(lw)],
        out_shape=[jax.ShapeDtypeStruct((nseq * t, W), BF16),
                   jax.ShapeDtypeStruct((nseq, n_heads, HEAD_A, HEAD_A), F32),
                   jax.ShapeDtypeStruct((nseq, 1, 3 * W), F32),
                   jax.ShapeDtypeStruct((nseq, 1, lw), F32)],
        scratch_shapes=[pltpu.VMEM((C + 8, 3 * W), F32), pltpu.VMEM((C + 8, lw), F32),
                        pltpu.VMEM((W // SLAB, SLAB, SLAB), F32)],
        compiler_params=_params("parallel", "arbitrary"),
        name="rwkv7",
    )(proj, lora, proj, shift_main, shift_lora, s0,
      row1(prm["a_mu"][:3 * W]), row1(prm["a_mu"][3 * W:]), row1(prm["a_w0"]), wup, row1(prm["a_a0"]), aup,
      row1(prm["a_k_k"]), row1(prm["a_k_a"]), row1(prm["a_r_k"]), row1(prm["a_gn_w"]), row1(prm["a_gn_b"]),
      e_mat, bd_mask, ltri)


COL_AZ, COL_BQ, COL_BK, COL_BV, COL_BZ, COL_CU, COL_MQ = 3, 4, 5, 6, 7, 8, 11


def trunk_layer(x, nseq, t, mk, mv, shift_prev, s_prev, past_k, past_v, prm, layer, depth, kv_out):
    m, d = x.shape
    W = d // 4
    tm = min(512, m)
    h = rmsnorm_rows(x, prm["g_pre"], min(256, m))
    proj = matmul(h, prm["w_in_main"], tm, 1024)
    lora = matmul(h, prm["w_in_lora"], tm, prm["w_in_lora"].shape[1])

    o_a, s_new, sh_main, sh_lora = rwkv7(proj, lora, COL_AZ, shift_prev[..., :3 * W], shift_prev[..., 3 * W:],
                                         s_prev, prm, nseq, t)
    shift_new = jnp.concatenate([sh_main, sh_lora], axis=-1)

    heads_b = W // HEAD_B
    hb = lambda col: col * heads_b
    o_b, kv_out = stick_breaking(proj, hb(COL_BQ), hb(COL_BZ), hb(COL_BK), hb(COL_BV), past_k, past_v,
                                 nseq, t, heads_b, layer, depth, kv_out)

    clen = min(t, prm["c_ws"].shape[1])
    groups = prm["c_ws"].shape[0]
    bs_full = jnp.repeat(prm["c_bs"][:, :clen].T, W // groups, axis=1)
    o_c, vn_c = chunk_mlp(proj, COL_CU, prm["c_ws"][:, :clen, :clen], bs_full, prm["c_ln_w"], prm["c_ln_b"],
                          m, clen, max(1, min(512, t) // clen))

    o_m = memory_attention(proj, COL_MQ, mk, mv, nseq, t, min(512, t))

    merged = gated_merge(h, (o_a, o_b, o_c, o_m), prm["w_gate"], prm["b_gate"], prm["w_br"], tm, 512)
    out = matmul(merged, prm["w_out"], tm, 1024)
    x_new = post_norm_residual(x, out, prm["g_post"], min(256, m))
    return x_new, shift_new, s_new, kv_out, vn_c


def kernel(x_prompt, x_sample, cache_mem_k, cache_mem_v, cache_sb_k, cache_sb_v, state_rwkv, state_shift, mem_prompt, g_pre, g_post, w_in, a_mu, a_w0, a_w_up, a_a0, a_a_up, a_k_k, a_k_a, a_r_k, a_gn_w, a_gn_b, c_ws, c_bs, c_ln_w, c_ln_b, g_mem, w_mem_kv, w_gate, b_gate, w_br, w_out):
    bp, tp, d = x_prompt.shape
    bs_, ts, _ = x_sample.shape
    depth = w_in.shape[0]
    W = d // 4
    n_mem = mem_prompt.shape[1]
    m_heads = cache_mem_k.shape[3]
    b_heads = cache_sb_k.shape[3]
    a_heads = state_rwkv.shape[2]
    shift_w = state_shift.shape[-1]
    n_past = cache_sb_k.shape[2]
    lora_lo, lora_hi = 3 * W, shift_w

    yp = x_prompt.reshape(bp * tp, d)
    ys = x_sample.reshape(bs_ * ts, d)
    mem_rows = mem_prompt.reshape(bp * n_mem, d)
    mem_out, rw_p, sh_p, rw_s, sh_s, cv_s = [[], []], [], [], [], [], []
    kv_p = kv_s = None
    for l in range(depth):
        prm = {
            "g_pre": g_pre[l], "g_post": g_post[l],
            "w_in_main": jnp.concatenate([w_in[l][:, :lora_lo], w_in[l][:, lora_hi:]], axis=1).astype(BF16),
            "w_in_lora": w_in[l][:, lora_lo:lora_hi].astype(BF16),
            "a_mu": a_mu[l], "a_w0": a_w0[l], "a_w_up": a_w_up[l], "a_a0": a_a0[l], "a_a_up": a_a_up[l],
            "a_k_k": a_k_k[l].reshape(-1), "a_k_a": a_k_a[l].reshape(-1), "a_r_k": a_r_k[l].reshape(-1),
            "a_gn_w": a_gn_w[l].reshape(-1), "a_gn_b": a_gn_b[l].reshape(-1),
            "c_ws": c_ws[l], "c_bs": c_bs[l], "c_ln_w": c_ln_w[l], "c_ln_b": c_ln_b[l],
            "w_gate": w_gate[l].astype(BF16), "b_gate": b_gate[l], "w_br": w_br[l].astype(BF16),
            "w_out": w_out[l].astype(BF16),
        }
        kv = matmul(rmsnorm_rows(mem_rows, g_mem[l], 256), w_mem_kv[l].astype(BF16), 512, 1024)
        mk = kv[:, :W].reshape(bp, n_mem, W)
        mv = kv[:, W:].reshape(bp, n_mem, W)
        shift0 = jnp.zeros((bp, 1, shift_w), F32)
        s0 = jnp.zeros((bp, a_heads, HEAD_A, HEAD_A), F32)
        yp, sh, st, kv_p, _ = trunk_layer(yp, bp, tp, mk.astype(BF16), mv.astype(BF16), shift0, s0, None, None, prm,
                                          l, depth, kv_p)
        mem_out[0].append(mk.reshape(bp, n_mem, m_heads, W // m_heads))
        mem_out[1].append(mv.reshape(bp, n_mem, m_heads, W // m_heads))
        rw_p.append(st)
        sh_p.append(sh)
        ys, sh, st, kv_s, cvn = trunk_layer(
            ys, bs_, ts, cache_mem_k[l].reshape(bs_, n_mem, W).astype(BF16),
            cache_mem_v[l].reshape(bs_, n_mem, W).astype(BF16), state_shift[l], state_rwkv[l],
            cache_sb_k[l].reshape(bs_ * n_past, W), cache_sb_v[l].reshape(bs_ * n_past, W), prm, l, depth, kv_s)
        rw_s.append(st)
        sh_s.append(sh)
        cv_s.append(cvn.reshape(bs_, ts, W))
    heads_p = lambda a: a.reshape(depth, bp, tp, b_heads, HEAD_B)
    heads_s = lambda a: a.reshape(depth, bs_, ts, b_heads, HEAD_B)
    return (yp.reshape(bp, tp, d), ys.reshape(bs_, ts, d), jnp.stack(mem_out[0]), jnp.stack(mem_out[1]),
            heads_p(kv_p[0]), heads_p(kv_p[1]), jnp.stack(rw_p), jnp.stack(sh_p),
            heads_s(kv_s[0]), heads_s(kv_s[1]), jnp.stack(rw_s), jnp.stack(sh_s), jnp.stack(cv_s))
```

```python
import functools
import math

import jax
import jax.numpy as jnp
from jax import lax
from jax.experimental import pallas as pl
from jax.experimental.pallas import tpu as pltpu

F32 = jnp.float32
BF16 = jnp.bfloat16

NORM_EPS = 1e-6
GN_EPS = 64e-5
HEAD_A = 64
GROUP_A = 4
SLAB = HEAD_A * GROUP_A
HEAD_B = 128
HEAD_M = 256
GROUP_C = 128
RWKV_CHUNK = 64
VMEM_LIMIT_BYTES = 56 * 1024 * 1024


def _params(*sem):
    return pltpu.CompilerParams(dimension_semantics=sem, vmem_limit_bytes=VMEM_LIMIT_BYTES)


def _softplus(x):
    return jnp.maximum(x, 0.0) + jnp.log1p(jnp.exp(-jnp.abs(x)))


def _split2(x):
    hi = x.astype(BF16)
    lo = (x - hi.astype(F32)).astype(BF16)
    return hi, lo


def _dot(a, b):
    return jnp.dot(a, b, preferred_element_type=F32)


def _dot_nt(a, b):
    return lax.dot_general(a, b, (((1,), (1,)), ((), ())), preferred_element_type=F32)


def _rmsnorm_kernel(x_ref, g_ref, o_ref):
    x = x_ref[...]
    ms = jnp.mean(x * x, axis=-1, keepdims=True)
    o_ref[...] = (x * lax.rsqrt(ms + NORM_EPS) * g_ref[...]).astype(o_ref.dtype)


def rmsnorm_rows(x, g, tm):
    m, d = x.shape
    return pl.pallas_call(
        _rmsnorm_kernel,
        grid=(m // tm,),
        in_specs=[pl.BlockSpec((tm, d), lambda i: (i, 0)), pl.BlockSpec((1, d), lambda i: (0, 0))],
        out_specs=pl.BlockSpec((tm, d), lambda i: (i, 0)),
        out_shape=jax.ShapeDtypeStruct((m, d), BF16),
        compiler_params=_params("parallel"),
        name="rmsnorm_rows",
    )(x, g.reshape(1, d))


POST_ROWS = 64


def _out_proj_kernel(m_ref, w_ref, x_ref, g_ref, o_ref, x_rows, *, tn):
    j = pl.program_id(1)
    cols = pl.ds(pl.multiple_of(j * tn, tn), tn)
    o_ref[:, cols] = _dot(m_ref[...], w_ref[...])
    x_rows[:, cols] = x_ref[...]

    @pl.when(j == pl.num_programs(1) - 1)
    def _normalise():
        for r0 in range(0, o_ref.shape[0], POST_ROWS):
            rows = slice(r0, r0 + POST_ROWS)
            y = o_ref[rows, :]
            ms = jnp.mean(y * y, axis=-1, keepdims=True)
            o_ref[rows, :] = x_rows[rows, :] + y * lax.rsqrt(ms + NORM_EPS) * g_ref[...]


def out_proj_norm_residual(merged, w, x, g, tm, tn):
    m, d = x.shape
    return pl.pallas_call(
        functools.partial(_out_proj_kernel, tn=tn),
        grid=(m // tm, d // tn),
        in_specs=[pl.BlockSpec((tm, d), lambda i, j: (i, 0)), pl.BlockSpec((d, tn), lambda i, j: (0, j)),
                  pl.BlockSpec((tm, tn), lambda i, j: (i, j)), pl.BlockSpec((1, d), lambda i, j: (0, 0))],
        out_specs=pl.BlockSpec((tm, d), lambda i, j: (i, 0)),
        out_shape=jax.ShapeDtypeStruct((m, d), F32),
        scratch_shapes=[pltpu.VMEM((tm, d), F32)],
        compiler_params=_params("parallel", "arbitrary"),
        name="out_proj_norm_residual",
    )(merged, w, x, g.reshape(1, d))


def _matmul_kernel(x_ref, w_ref, o_ref):
    o_ref[...] = _dot(x_ref[...], w_ref[...]).astype(o_ref.dtype)


def matmul(x, w, tm, tn, out_dtype=F32):
    m, k = x.shape
    n = w.shape[1]
    return pl.pallas_call(
        _matmul_kernel,
        grid=(n // tn, m // tm),
        in_specs=[pl.BlockSpec((tm, k), lambda j, i: (i, 0)), pl.BlockSpec((k, tn), lambda j, i: (0, j))],
        out_specs=pl.BlockSpec((tm, tn), lambda j, i: (i, j)),
        out_shape=jax.ShapeDtypeStruct((m, n), out_dtype),
        compiler_params=_params("parallel", "parallel"),
        name="matmul",
    )(x, w)


def _merge_kernel(h_ref, oa_ref, ob_ref, oc_ref, om_ref, wg_ref, bg_ref, wbr_ref, out_ref):
    h = h_ref[...]
    acc = None
    for n, o_ref in enumerate((oa_ref, ob_ref, oc_ref, om_ref)):
        gate = jax.nn.sigmoid(_dot(h, wg_ref[n]) + bg_ref[n])
        term = gate * _dot(o_ref[...], wbr_ref[n])
        acc = term if acc is None else acc + term
    out_ref[...] = acc.astype(out_ref.dtype)


def gated_merge(h, branches, wg, bg, wbr, tm, tn):
    m, d = h.shape
    nb, w, _ = wbr.shape
    resident = dict(pipeline_mode=pl.Buffered(1)) if m > tm else {}
    o_spec = pl.BlockSpec((tm, w), lambda j, i: (i, 0))
    return pl.pallas_call(
        _merge_kernel,
        grid=(d // tn, m // tm),
        in_specs=[pl.BlockSpec((tm, d), lambda j, i: (i, 0)), o_spec, o_spec, o_spec, o_spec,
                  pl.BlockSpec((nb, d, tn), lambda j, i: (0, 0, j), **resident),
                  pl.BlockSpec((nb, 1, tn), lambda j, i: (0, 0, j)),
                  pl.BlockSpec((nb, w, tn), lambda j, i: (0, 0, j), **resident)],
        out_specs=pl.BlockSpec((tm, tn), lambda j, i: (i, j)),
        out_shape=jax.ShapeDtypeStruct((m, d), BF16),
        compiler_params=_params("parallel", "parallel"),
        name="gated_merge",
    )(h, *branches, wg, bg.reshape(nb, 1, d), wbr)


def _memattn_kernel(q_ref, mk_ref, mv_ref, o_ref, *, heads):
    scale = 1.0 / math.sqrt(HEAD_M)
    for hd in range(heads):
        sl = slice(hd * HEAD_M, (hd + 1) * HEAD_M)
        q = q_ref[:, sl].astype(BF16)
        s = _dot_nt(q, mk_ref[0, :, sl]) * scale
        s = s - jnp.max(s, axis=-1, keepdims=True)
        p = jnp.exp(s)
        p = p / jnp.sum(p, axis=-1, keepdims=True)
        o_ref[:, sl] = _dot(p.astype(BF16), mv_ref[0, :, sl]).astype(o_ref.dtype)


def memory_attention(proj, q_col, mk, mv, nseq, t, tq):
    n_mem, w = mk.shape[1:]
    per_seq = t // tq
    mem_spec = pl.BlockSpec((1, n_mem, w), lambda i: (i // per_seq, 0, 0))
    return pl.pallas_call(
        functools.partial(_memattn_kernel, heads=w // HEAD_M),
        grid=(nseq * per_seq,),
        in_specs=[pl.BlockSpec((tq, w), lambda i: (i, q_col)), mem_spec, mem_spec],
        out_specs=pl.BlockSpec((tq, w), lambda i: (i, 0)),
        out_shape=jax.ShapeDtypeStruct((nseq * t, w), BF16),
        compiler_params=_params("parallel"),
        name="memory_attention",
    )(proj, mk, mv)


def _cmlp_kernel(u_ref, v_ref, z_ref, ws_ref, bs_ref, lnw_ref, lnb_ref, o_ref, *maybe_vn_ref, clen, chunks):
    v = v_ref[...]
    mu = jnp.mean(v, axis=-1, keepdims=True)
    var = jnp.mean(jnp.square(v - mu), axis=-1, keepdims=True)
    vn = (v - mu) * lax.rsqrt(var + NORM_EPS) * lnw_ref[...] + lnb_ref[...]
    for vn_ref in maybe_vn_ref:
        vn_ref[...] = vn
    vn16 = vn.astype(BF16)
    groups = vn.shape[1] // GROUP_C
    row = lax.broadcasted_iota(jnp.int32, (clen, clen), 0)
    col = lax.broadcasted_iota(jnp.int32, (clen, clen), 1)
    for g in range(groups):
        wm = jnp.where(row >= col, ws_ref[g], 0.0).astype(BF16)
        cs = slice(g * GROUP_C, (g + 1) * GROUP_C)
        for c in range(chunks):
            rs = slice(c * clen, (c + 1) * clen)
            s = _dot(wm, vn16[rs, cs]) + bs_ref[:, cs]
            o_ref[rs, cs] = (u_ref[rs, cs] * s * jax.nn.silu(z_ref[rs, cs])).astype(o_ref.dtype)


def chunk_mlp(proj, u_col, ws, bs_full, ln_w, ln_b, rows, clen, chunks, want_vn):
    w = ln_w.shape[0]
    tm = clen * chunks
    col = lambda c: pl.BlockSpec((tm, w), lambda i: (i, c))
    const2 = lambda a: pl.BlockSpec(a.shape, lambda i: (0, 0))
    out = pl.BlockSpec((tm, w), lambda i: (i, 0))
    out_dtypes = [BF16, F32] if want_vn else [BF16]
    return pl.pallas_call(
        functools.partial(_cmlp_kernel, clen=clen, chunks=chunks),
        grid=(rows // tm,),
        in_specs=[col(u_col), col(u_col + 1), col(u_col + 2),
                  pl.BlockSpec(ws.shape, lambda i: (0, 0, 0)), const2(bs_full),
                  pl.BlockSpec((1, w), lambda i: (0, 0)), pl.BlockSpec((1, w), lambda i: (0, 0))],
        out_specs=[out] * len(out_dtypes),
        out_shape=[jax.ShapeDtypeStruct((rows, w), dt) for dt in out_dtypes],
        compiler_params=_params("parallel"),
        name="chunk_mlp",
    )(proj, proj, proj, ws, bs_full, ln_w.reshape(1, w), ln_b.reshape(1, w))


SB_HEADS_PER_STEP = 2
SB_BLOCK = 256
SB_GROUP = 2


def _sb_group(q16, k16, v16, tk, u, masks, carry):
    tq = q16.shape[0]
    nb = k16.shape[0] // tk
    z2 = _dot_nt(q16, k16) * (math.log2(math.e) / math.sqrt(HEAD_B))
    neg_l1m = jnp.maximum(z2, 0.0) + jnp.log2(1.0 + jnp.exp2(-jnp.abs(z2)))
    log_beta = z2 - neg_l1m
    nl = []
    for j in range(nb):
        nl_j = neg_l1m[:, j * tk:(j + 1) * tk]
        nl.append(jnp.where(masks[j], nl_j, 0.0) if j in masks else nl_j)
    lhs = jnp.concatenate([jnp.concatenate(_split2(nl_j), axis=1) for nl_j in nl], axis=0)
    cs = _dot(lhs, u)
    att = [None] * nb
    for j in reversed(range(nb)):
        cs_j = cs[j * tq:(j + 1) * tq]
        att_j = jnp.exp2(log_beta[:, j * tk:(j + 1) * tk] - cs_j - carry)
        att[j] = (jnp.where(masks[j], att_j, 0.0) if j in masks else att_j).astype(BF16)
        carry = carry + (cs_j[:, 0:1] + nl[j][:, 0:1])
    return _dot(jnp.concatenate(att, axis=1), v16), carry


def _head_slices(ref):
    return [slice(i * HEAD_B, (i + 1) * HEAD_B) for i in range(ref.shape[1] // HEAD_B)]


def _sb_prompt_kernel(*refs, tq, nq):
    q_ref, z_ref, k_ref, v_ref, u_ref = refs[:5]
    o_ref, kout_ref, vout_ref = refs[-3:]
    qi = pl.program_id(2)

    @pl.when(qi == 0)
    def _emit_kv():
        kout_ref[...] = k_ref[...]
        vout_ref[...] = v_ref[...]

    row = lax.broadcasted_iota(jnp.int32, (tq, tq), 0)
    col = lax.broadcasted_iota(jnp.int32, (tq, tq), 1)
    q_pos = qi * tq + row
    for grp in range(1, pl.cdiv(nq, SB_GROUP) + 1):
        nb = min(grp * SB_GROUP, nq)

        @pl.when(qi // SB_GROUP + 1 == grp)
        def _sweep():
            masks = {j: (j * tq + col) < q_pos for j in range((grp - 1) * SB_GROUP, nb)}
            for s in _head_slices(q_ref):
                out, _ = _sb_group(q_ref[:, s].astype(BF16), k_ref[0:nb * tq, s].astype(BF16),
                                   v_ref[0:nb * tq, s].astype(BF16), tq, u_ref[...], masks, jnp.zeros((tq, 1), F32))
                o_ref[:, s] = (out * jax.nn.silu(z_ref[:, s])).astype(o_ref.dtype)


def _sb_sample_kernel(*refs, tk):
    q_ref, z_ref, kn_ref, vn_ref, kp_ref, vp_ref, ud_ref, uf_ref = refs[:8]
    o_ref, kout_ref, vout_ref = refs[-3:]
    kout_ref[...] = kn_ref[...]
    vout_ref[...] = vn_ref[...]
    tq = q_ref.shape[0]
    row = lax.broadcasted_iota(jnp.int32, (tq, tq), 0)
    col = lax.broadcasted_iota(jnp.int32, (tq, tq), 1)
    for s in _head_slices(q_ref):
        q16 = q_ref[:, s].astype(BF16)
        out_new, carry = _sb_group(q16, kn_ref[:, s].astype(BF16), vn_ref[:, s].astype(BF16), tq, ud_ref[...],
                                   {0: col < row}, jnp.zeros((tq, 1), F32))
        out_past, _ = _sb_group(q16, kp_ref[:, s].astype(BF16), vp_ref[:, s].astype(BF16), tk, uf_ref[...],
                                {}, carry)
        o_ref[:, s] = ((out_new + out_past) * jax.nn.silu(z_ref[:, s])).astype(o_ref.dtype)


def _suffix_matrix(tk):
    s_later = lax.broadcasted_iota(jnp.int32, (tk, tk), 0)
    s_here = lax.broadcasted_iota(jnp.int32, (tk, tk), 1)
    u = (s_later > s_here).astype(BF16)
    return jnp.concatenate([u, u], axis=0)


def stick_breaking(proj, q_col, z_col, k_col, v_col, past_k, past_v, nseq, t, heads, layer, depth, kv_out):
    hp = SB_HEADS_PER_STEP
    wide = hp * HEAD_B
    tq = min(SB_BLOCK, t)
    nq = t // tq
    blk = lambda col: pl.BlockSpec((tq, wide), lambda b, h, i: (b * nq + i, col // hp + h))
    seq = lambda rows, col: pl.BlockSpec((rows, wide), lambda b, h, i: (b, col // hp + h))
    const = lambda a: pl.BlockSpec(a.shape, lambda b, h, i: (0, 0))
    if past_k is None:
        u = _suffix_matrix(tq)
        body = functools.partial(_sb_prompt_kernel, tq=tq, nq=nq)
        in_specs = [blk(q_col), blk(z_col), seq(t, k_col), seq(t, v_col), const(u)]
        args = [proj, proj, proj, proj, u]
    else:
        assert nq == 1
        n_past = past_k.shape[0] // nseq
        ud, uf = _suffix_matrix(tq), _suffix_matrix(SB_BLOCK)
        body = functools.partial(_sb_sample_kernel, tk=SB_BLOCK)
        in_specs = [blk(q_col), blk(z_col), seq(t, k_col), seq(t, v_col), seq(n_past, 0), seq(n_past, 0),
                    const(ud), const(uf)]
        args = [proj, proj, proj, proj, past_k, past_v, ud, uf]
    aliases = {}
    if kv_out is not None:
        aliases = {len(args): 1, len(args) + 1: 2}
        in_specs = in_specs + [pl.BlockSpec(memory_space=pl.ANY)] * 2
        args = args + list(kv_out)
    kv_shape = jax.ShapeDtypeStruct((depth, nseq * t, heads * HEAD_B), F32)
    kv_spec = pl.BlockSpec((None, t, wide), lambda b, h, i: (layer, b, h))
    o_b, k_all, v_all = pl.pallas_call(
        body,
        grid=(nseq, heads // hp, nq),
        in_specs=in_specs,
        out_specs=[pl.BlockSpec((tq, wide), lambda b, h, i: (b * nq + i, h)), kv_spec, kv_spec],
        out_shape=[jax.ShapeDtypeStruct((nseq * t, heads * HEAD_B), BF16), kv_shape, kv_shape],
        input_output_aliases=aliases,
        compiler_params=_params("parallel", "parallel", "arbitrary"),
        name="stick_breaking",
    )(*args)
    return o_b, (k_all, v_all)


def _block_stack(x, lane_masks):
    return jnp.concatenate([jnp.where(m, x, 0.0) for m in lane_masks], axis=0)


def _rwkv_kernel(x_ref, lora_ref, z_ref, sh_main_ref, sh_lora_ref, s0_ref,
                 mu_main_ref, mu_lora_ref, w0_ref, wup_ref, a0_ref, aup_ref, kk_ref, ka_ref, rk_ref,
                 gnw_ref, gnb_ref, e_ref, bd_ref, ltri_ref,
                 o_ref, sout_ref, shm_out_ref, shl_out_ref,
                 xbuf, lbuf, sbd, *, chunk, width):
    c = pl.program_id(1)
    n_chunks = pl.num_programs(1)
    C, W = chunk, width
    n_seq = x_ref.shape[0]
    n_slabs = W // SLAB
    n_heads = W // HEAD_A

    @pl.when(c == 0)
    def _init():
        sbd[...] = jnp.zeros_like(sbd)
        for s in range(n_seq):
            xbuf[s, 7:8, :] = sh_main_ref[s]
            lbuf[s, 7:8, :] = sh_lora_ref[s]
            for hd in range(n_heads):
                g, j = divmod(hd, GROUP_A)
                sbd[s, g, j * HEAD_A:(j + 1) * HEAD_A, j * HEAD_A:(j + 1) * HEAD_A] = s0_ref[s, hd]

    def shifted(ref, buf, mu_ref):
        rows = []
        for s in range(n_seq):
            x = ref[s]
            buf[s, 8:8 + C, :] = x
            rows.append(x + mu_ref[...] * (buf[s, 7:7 + C, :] - x))
            buf[s, 7:8, :] = x[C - 1:C, :]
        return jnp.concatenate(rows, axis=0)

    xs = shifted(x_ref, xbuf, mu_main_ref)
    lo_s = shifted(lora_ref, lbuf, mu_lora_ref)
    seq_rows = [slice(s * C, (s + 1) * C) for s in range(n_seq)]
    slabs = [slice(g * SLAB, (g + 1) * SLAB) for g in range(n_slabs)]

    r, k, v = xs[:, :W], xs[:, W:2 * W], xs[:, 2 * W:]
    w_pre = w0_ref[...] + _dot(jnp.tanh(lo_s).astype(BF16), wup_ref[...])
    ld = -jnp.exp(-_softplus(-w_pre) - 0.5)
    a = jax.nn.sigmoid(a0_ref[...] + _dot(lo_s.astype(BF16), aup_ref[...]))

    e_mat = e_ref[...]

    def seg_sum(val):
        n = val.shape[0]
        hi, lo = _split2(val)
        out = _dot(jnp.concatenate([part[:, sl] for part in (hi, lo) for sl in slabs], axis=0), e_mat)
        return jnp.concatenate([out[g * n:(g + 1) * n] + out[(n_slabs + g) * n:(n_slabs + g + 1) * n]
                                for g in range(n_slabs)], axis=1)

    kk = k * kk_ref[...]
    kk = kk / jnp.maximum(jnp.sqrt(seg_sum(kk * kk)), 1e-12)
    kmod = k * (1.0 + (a - 1.0) * ka_ref[...])

    ld_hi = ld.astype(BF16)
    ld_r1 = ld - ld_hi.astype(F32)
    ld_mid = ld_r1.astype(BF16)
    ld_lo = (ld_r1 - ld_mid.astype(F32)).astype(BF16)
    ltri = ltri_ref[...]
    lp = _dot(ltri, ld_hi) + _dot(ltri, ld_mid) + _dot(ltri, ld_lo)
    lp_last = [lp[rs.stop - 1:rs.stop, :] for rs in seq_rows]
    lp_end = jnp.concatenate([jnp.broadcast_to(row, (C, W)) for row in lp_last], axis=0)
    e_neg = jnp.exp(-lp)
    kka = kk * a
    kap = kk * jnp.exp(lp - ld)
    bet = kka * e_neg
    kt = kmod * e_neg
    rt = r * jnp.exp(lp)
    e_end = jnp.exp(lp_end - lp)
    kt_end = kmod * e_end
    bet_end = kka * e_end
    dec_end = [jnp.exp(row) for row in lp_last]

    lane = lax.broadcasted_iota(jnp.int32, (1, SLAB), 1)
    lane_masks = [(lane >= j * HEAD_A) & (lane < (j + 1) * HEAD_A) for j in range(GROUP_A)]
    t_row = lax.broadcasted_iota(jnp.int32, (C, GROUP_A * C), 0)
    s_col = lax.broadcasted_iota(jnp.int32, (C, GROUP_A * C), 1) & (C - 1)
    strict, incl = s_col < t_row, s_col <= t_row
    stack = lambda val16: _block_stack(val16, lane_masks)
    nh = GROUP_A * C
    units = [(s, g) for s in range(n_seq) for g in range(n_slabs)]
    per_slab = lambda fn: [fn(u, seq_rows[s], slabs[g]) for u, (s, g) in enumerate(units)]

    kap16, rt16, bet16, kt16, v16 = (val.astype(BF16) for val in (kap, rt, bet, kt, v))
    s_old = [sbd[s, g] for s, g in units]
    lhs = per_slab(lambda g, rs, sl: jnp.concatenate([kap16[rs, sl], rt16[rs, sl]], axis=0))
    rhs = per_slab(lambda g, rs, sl: jnp.concatenate([stack(bet16[rs, sl]), stack(kt16[rs, sl])], axis=0))
    sc = per_slab(lambda g, rs, sl: _dot_nt(lhs[g], rhs[g]))
    ls = per_slab(lambda g, rs, sl: _dot_nt(lhs[g], s_old[g].astype(BF16)))
    v_stack = per_slab(lambda g, rs, sl: stack(v16[rs, sl]))
    p16 = per_slab(lambda g, rs, sl: jnp.where(strict, -sc[g][:C, :nh], 0.0).astype(BF16))
    xw = per_slab(lambda g, rs, sl: ls[g][:C]
                  + _dot(jnp.where(strict, sc[g][:C, nh:], 0.0).astype(BF16), v_stack[g]))
    for step in range(6):
        xw = per_slab(lambda g, rs, sl: xw[g] + _dot(p16[g], stack(xw[g].astype(BF16))))
        if step < 5:
            p16 = per_slab(lambda g, rs, sl: _dot(p16[g], stack(p16[g])).astype(BF16))
    ab_inc = per_slab(lambda g, rs, sl: jnp.concatenate(
        [jnp.where(incl, sc[g][C:, :nh], 0.0), jnp.where(incl, sc[g][C:, nh:], 0.0)], axis=1).astype(BF16))
    y = per_slab(lambda g, rs, sl: ls[g][C:] + _dot(
        ab_inc[g], jnp.concatenate([stack((-xw[g]).astype(BF16)), v_stack[g]], axis=0)))
    upd = per_slab(lambda g, rs, sl: _dot(
        jnp.concatenate([v[rs, sl], -xw[g]], axis=0).T.astype(BF16),
        jnp.concatenate([kt_end[rs, sl], bet_end[rs, sl]], axis=0).astype(BF16)))
    for u, (s, g) in enumerate(units):
        sbd[s, g] = s_old[u] * dec_end[s][:, slabs[g]] + upd[u] * bd_ref[...]

    y = jnp.concatenate([jnp.concatenate(y[s * n_slabs:(s + 1) * n_slabs], axis=1) for s in range(n_seq)], axis=0)
    inv_n = 1.0 / HEAD_A
    mean = seg_sum(y) * inv_n
    yc = y - mean
    var = seg_sum(yc * yc) * inv_n
    y = yc * lax.rsqrt(var + GN_EPS) * gnw_ref[...] + gnb_ref[...]
    y = y + seg_sum(r * kmod * rk_ref[...]) * v
    for s, rs in enumerate(seq_rows):
        o_ref[s] = (y[rs] * jax.nn.silu(z_ref[s])).astype(o_ref.dtype)

    @pl.when(c == n_chunks - 1)
    def _fin():
        for s in range(n_seq):
            shm_out_ref[s] = xbuf[s, 7:8, :]
            shl_out_ref[s] = lbuf[s, 7:8, :]
            for hd in range(n_heads):
                g, j = divmod(hd, GROUP_A)
                sout_ref[s, hd] = sbd[s, g, j * HEAD_A:(j + 1) * HEAD_A, j * HEAD_A:(j + 1) * HEAD_A]


RWKV_SEQS_PER_STEP = 2


def rwkv7(proj, lora, z_col, shift_main, shift_lora, s0, prm, nseq, t):
    W = prm["a_w0"].shape[0]
    C = RWKV_CHUNK
    S = RWKV_SEQS_PER_STEP
    nc = t // C
    n_heads = W // HEAD_A
    lw = lora.shape[1]
    half = lw // 2
    zpad = jnp.zeros((half, W), F32)
    wup = jnp.concatenate([prm["a_w_up"], zpad], axis=0).astype(BF16)
    aup = jnp.concatenate([zpad, prm["a_a_up"]], axis=0).astype(BF16)
    idx = jnp.arange(SLAB) // HEAD_A
    same = idx[:, None] == idx[None, :]
    ti = jnp.arange(S * C)
    ltri = ((ti[:, None] >= ti[None, :]) & (ti[:, None] // C == ti[None, :] // C)).astype(BF16)
    row1 = lambda a: a.reshape(1, -1)
    vec = lambda n: pl.BlockSpec((1, n), lambda b, c: (0, 0))
    full = lambda a: pl.BlockSpec(a.shape, lambda b, c: (0, 0))
    rows = lambda n, col: pl.BlockSpec((S, C, n), lambda b, c: (b, c, col))
    per_seq = lambda n: pl.BlockSpec((S, 1, n), lambda b, c: (b, 0, 0))
    state = pl.BlockSpec((S, n_heads, HEAD_A, HEAD_A), lambda b, c: (b, 0, 0, 0))
    e_mat, bd_mask = same.astype(BF16), same.astype(F32)
    proj3 = proj.reshape(nseq, t, proj.shape[1])
    o_a, s_new, sh_main, sh_lora = pl.pallas_call(
        functools.partial(_rwkv_kernel, chunk=C, width=W),
        grid=(nseq // S, nc),
        in_specs=[rows(3 * W, 0), rows(lw, 0), rows(W, z_col),
                  per_seq(3 * W), per_seq(lw), state,
                  vec(3 * W), vec(lw), vec(W), full(wup), vec(W), full(aup), vec(W), vec(W), vec(W), vec(W), vec(W),
                  full(e_mat), full(bd_mask), full(ltri)],
        out_specs=[rows(W, 0), state, per_seq(3 * W), per_seq(lw)],
        out_shape=[jax.ShapeDtypeStruct((nseq, t, W), BF16),
                   jax.ShapeDtypeStruct((nseq, n_heads, HEAD_A, HEAD_A), F32),
                   jax.ShapeDtypeStruct((nseq, 1, 3 * W), F32),
                   jax.ShapeDtypeStruct((nseq, 1, lw), F32)],
        scratch_shapes=[pltpu.VMEM((S, C + 8, 3 * W), F32), pltpu.VMEM((S, C + 8, lw), F32),
                        pltpu.VMEM((S, W // SLAB, SLAB, SLAB), F32)],
        compiler_params=_params("parallel", "arbitrary"),
        name="rwkv7",
    )(proj3, lora.reshape(nseq, t, lw), proj3, shift_main, shift_lora, s0,
      row1(prm["a_mu"][:3 * W]), row1(prm["a_mu"][3 * W:]), row1(prm["a_w0"]), wup, row1(prm["a_a0"]), aup,
      row1(prm["a_k_k"]), row1(prm["a_k_a"]), row1(prm["a_r_k"]), row1(prm["a_gn_w"]), row1(prm["a_gn_b"]),
      e_mat, bd_mask, ltri)
    return o_a.reshape(nseq * t, W), s_new, sh_main, sh_lora


COL_AZ, COL_BQ, COL_BK, COL_BV, COL_BZ, COL_CU, COL_MQ = 3, 4, 5, 6, 7, 8, 11


def trunk_layer(x, nseq, t, mk, mv, shift_prev, s_prev, past_k, past_v, prm, layer, depth, kv_out, want_vn):
    m, d = x.shape
    W = d // 4
    tm = min(512, m)
    h = rmsnorm_rows(x, prm["g_pre"], min(256, m))
    proj = matmul(h, prm["w_in_main"], tm, 1024)
    lora = matmul(h, prm["w_in_lora"], tm, prm["w_in_lora"].shape[1])

    o_a, s_new, sh_main, sh_lora = rwkv7(proj, lora, COL_AZ, shift_prev[..., :3 * W], shift_prev[..., 3 * W:],
                                         s_prev, prm, nseq, t)
    shift_new = jnp.concatenate([sh_main, sh_lora], axis=-1)

    heads_b = W // HEAD_B
    hb = lambda col: col * heads_b
    o_b, kv_out = stick_breaking(proj, hb(COL_BQ), hb(COL_BZ), hb(COL_BK), hb(COL_BV), past_k, past_v,
                                 nseq, t, heads_b, layer, depth, kv_out)

    clen = min(t, prm["c_ws"].shape[1])
    groups = prm["c_ws"].shape[0]
    bs_full = jnp.repeat(prm["c_bs"][:, :clen].T, W // groups, axis=1)
    o_c, *vn_c = chunk_mlp(proj, COL_CU, prm["c_ws"][:, :clen, :clen], bs_full, prm["c_ln_w"], prm["c_ln_b"],
                           m, clen, max(1, min(512, t) // clen), want_vn)

    o_m = memory_attention(proj, COL_MQ, mk, mv, nseq, t, min(512, t))

    merged = gated_merge(h, (o_a, o_b, o_c, o_m), prm["w_gate"], prm["b_gate"], prm["w_br"], tm,
                         512 if m > tm else 256)
    x_new = out_proj_norm_residual(merged, prm["w_out"], x, prm["g_post"], tm, 512)
    return x_new, shift_new, s_new, kv_out, vn_c


def kernel(x_prompt, x_sample, cache_mem_k, cache_mem_v, cache_sb_k, cache_sb_v, state_rwkv, state_shift, mem_prompt, g_pre, g_post, w_in, a_mu, a_w0, a_w_up, a_a0, a_a_up, a_k_k, a_k_a, a_r_k, a_gn_w, a_gn_b, c_ws, c_bs, c_ln_w, c_ln_b, g_mem, w_mem_kv, w_gate, b_gate, w_br, w_out):
    bp, tp, d = x_prompt.shape
    bs_, ts, _ = x_sample.shape
    depth = w_in.shape[0]
    W = d // 4
    n_mem = mem_prompt.shape[1]
    m_heads = cache_mem_k.shape[3]
    b_heads = cache_sb_k.shape[3]
    a_heads = state_rwkv.shape[2]
    shift_w = state_shift.shape[-1]
    n_past = cache_sb_k.shape[2]
    lora_lo, lora_hi = 3 * W, shift_w

    yp = x_prompt.reshape(bp * tp, d)
    ys = x_sample.reshape(bs_ * ts, d)
    mem_rows = mem_prompt.reshape(bp * n_mem, d)
    mem_out, rw_p, sh_p, rw_s, sh_s, cv_s = [[], []], [], [], [], [], []
    kv_p = kv_s = None
    for l in range(depth):
        prm = {
            "g_pre": g_pre[l], "g_post": g_post[l],
            "w_in_main": jnp.concatenate([w_in[l][:, :lora_lo], w_in[l][:, lora_hi:]], axis=1).astype(BF16),
            "w_in_lora": w_in[l][:, lora_lo:lora_hi].astype(BF16),
            "a_mu": a_mu[l], "a_w0": a_w0[l], "a_w_up": a_w_up[l], "a_a0": a_a0[l], "a_a_up": a_a_up[l],
            "a_k_k": a_k_k[l].reshape(-1), "a_k_a": a_k_a[l].reshape(-1), "a_r_k": a_r_k[l].reshape(-1),
            "a_gn_w": a_gn_w[l].reshape(-1), "a_gn_b": a_gn_b[l].reshape(-1),
            "c_ws": c_ws[l], "c_bs": c_bs[l], "c_ln_w": c_ln_w[l], "c_ln_b": c_ln_b[l],
            "w_gate": w_gate[l].astype(BF16), "b_gate": b_gate[l], "w_br": w_br[l].astype(BF16),
            "w_out": w_out[l].astype(BF16),
        }
        kv = matmul(rmsnorm_rows(mem_rows, g_mem[l], 256), w_mem_kv[l].astype(BF16), 512, 1024)
        mk = kv[:, :W].reshape(bp, n_mem, W)
        mv = kv[:, W:].reshape(bp, n_mem, W)
        shift0 = jnp.zeros((bp, 1, shift_w), F32)
        s0 = jnp.zeros((bp, a_heads, HEAD_A, HEAD_A), F32)
        yp, sh, st, kv_p, _ = trunk_layer(yp, bp, tp, mk.astype(BF16), mv.astype(BF16), shift0, s0, None, None, prm,
                                          l, depth, kv_p, False)
        mem_out[0].append(mk.reshape(bp, n_mem, m_heads, W // m_heads))
        mem_out[1].append(mv.reshape(bp, n_mem, m_heads, W // m_heads))
        rw_p.append(st)
        sh_p.append(sh)
        ys, sh, st, kv_s, cvn = trunk_layer(
            ys, bs_, ts, cache_mem_k[l].reshape(bs_, n_mem, W).astype(BF16),
            cache_mem_v[l].reshape(bs_, n_mem, W).astype(BF16), state_shift[l], state_rwkv[l],
            cache_sb_k[l].reshape(bs_ * n_past, W), cache_sb_v[l].reshape(bs_ * n_past, W), prm, l, depth, kv_s,
            True)
        rw_s.append(st)
        sh_s.append(sh)
        cv_s.append(cvn[0].reshape(bs_, ts, W))
    heads_p = lambda a: a.reshape(depth, bp, tp, b_heads, HEAD_B)
    heads_s = lambda a: a.reshape(depth, bs_, ts, b_heads, HEAD_B)
    return (yp.reshape(bp, tp, d), ys.reshape(bs_, ts, d), jnp.stack(mem_out[0]), jnp.stack(mem_out[1]),
            heads_p(kv_p[0]), heads_p(kv_p[1]), jnp.stack(rw_p), jnp.stack(sh_p),
            heads_s(kv_s[0]), heads_s(kv_s[1]), jnp.stack(rw_s), jnp.stack(sh_s), jnp.stack(cv_s))
```

```python
import functools
import math

import jax
import jax.numpy as jnp
from jax import lax
from jax.experimental import pallas as pl
from jax.experimental.pallas import tpu as pltpu

F32 = jnp.float32
BF16 = jnp.bfloat16

NORM_EPS = 1e-6
GN_EPS = 64e-5
HEAD_A = 64
GROUP_A = 4
SLAB = HEAD_A * GROUP_A
HEAD_B = 128
HEAD_M = 256
GROUP_C = 128
RWKV_CHUNK = 64
VMEM_LIMIT_BYTES = 56 * 1024 * 1024


def _params(*sem):
    return pltpu.CompilerParams(dimension_semantics=sem, vmem_limit_bytes=VMEM_LIMIT_BYTES)


def _split2(x):
    hi = x.astype(BF16)
    lo = (x - hi.astype(F32)).astype(BF16)
    return hi, lo


def _dot(a, b):
    return jnp.dot(a, b, preferred_element_type=F32)


def _dot_nt(a, b):
    return lax.dot_general(a, b, (((1,), (1,)), ((), ())), preferred_element_type=F32)


def _rmsnorm_kernel(x_ref, g_ref, o_ref):
    x = x_ref[...]
    ms = jnp.mean(x * x, axis=-1, keepdims=True)
    o_ref[...] = (x * lax.rsqrt(ms + NORM_EPS) * g_ref[...]).astype(o_ref.dtype)


def rmsnorm_rows(x, g, tm):
    m, d = x.shape
    return pl.pallas_call(
        _rmsnorm_kernel,
        grid=(m // tm,),
        in_specs=[pl.BlockSpec((tm, d), lambda i: (i, 0)), pl.BlockSpec((1, d), lambda i: (0, 0))],
        out_specs=pl.BlockSpec((tm, d), lambda i: (i, 0)),
        out_shape=jax.ShapeDtypeStruct((m, d), BF16),
        compiler_params=_params("parallel"),
        name="rmsnorm_rows",
    )(x, g.reshape(1, d))


POST_ROWS = 64


def _out_proj_kernel(m_ref, w_ref, x_ref, g_ref, o_ref, x_rows, ssq, *, tn):
    j = pl.program_id(1)
    cols = pl.ds(pl.multiple_of(j * tn, tn), tn)
    y = _dot(m_ref[...], w_ref[...])
    sq = jnp.sum(y * y, axis=-1, keepdims=True)
    o_ref[:, cols] = y * g_ref[...]
    x_rows[:, cols] = x_ref[...]

    @pl.when(j == 0)
    def _first():
        ssq[...] = sq

    @pl.when(j > 0)
    def _rest():
        ssq[...] += sq

    @pl.when(j == pl.num_programs(1) - 1)
    def _normalise():
        inv_d = 1.0 / o_ref.shape[1]
        for r0 in range(0, o_ref.shape[0], POST_ROWS):
            rows = slice(r0, r0 + POST_ROWS)
            scale = lax.rsqrt(ssq[rows, :] * inv_d + NORM_EPS)
            o_ref[rows, :] = x_rows[rows, :] + o_ref[rows, :] * scale


def out_proj_norm_residual(merged, w, layer, x, g, tm, tn):
    m, d = x.shape
    return pl.pallas_call(
        functools.partial(_out_proj_kernel, tn=tn),
        grid=(m // tm, d // tn),
        in_specs=[pl.BlockSpec((tm, d), lambda i, j: (i, 0)), pl.BlockSpec((None, d, tn), lambda i, j: (layer, 0, j)),
                  pl.BlockSpec((tm, tn), lambda i, j: (i, j)), pl.BlockSpec((1, tn), lambda i, j: (0, j))],
        out_specs=pl.BlockSpec((tm, d), lambda i, j: (i, 0)),
        out_shape=jax.ShapeDtypeStruct((m, d), F32),
        scratch_shapes=[pltpu.VMEM((tm, d), F32), pltpu.VMEM((tm, 1), F32)],
        compiler_params=_params("parallel", "arbitrary"),
        name="out_proj_norm_residual",
    )(merged, w, x, g.reshape(1, d))


def _matmul_kernel(x_ref, w_ref, o_ref):
    o_ref[...] = _dot(x_ref[...], w_ref[...]).astype(o_ref.dtype)


def matmul(x, w, layer, tm, tn, out_dtype=F32):
    m, k = x.shape
    n = w.shape[2]
    return pl.pallas_call(
        _matmul_kernel,
        grid=(n // tn, m // tm),
        in_specs=[pl.BlockSpec((tm, k), lambda j, i: (i, 0)),
                  pl.BlockSpec((None, k, tn), lambda j, i: (layer, 0, j))],
        out_specs=pl.BlockSpec((tm, tn), lambda j, i: (i, j)),
        out_shape=jax.ShapeDtypeStruct((m, n), out_dtype),
        compiler_params=_params("parallel", "parallel"),
        name="matmul",
    )(x, w)


def _merge_kernel(h_ref, oa_ref, ob_ref, oc_ref, om_ref, wg_ref, bg_ref, wbr_ref, out_ref):
    h = h_ref[...]
    acc = None
    for n, o_ref in enumerate((oa_ref, ob_ref, oc_ref, om_ref)):
        gate = jax.nn.sigmoid(_dot(h, wg_ref[n]) + bg_ref[n])
        term = gate * _dot(o_ref[...], wbr_ref[n])
        acc = term if acc is None else acc + term
    out_ref[...] = acc.astype(out_ref.dtype)


def gated_merge(h, branches, wg, bg, wbr, layer, tm, tn):
    m, d = h.shape
    _, nb, w, _ = wbr.shape
    resident = dict(pipeline_mode=pl.Buffered(1)) if m > tm else {}
    o_spec = pl.BlockSpec((tm, w), lambda j, i: (i, 0))
    return pl.pallas_call(
        _merge_kernel,
        grid=(d // tn, m // tm),
        in_specs=[pl.BlockSpec((tm, d), lambda j, i: (i, 0)), o_spec, o_spec, o_spec, o_spec,
                  pl.BlockSpec((None, nb, d, tn), lambda j, i: (layer, 0, 0, j), **resident),
                  pl.BlockSpec((nb, 1, tn), lambda j, i: (0, 0, j)),
                  pl.BlockSpec((None, nb, w, tn), lambda j, i: (layer, 0, 0, j), **resident)],
        out_specs=pl.BlockSpec((tm, tn), lambda j, i: (i, j)),
        out_shape=jax.ShapeDtypeStruct((m, d), BF16),
        compiler_params=_params("parallel", "parallel"),
        name="gated_merge",
    )(h, *branches, wg, bg.reshape(nb, 1, d), wbr)


def _memattn_kernel(q_ref, mk_ref, mv_ref, o_ref, *, heads):
    scale = 1.0 / math.sqrt(HEAD_M)
    for hd in range(heads):
        sl = slice(hd * HEAD_M, (hd + 1) * HEAD_M)
        q = q_ref[:, sl].astype(BF16)
        s = _dot_nt(q, mk_ref[0, :, sl]) * scale
        s = s - jnp.max(s, axis=-1, keepdims=True)
        p = jnp.exp(s)
        p = p * (1.0 / jnp.sum(p, axis=-1, keepdims=True))
        o_ref[:, sl] = _dot(p.astype(BF16), mv_ref[0, :, sl]).astype(o_ref.dtype)


def memory_attention(proj, q_col, mk, mv, nseq, t, tq):
    n_mem, w = mk.shape[1:]
    per_seq = t // tq
    mem_spec = pl.BlockSpec((1, n_mem, w), lambda i: (i // per_seq, 0, 0))
    return pl.pallas_call(
        functools.partial(_memattn_kernel, heads=w // HEAD_M),
        grid=(nseq * per_seq,),
        in_specs=[pl.BlockSpec((tq, w), lambda i: (i, q_col)), mem_spec, mem_spec],
        out_specs=pl.BlockSpec((tq, w), lambda i: (i, 0)),
        out_shape=jax.ShapeDtypeStruct((nseq * t, w), BF16),
        compiler_params=_params("parallel"),
        name="memory_attention",
    )(proj, mk, mv)


def _cmlp_kernel(u_ref, v_ref, z_ref, ws_ref, bs_ref, lnw_ref, lnb_ref, o_ref, *maybe_vn_ref, clen, chunks):
    v = v_ref[...]
    mu = jnp.mean(v, axis=-1, keepdims=True)
    var = jnp.mean(jnp.square(v - mu), axis=-1, keepdims=True)
    vn = (v - mu) * lax.rsqrt(var + NORM_EPS) * lnw_ref[...] + lnb_ref[...]
    for vn_ref in maybe_vn_ref:
        vn_ref[...] = vn
    vn16 = vn.astype(BF16)
    groups = vn.shape[1] // GROUP_C
    row = lax.broadcasted_iota(jnp.int32, (clen, clen), 0)
    col = lax.broadcasted_iota(jnp.int32, (clen, clen), 1)
    for g in range(groups):
        wm = jnp.where(row >= col, ws_ref[g], 0.0).astype(BF16)
        cs = slice(g * GROUP_C, (g + 1) * GROUP_C)
        for c in range(chunks):
            rs = slice(c * clen, (c + 1) * clen)
            s = _dot(wm, vn16[rs, cs]) + bs_ref[:, cs]
            o_ref[rs, cs] = (u_ref[rs, cs] * s * jax.nn.silu(z_ref[rs, cs])).astype(o_ref.dtype)


def chunk_mlp(proj, u_col, ws, bs_full, ln_w, ln_b, rows, clen, chunks, want_vn):
    w = ln_w.shape[0]
    tm = clen * chunks
    col = lambda c: pl.BlockSpec((tm, w), lambda i: (i, c))
    const2 = lambda a: pl.BlockSpec(a.shape, lambda i: (0, 0))
    out = pl.BlockSpec((tm, w), lambda i: (i, 0))
    out_dtypes = [BF16, F32] if want_vn else [BF16]
    return pl.pallas_call(
        functools.partial(_cmlp_kernel, clen=clen, chunks=chunks),
        grid=(rows // tm,),
        in_specs=[col(u_col), col(u_col + 1), col(u_col + 2),
                  pl.BlockSpec(ws.shape, lambda i: (0, 0, 0)), const2(bs_full),
                  pl.BlockSpec((1, w), lambda i: (0, 0)), pl.BlockSpec((1, w), lambda i: (0, 0))],
        out_specs=[out] * len(out_dtypes),
        out_shape=[jax.ShapeDtypeStruct((rows, w), dt) for dt in out_dtypes],
        compiler_params=_params("parallel"),
        name="chunk_mlp",
    )(proj, proj, proj, ws, bs_full, ln_w.reshape(1, w), ln_b.reshape(1, w))


SB_HEADS_PER_STEP = 2
SB_BLOCK = 256
SB_GROUP = 2


def _sb_group(q16, k16, v16, tk, u, masks, carry):
    tq = q16.shape[0]
    nb = k16.shape[0] // tk
    z2 = _dot_nt(q16, k16) * (math.log2(math.e) / math.sqrt(HEAD_B))
    neg_abs = lax.bitcast_convert_type(lax.bitcast_convert_type(z2, jnp.uint32) | jnp.uint32(0x80000000), F32)
    neg_l1m = jnp.maximum(z2, 0.0) + jnp.log2(1.0 + jnp.exp2(neg_abs))
    log_beta = z2 - neg_l1m
    nl = []
    for j in range(nb):
        nl_j = neg_l1m[:, j * tk:(j + 1) * tk]
        nl.append(jnp.where(masks[j], nl_j, 0.0) if j in masks else nl_j)
    lhs = jnp.concatenate([jnp.concatenate(_split2(nl_j), axis=1) for nl_j in nl], axis=0)
    cs = _dot(lhs, u)
    att = [None] * nb
    for j in reversed(range(nb)):
        cs_j = cs[j * tq:(j + 1) * tq]
        att_j = jnp.exp2(log_beta[:, j * tk:(j + 1) * tk] - cs_j - carry)
        att[j] = (jnp.where(masks[j], att_j, 0.0) if j in masks else att_j).astype(BF16)
        carry = carry + (cs_j[:, 0:1] + nl[j][:, 0:1])
    return _dot(jnp.concatenate(att, axis=1), v16), carry


def _head_slices(ref):
    return [slice(i * HEAD_B, (i + 1) * HEAD_B) for i in range(ref.shape[1] // HEAD_B)]


def _sb_prompt_kernel(*refs, tq, nq):
    q_ref, z_ref, k_ref, v_ref, u_ref = refs[:5]
    o_ref, kout_ref, vout_ref = refs[-3:]
    qi = pl.program_id(2)

    @pl.when(qi == 0)
    def _emit_kv():
        kout_ref[...] = k_ref[...]
        vout_ref[...] = v_ref[...]

    row = lax.broadcasted_iota(jnp.int32, (tq, tq), 0)
    col = lax.broadcasted_iota(jnp.int32, (tq, tq), 1)
    q_pos = qi * tq + row
    for grp in range(1, pl.cdiv(nq, SB_GROUP) + 1):
        nb = min(grp * SB_GROUP, nq)

        @pl.when(qi // SB_GROUP + 1 == grp)
        def _sweep():
            masks = {j: (j * tq + col) < q_pos for j in range((grp - 1) * SB_GROUP, nb)}
            for s in _head_slices(q_ref):
                out, _ = _sb_group(q_ref[:, s].astype(BF16), k_ref[0:nb * tq, s].astype(BF16),
                                   v_ref[0:nb * tq, s].astype(BF16), tq, u_ref[...], masks, jnp.zeros((tq, 1), F32))
                o_ref[:, s] = (out * jax.nn.silu(z_ref[:, s])).astype(o_ref.dtype)


def _sb_sample_kernel(*refs, tk):
    q_ref, z_ref, kn_ref, vn_ref, kp_ref, vp_ref, ud_ref, uf_ref = refs[:8]
    o_ref, kout_ref, vout_ref = refs[-3:]
    kout_ref[...] = kn_ref[...]
    vout_ref[...] = vn_ref[...]
    tq = q_ref.shape[0]
    row = lax.broadcasted_iota(jnp.int32, (tq, tq), 0)
    col = lax.broadcasted_iota(jnp.int32, (tq, tq), 1)
    for s in _head_slices(q_ref):
        q16 = q_ref[:, s].astype(BF16)
        out_new, carry = _sb_group(q16, kn_ref[:, s].astype(BF16), vn_ref[:, s].astype(BF16), tq, ud_ref[...],
                                   {0: col < row}, jnp.zeros((tq, 1), F32))
        out_past, _ = _sb_group(q16, kp_ref[:, s].astype(BF16), vp_ref[:, s].astype(BF16), tk, uf_ref[...],
                                {}, carry)
        o_ref[:, s] = ((out_new + out_past) * jax.nn.silu(z_ref[:, s])).astype(o_ref.dtype)


def _suffix_matrix(tk):
    s_later = lax.broadcasted_iota(jnp.int32, (tk, tk), 0)
    s_here = lax.broadcasted_iota(jnp.int32, (tk, tk), 1)
    u = (s_later > s_here).astype(BF16)
    return jnp.concatenate([u, u], axis=0)


def stick_breaking(proj, q_col, z_col, k_col, v_col, past_k, past_v, nseq, t, heads, layer, depth, kv_out):
    hp = SB_HEADS_PER_STEP
    wide = hp * HEAD_B
    tq = min(SB_BLOCK, t)
    nq = t // tq
    blk = lambda col: pl.BlockSpec((tq, wide), lambda b, h, i: (b * nq + i, col // hp + h))
    seq = lambda rows, col: pl.BlockSpec((rows, wide), lambda b, h, i: (b, col // hp + h))
    const = lambda a: pl.BlockSpec(a.shape, lambda b, h, i: (0, 0))
    if past_k is None:
        u = _suffix_matrix(tq)
        body = functools.partial(_sb_prompt_kernel, tq=tq, nq=nq)
        in_specs = [blk(q_col), blk(z_col), seq(t, k_col), seq(t, v_col), const(u)]
        args = [proj, proj, proj, proj, u]
    else:
        assert nq == 1
        n_past = past_k.shape[1] // nseq
        past = pl.BlockSpec((None, n_past, wide), lambda b, h, i: (layer, b, h))
        ud, uf = _suffix_matrix(tq), _suffix_matrix(SB_BLOCK)
        body = functools.partial(_sb_sample_kernel, tk=SB_BLOCK)
        in_specs = [blk(q_col), blk(z_col), seq(t, k_col), seq(t, v_col), past, past, const(ud), const(uf)]
        args = [proj, proj, proj, proj, past_k, past_v, ud, uf]
    aliases = {}
    if kv_out is not None:
        aliases = {len(args): 1, len(args) + 1: 2}
        in_specs = in_specs + [pl.BlockSpec(memory_space=pl.ANY)] * 2
        args = args + list(kv_out)
    kv_shape = jax.ShapeDtypeStruct((depth, nseq * t, heads * HEAD_B), F32)
    kv_spec = pl.BlockSpec((None, t, wide), lambda b, h, i: (layer, b, h))
    o_b, k_all, v_all = pl.pallas_call(
        body,
        grid=(nseq, heads // hp, nq),
        in_specs=in_specs,
        out_specs=[pl.BlockSpec((tq, wide), lambda b, h, i: (b * nq + i, h)), kv_spec, kv_spec],
        out_shape=[jax.ShapeDtypeStruct((nseq * t, heads * HEAD_B), BF16), kv_shape, kv_shape],
        input_output_aliases=aliases,
        compiler_params=_params("parallel", "parallel", "arbitrary"),
        name="stick_breaking",
    )(*args)
    return o_b, (k_all, v_all)


def _block_stack(x, lane_masks):
    return jnp.concatenate([jnp.where(m, x, 0.0) for m in lane_masks], axis=0)


def _rwkv_kernel(x_ref, lora_ref, z_ref, sh_main_ref, sh_lora_ref, s0_ref,
                 mu_main_ref, mu_lora_ref, w0_ref, wup_ref, a0_ref, aup_ref, kk_ref, ka_ref, rk_ref,
                 gnw_ref, gnb_ref, e_ref, bd_ref, ltri_ref,
                 o_ref, sout_ref, shm_out_ref, shl_out_ref,
                 xbuf, lbuf, sbd, *, chunk, width):
    c = pl.program_id(1)
    n_chunks = pl.num_programs(1)
    C, W = chunk, width
    n_seq = x_ref.shape[0]
    n_slabs = W // SLAB
    n_heads = W // HEAD_A

    @pl.when(c == 0)
    def _init():
        sbd[...] = jnp.zeros_like(sbd)
        for s in range(n_seq):
            xbuf[s, 7:8, :] = sh_main_ref[s]
            lbuf[s, 7:8, :] = sh_lora_ref[s]
            for hd in range(n_heads):
                g, j = divmod(hd, GROUP_A)
                sbd[s, g, j * HEAD_A:(j + 1) * HEAD_A, j * HEAD_A:(j + 1) * HEAD_A] = s0_ref[s, hd]

    def shifted(ref, buf, mu_ref):
        rows = []
        for s in range(n_seq):
            x = ref[s]
            buf[s, 8:8 + C, :] = x
            rows.append(x + mu_ref[...] * (buf[s, 7:7 + C, :] - x))
            buf[s, 7:8, :] = x[C - 1:C, :]
        return jnp.concatenate(rows, axis=0)

    xs = shifted(x_ref, xbuf, mu_main_ref)
    lo_s = shifted(lora_ref, lbuf, mu_lora_ref)
    seq_rows = [slice(s * C, (s + 1) * C) for s in range(n_seq)]
    slabs = [slice(g * SLAB, (g + 1) * SLAB) for g in range(n_slabs)]

    r, k, v = xs[:, :W], xs[:, W:2 * W], xs[:, 2 * W:]
    w_pre = w0_ref[...] + _dot(jnp.tanh(lo_s).astype(BF16), wup_ref[...])
    ld = -math.exp(-0.5) * jax.nn.sigmoid(w_pre)
    a = jax.nn.sigmoid(a0_ref[...] + _dot(lo_s.astype(BF16), aup_ref[...]))

    e_mat = e_ref[...]

    def seg_sum(val):
        n = val.shape[0]
        hi, lo = _split2(val)
        out = _dot(jnp.concatenate([part[:, sl] for part in (hi, lo) for sl in slabs], axis=0), e_mat)
        return jnp.concatenate([out[g * n:(g + 1) * n] + out[(n_slabs + g) * n:(n_slabs + g + 1) * n]
                                for g in range(n_slabs)], axis=1)

    kk = k * kk_ref[...]
    kk = kk * lax.rsqrt(jnp.maximum(seg_sum(kk * kk), 1e-24))
    kmod = k * (1.0 + (a - 1.0) * ka_ref[...])

    ld_hi = ld.astype(BF16)
    ld_r1 = ld - ld_hi.astype(F32)
    ld_mid = ld_r1.astype(BF16)
    ld_lo = (ld_r1 - ld_mid.astype(F32)).astype(BF16)
    ltri = ltri_ref[...]
    lp = _dot(ltri, ld_hi) + _dot(ltri, ld_mid) + _dot(ltri, ld_lo)
    lp_last = [lp[rs.stop - 1:rs.stop, :] for rs in seq_rows]
    lp_end = jnp.concatenate([jnp.broadcast_to(row, (C, W)) for row in lp_last], axis=0)
    e_neg = jnp.exp(-lp)
    kka = kk * a
    kap = kk * jnp.exp(lp - ld)
    bet = kka * e_neg
    kt = kmod * e_neg
    rt = r * jnp.exp(lp)
    e_end = jnp.exp(lp_end - lp)
    kt_end = kmod * e_end
    bet_end = kka * e_end
    dec_end = [jnp.exp(row) for row in lp_last]

    lane = lax.broadcasted_iota(jnp.int32, (1, SLAB), 1)
    lane_masks = [(lane >= j * HEAD_A) & (lane < (j + 1) * HEAD_A) for j in range(GROUP_A)]
    t_row = lax.broadcasted_iota(jnp.int32, (C, GROUP_A * C), 0)
    s_col = lax.broadcasted_iota(jnp.int32, (C, GROUP_A * C), 1) & (C - 1)
    strict, incl = s_col < t_row, s_col <= t_row
    stack = lambda val16: _block_stack(val16, lane_masks)
    nh = GROUP_A * C
    units = [(s, g) for s in range(n_seq) for g in range(n_slabs)]
    per_slab = lambda fn: [fn(u, seq_rows[s], slabs[g]) for u, (s, g) in enumerate(units)]

    kap16, rt16, bet16, kt16, v16 = (val.astype(BF16) for val in (kap, rt, bet, kt, v))
    s_old = [sbd[s, g] for s, g in units]
    lhs = per_slab(lambda g, rs, sl: jnp.concatenate([kap16[rs, sl], rt16[rs, sl]], axis=0))
    rhs = per_slab(lambda g, rs, sl: jnp.concatenate([stack(bet16[rs, sl]), stack(kt16[rs, sl])], axis=0))
    sc = per_slab(lambda g, rs, sl: _dot_nt(lhs[g], rhs[g]))
    ls = per_slab(lambda g, rs, sl: _dot_nt(lhs[g], s_old[g].astype(BF16)))
    v_stack = per_slab(lambda g, rs, sl: stack(v16[rs, sl]))
    p16 = per_slab(lambda g, rs, sl: jnp.where(strict, -sc[g][:C, :nh], 0.0).astype(BF16))
    xw = per_slab(lambda g, rs, sl: ls[g][:C]
                  + _dot(jnp.where(strict, sc[g][:C, nh:], 0.0).astype(BF16), v_stack[g]))
    for step in range(6):
        xw = per_slab(lambda g, rs, sl: xw[g] + _dot(p16[g], stack(xw[g].astype(BF16))))
        if step < 5:
            p16 = per_slab(lambda g, rs, sl: _dot(p16[g], stack(p16[g])).astype(BF16))
    ab_inc = per_slab(lambda g, rs, sl: jnp.concatenate(
        [jnp.where(incl, sc[g][C:, :nh], 0.0), jnp.where(incl, sc[g][C:, nh:], 0.0)], axis=1).astype(BF16))
    y = per_slab(lambda g, rs, sl: ls[g][C:] + _dot(
        ab_inc[g], jnp.concatenate([stack((-xw[g]).astype(BF16)), v_stack[g]], axis=0)))
    upd = per_slab(lambda g, rs, sl: _dot(
        jnp.concatenate([v[rs, sl], -xw[g]], axis=0).T.astype(BF16),
        jnp.concatenate([kt_end[rs, sl], bet_end[rs, sl]], axis=0).astype(BF16)))
    for u, (s, g) in enumerate(units):
        sbd[s, g] = s_old[u] * dec_end[s][:, slabs[g]] + upd[u] * bd_ref[...]

    y = jnp.concatenate([jnp.concatenate(y[s * n_slabs:(s + 1) * n_slabs], axis=1) for s in range(n_seq)], axis=0)
    inv_n = 1.0 / HEAD_A
    mean = seg_sum(y) * inv_n
    yc = y - mean
    var = seg_sum(yc * yc) * inv_n
    y = yc * lax.rsqrt(var + GN_EPS) * gnw_ref[...] + gnb_ref[...]
    y = y + seg_sum(r * kmod * rk_ref[...]) * v
    for s, rs in enumerate(seq_rows):
        o_ref[s] = (y[rs] * jax.nn.silu(z_ref[s])).astype(o_ref.dtype)

    @pl.when(c == n_chunks - 1)
    def _fin():
        for s in range(n_seq):
            shm_out_ref[s] = xbuf[s, 7:8, :]
            shl_out_ref[s] = lbuf[s, 7:8, :]
            for hd in range(n_heads):
                g, j = divmod(hd, GROUP_A)
                sout_ref[s, hd] = sbd[s, g, j * HEAD_A:(j + 1) * HEAD_A, j * HEAD_A:(j + 1) * HEAD_A]


RWKV_SEQS_PER_STEP = 2


def rwkv7(proj, lora, z_col, shift_main, shift_lora, s0, prm, nseq, t):
    W = prm["a_w0"].shape[0]
    C = RWKV_CHUNK
    S = RWKV_SEQS_PER_STEP
    nc = t // C
    n_heads = W // HEAD_A
    lw = lora.shape[1]
    half = lw // 2
    zpad = jnp.zeros((half, W), F32)
    wup = jnp.concatenate([prm["a_w_up"], zpad], axis=0).astype(BF16)
    aup = jnp.concatenate([zpad, prm["a_a_up"]], axis=0).astype(BF16)
    idx = jnp.arange(SLAB) // HEAD_A
    same = idx[:, None] == idx[None, :]
    ti = jnp.arange(S * C)
    ltri = ((ti[:, None] >= ti[None, :]) & (ti[:, None] // C == ti[None, :] // C)).astype(BF16)
    row1 = lambda a: a.reshape(1, -1)
    vec = lambda n: pl.BlockSpec((1, n), lambda b, c: (0, 0))
    full = lambda a: pl.BlockSpec(a.shape, lambda b, c: (0, 0))
    rows = lambda n, col: pl.BlockSpec((S, C, n), lambda b, c: (b, c, col))
    per_seq = lambda n: pl.BlockSpec((S, 1, n), lambda b, c: (b, 0, 0))
    state = pl.BlockSpec((S, n_heads, HEAD_A, HEAD_A), lambda b, c: (b, 0, 0, 0))
    e_mat, bd_mask = same.astype(BF16), same.astype(F32)
    proj3 = proj.reshape(nseq, t, proj.shape[1])
    o_a, s_new, sh_main, sh_lora = pl.pallas_call(
        functools.partial(_rwkv_kernel, chunk=C, width=W),
        grid=(nseq // S, nc),
        in_specs=[rows(3 * W, 0), rows(lw, 0), rows(W, z_col),
                  per_seq(3 * W), per_seq(lw), state,
                  vec(3 * W), vec(lw), vec(W), full(wup), vec(W), full(aup), vec(W), vec(W), vec(W), vec(W), vec(W),
                  full(e_mat), full(bd_mask), full(ltri)],
        out_specs=[rows(W, 0), state, per_seq(3 * W), per_seq(lw)],
        out_shape=[jax.ShapeDtypeStruct((nseq, t, W), BF16),
                   jax.ShapeDtypeStruct((nseq, n_heads, HEAD_A, HEAD_A), F32),
                   jax.ShapeDtypeStruct((nseq, 1, 3 * W), F32),
                   jax.ShapeDtypeStruct((nseq, 1, lw), F32)],
        scratch_shapes=[pltpu.VMEM((S, C + 8, 3 * W), F32), pltpu.VMEM((S, C + 8, lw), F32),
                        pltpu.VMEM((S, W // SLAB, SLAB, SLAB), F32)],
        compiler_params=_params("parallel", "arbitrary"),
        name="rwkv7",
    )(proj3, lora.reshape(nseq, t, lw), proj3, shift_main, shift_lora, s0,
      row1(prm["a_mu"][:3 * W]), row1(prm["a_mu"][3 * W:]), row1(prm["a_w0"]), wup, row1(prm["a_a0"]), aup,
      row1(prm["a_k_k"]), row1(prm["a_k_a"]), row1(prm["a_r_k"]), row1(prm["a_gn_w"]), row1(prm["a_gn_b"]),
      e_mat, bd_mask, ltri)
    return o_a.reshape(nseq * t, W), s_new, sh_main, sh_lora


COL_AZ, COL_BQ, COL_BK, COL_BV, COL_BZ, COL_CU, COL_MQ = 3, 4, 5, 6, 7, 8, 11


def trunk_layer(x, nseq, t, mk, mv, shift_prev, s_prev, past_k, past_v, prm, layer, depth, kv_out, want_vn):
    m, d = x.shape
    W = d // 4
    tm = min(512, m)
    h = rmsnorm_rows(x, prm["g_pre"], min(256, m))
    proj = matmul(h, prm["w_in_main"], layer, tm, 1024)
    lora = matmul(h, prm["w_in_lora"], layer, tm, prm["w_in_lora"].shape[2])

    o_a, s_new, sh_main, sh_lora = rwkv7(proj, lora, COL_AZ, shift_prev[..., :3 * W], shift_prev[..., 3 * W:],
                                         s_prev, prm, nseq, t)
    shift_new = jnp.concatenate([sh_main, sh_lora], axis=-1)

    heads_b = W // HEAD_B
    hb = lambda col: col * heads_b
    o_b, kv_out = stick_breaking(proj, hb(COL_BQ), hb(COL_BZ), hb(COL_BK), hb(COL_BV), past_k, past_v,
                                 nseq, t, heads_b, layer, depth, kv_out)

    clen = min(t, prm["c_ws"].shape[1])
    groups = prm["c_ws"].shape[0]
    bs_full = jnp.repeat(prm["c_bs"][:, :clen].T, W // groups, axis=1)
    o_c, *vn_c = chunk_mlp(proj, COL_CU, prm["c_ws"][:, :clen, :clen], bs_full, prm["c_ln_w"], prm["c_ln_b"],
                           m, clen, max(1, min(512, t) // clen), want_vn)

    o_m = memory_attention(proj, COL_MQ, mk, mv, nseq, t, min(512, t))

    merged = gated_merge(h, (o_a, o_b, o_c, o_m), prm["w_gate"], prm["b_gate"], prm["w_br"], layer, tm,
                         512 if m > tm else 256)
    x_new = out_proj_norm_residual(merged, prm["w_out"], layer, x, prm["g_post"], tm, 512)
    return x_new, shift_new, s_new, kv_out, vn_c


def kernel(x_prompt, x_sample, cache_mem_k, cache_mem_v, cache_sb_k, cache_sb_v, state_rwkv, state_shift, mem_prompt, g_pre, g_post, w_in, a_mu, a_w0, a_w_up, a_a0, a_a_up, a_k_k, a_k_a, a_r_k, a_gn_w, a_gn_b, c_ws, c_bs, c_ln_w, c_ln_b, g_mem, w_mem_kv, w_gate, b_gate, w_br, w_out):
    bp, tp, d = x_prompt.shape
    bs_, ts, _ = x_sample.shape
    depth = w_in.shape[0]
    W = d // 4
    n_mem = mem_prompt.shape[1]
    m_heads = cache_mem_k.shape[3]
    b_heads = cache_sb_k.shape[3]
    a_heads = state_rwkv.shape[2]
    shift_w = state_shift.shape[-1]
    n_past = cache_sb_k.shape[2]
    lora_lo, lora_hi = 3 * W, shift_w

    yp = x_prompt.reshape(bp * tp, d)
    ys = x_sample.reshape(bs_ * ts, d)
    mem_rows = mem_prompt.reshape(bp * n_mem, d)
    mem_out, rw_p, sh_p, rw_s, sh_s, cv_s = [[], []], [], [], [], [], []
    kv_p = kv_s = None
    dense = {
        "w_in_main": jnp.concatenate([w_in[:, :, :lora_lo], w_in[:, :, lora_hi:]], axis=2).astype(BF16),
        "w_in_lora": w_in[:, :, lora_lo:lora_hi].astype(BF16),
        "w_gate": w_gate.astype(BF16), "w_br": w_br.astype(BF16), "w_out": w_out.astype(BF16),
    }
    w_mem16 = w_mem_kv.astype(BF16)
    past_k = cache_sb_k.reshape(depth, bs_ * n_past, W)
    past_v = cache_sb_v.reshape(depth, bs_ * n_past, W)
    for l in range(depth):
        prm = dict(dense)
        prm.update({
            "g_pre": g_pre[l], "g_post": g_post[l],
            "a_mu": a_mu[l], "a_w0": a_w0[l], "a_w_up": a_w_up[l], "a_a0": a_a0[l], "a_a_up": a_a_up[l],
            "a_k_k": a_k_k[l].reshape(-1), "a_k_a": a_k_a[l].reshape(-1), "a_r_k": a_r_k[l].reshape(-1),
            "a_gn_w": a_gn_w[l].reshape(-1), "a_gn_b": a_gn_b[l].reshape(-1),
            "c_ws": c_ws[l], "c_bs": c_bs[l], "c_ln_w": c_ln_w[l], "c_ln_b": c_ln_b[l], "b_gate": b_gate[l],
        })
        kv = matmul(rmsnorm_rows(mem_rows, g_mem[l], 256), w_mem16, l, 512, 1024)
        mk = kv[:, :W].reshape(bp, n_mem, W)
        mv = kv[:, W:].reshape(bp, n_mem, W)
        shift0 = jnp.zeros((bp, 1, shift_w), F32)
        s0 = jnp.zeros((bp, a_heads, HEAD_A, HEAD_A), F32)
        yp, sh, st, kv_p, _ = trunk_layer(yp, bp, tp, mk.astype(BF16), mv.astype(BF16), shift0, s0, None, None, prm,
                                          l, depth, kv_p, False)
        mem_out[0].append(mk.reshape(bp, n_mem, m_heads, W // m_heads))
        mem_out[1].append(mv.reshape(bp, n_mem, m_heads, W // m_heads))
        rw_p.append(st)
        sh_p.append(sh)
        ys, sh, st, kv_s, cvn = trunk_layer(
            ys, bs_, ts, cache_mem_k[l].reshape(bs_, n_mem, W).astype(BF16),
            cache_mem_v[l].reshape(bs_, n_mem, W).astype(BF16), state_shift[l], state_rwkv[l],
            past_k, past_v, prm, l, depth, kv_s, True)
        rw_s.append(st)
        sh_s.append(sh)
        cv_s.append(cvn[0].reshape(bs_, ts, W))
    heads_p = lambda a: a.reshape(depth, bp, tp, b_heads, HEAD_B)
    heads_s = lambda a: a.reshape(depth, bs_, ts, b_heads, HEAD_B)
    return (yp.reshape(bp, tp, d), ys.reshape(bs_, ts, d), jnp.stack(mem_out[0]), jnp.stack(mem_out[1]),
            heads_p(kv_p[0]), heads_p(kv_p[1]), jnp.stack(rw_p), jnp.stack(sh_p),
            heads_s(kv_s[0]), heads_s(kv_s[1]), jnp.stack(rw_s), jnp.stack(sh_s), jnp.stack(cv_s))
```

```python
import functools
import math

import jax
import jax.numpy as jnp
from jax import lax
from jax.experimental import pallas as pl
from jax.experimental.pallas import tpu as pltpu

F32 = jnp.float32
BF16 = jnp.bfloat16

NORM_EPS = 1e-6
GN_EPS = 64e-5
HEAD_A = 64
GROUP_A = 4
SLAB = HEAD_A * GROUP_A
HEAD_B = 128
HEAD_M = 256
GROUP_C = 128
RWKV_CHUNK = 64
VMEM_LIMIT_BYTES = 56 * 1024 * 1024


def _params(*sem):
    return pltpu.CompilerParams(dimension_semantics=sem, vmem_limit_bytes=VMEM_LIMIT_BYTES)


def _split2(x):
    hi = x.astype(BF16)
    lo = (x - hi.astype(F32)).astype(BF16)
    return hi, lo


def _dot(a, b):
    return jnp.dot(a, b, preferred_element_type=F32)


def _dot_nt(a, b):
    return lax.dot_general(a, b, (((1,), (1,)), ((), ())), preferred_element_type=F32)


def _rmsnorm_kernel(x_ref, g_ref, o_ref):
    x = x_ref[...]
    ms = jnp.mean(x * x, axis=-1, keepdims=True)
    o_ref[...] = (x * lax.rsqrt(ms + NORM_EPS) * g_ref[...]).astype(o_ref.dtype)


def rmsnorm_rows(x, g, tm):
    m, d = x.shape
    return pl.pallas_call(
        _rmsnorm_kernel,
        grid=(m // tm,),
        in_specs=[pl.BlockSpec((tm, d), lambda i: (i, 0)), pl.BlockSpec((1, d), lambda i: (0, 0))],
        out_specs=pl.BlockSpec((tm, d), lambda i: (i, 0)),
        out_shape=jax.ShapeDtypeStruct((m, d), BF16),
        compiler_params=_params("parallel"),
        name="rmsnorm_rows",
    )(x, g.reshape(1, d))


POST_ROWS = 64


def _out_proj_kernel(*refs, tn, with_next):
    if with_next:
        m_ref, w_ref, x_ref, g_ref, gn_ref, o_ref, hn_ref, x_rows, ssq = refs
    else:
        m_ref, w_ref, x_ref, g_ref, o_ref, x_rows, ssq = refs
    j = pl.program_id(1)
    cols = pl.ds(pl.multiple_of(j * tn, tn), tn)
    y = _dot(m_ref[...], w_ref[...])
    sq = jnp.sum(y * y, axis=-1, keepdims=True)
    o_ref[:, cols] = y * g_ref[...]
    x_rows[:, cols] = x_ref[...]

    @pl.when(j == 0)
    def _first():
        ssq[...] = sq

    @pl.when(j > 0)
    def _rest():
        ssq[...] += sq

    @pl.when(j == pl.num_programs(1) - 1)
    def _normalise():
        inv_d = 1.0 / o_ref.shape[1]
        for r0 in range(0, o_ref.shape[0], POST_ROWS):
            rows = slice(r0, r0 + POST_ROWS)
            scale = lax.rsqrt(ssq[rows, :] * inv_d + NORM_EPS)
            x_new = x_rows[rows, :] + o_ref[rows, :] * scale
            o_ref[rows, :] = x_new
            if with_next:
                ms = jnp.mean(x_new * x_new, axis=-1, keepdims=True)
                hn_ref[rows, :] = (x_new * lax.rsqrt(ms + NORM_EPS) * gn_ref[...]).astype(hn_ref.dtype)


def out_proj_norm_residual(merged, w, layer, x, g, g_next, tm, tn):
    m, d = x.shape
    with_next = g_next is not None
    row = pl.BlockSpec((tm, d), lambda i, j: (i, 0))
    gain = pl.BlockSpec((1, d), lambda i, j: (0, 0))
    out = pl.pallas_call(
        functools.partial(_out_proj_kernel, tn=tn, with_next=with_next),
        grid=(m // tm, d // tn),
        in_specs=[row, pl.BlockSpec((None, d, tn), lambda i, j: (layer, 0, j)),
                  pl.BlockSpec((tm, tn), lambda i, j: (i, j)), pl.BlockSpec((1, tn), lambda i, j: (0, j))]
                 + ([gain] if with_next else []),
        out_specs=[row, row] if with_next else [row],
        out_shape=[jax.ShapeDtypeStruct((m, d), F32)] + ([jax.ShapeDtypeStruct((m, d), BF16)] if with_next else []),
        scratch_shapes=[pltpu.VMEM((tm, d), F32), pltpu.VMEM((tm, 1), F32)],
        compiler_params=_params("parallel", "arbitrary"),
        name="out_proj_norm_residual",
    )(merged, w, x, g.reshape(1, d), *([g_next.reshape(1, d)] if with_next else []))
    return out if with_next else (out[0], None)


def _drop_cols_kernel(a_ref, b_ref, o_ref, *, first_shifted, gap):
    j = pl.program_id(2)

    @pl.when(j < first_shifted)
    def _before_gap():
        o_ref[...] = a_ref[...].astype(o_ref.dtype)

    @pl.when(j >= first_shifted)
    def _after_gap():
        o_ref[...] = jnp.concatenate([a_ref[:, gap:], b_ref[...]], axis=1).astype(o_ref.dtype)


def drop_cols_cast(w, lo, hi, cw, tk):
    depth, k, n = w.shape
    gap = hi - lo
    n_out = n - gap
    return pl.pallas_call(
        functools.partial(_drop_cols_kernel, first_shifted=lo // cw, gap=gap),
        grid=(depth, k // tk, n_out // cw),
        in_specs=[pl.BlockSpec((None, tk, cw), lambda l, i, j: (l, i, j)),
                  pl.BlockSpec((None, tk, gap), lambda l, i, j: (l, i, (j + 1) * (cw // gap)))],
        out_specs=pl.BlockSpec((None, tk, cw), lambda l, i, j: (l, i, j)),
        out_shape=jax.ShapeDtypeStruct((depth, k, n_out), BF16),
        compiler_params=_params("parallel", "parallel", "parallel"),
        name="drop_cols_cast",
    )(w, w)
def _matmul_kernel(x_ref, w_ref, o_ref):
    o_ref[...] = _dot(x_ref[...], w_ref[...]).astype(o_ref.dtype)


def matmul(x, w, layer, tm, tn, out_dtype=F32):
    m, k = x.shape
    n = w.shape[2]
    return pl.pallas_call(
        _matmul_kernel,
        grid=(n // tn, m // tm),
        in_specs=[pl.BlockSpec((tm, k), lambda j, i: (i, 0)),
                  pl.BlockSpec((None, k, tn), lambda j, i: (layer, 0, j))],
        out_specs=pl.BlockSpec((tm, tn), lambda j, i: (i, j)),
        out_shape=jax.ShapeDtypeStruct((m, n), out_dtype),
        compiler_params=_params("parallel", "parallel"),
        name="matmul",
    )(x, w)


def _merge_kernel(h_ref, oa_ref, ob_ref, oc_ref, om_ref, wg_ref, bg_ref, wbr_ref, out_ref):
    h = h_ref[...]
    acc = None
    for n, o_ref in enumerate((oa_ref, ob_ref, oc_ref, om_ref)):
        gate = jax.nn.sigmoid(_dot(h, wg_ref[n]) + bg_ref[n])
        term = gate * _dot(o_ref[...], wbr_ref[n])
        acc = term if acc is None else acc + term
    out_ref[...] = acc.astype(out_ref.dtype)


def gated_merge(h, branches, wg, bg, wbr, layer, tm, tn):
    m, d = h.shape
    _, nb, w, _ = wbr.shape
    resident = dict(pipeline_mode=pl.Buffered(1)) if m > tm else {}
    o_spec = pl.BlockSpec((tm, w), lambda j, i: (i, 0))
    return pl.pallas_call(
        _merge_kernel,
        grid=(d // tn, m // tm),
        in_specs=[pl.BlockSpec((tm, d), lambda j, i: (i, 0)), o_spec, o_spec, o_spec, o_spec,
                  pl.BlockSpec((None, nb, d, tn), lambda j, i: (layer, 0, 0, j), **resident),
                  pl.BlockSpec((nb, 1, tn), lambda j, i: (0, 0, j)),
                  pl.BlockSpec((None, nb, w, tn), lambda j, i: (layer, 0, 0, j), **resident)],
        out_specs=pl.BlockSpec((tm, tn), lambda j, i: (i, j)),
        out_shape=jax.ShapeDtypeStruct((m, d), BF16),
        compiler_params=_params("parallel", "parallel"),
        name="gated_merge",
    )(h, *branches, wg, bg.reshape(nb, 1, d), wbr)


def _memattn_kernel(q_ref, mk_ref, mv_ref, o_ref, *, heads):
    scale = 1.0 / math.sqrt(HEAD_M)
    for hd in range(heads):
        sl = slice(hd * HEAD_M, (hd + 1) * HEAD_M)
        q = q_ref[:, sl].astype(BF16)
        s = _dot_nt(q, mk_ref[0, :, sl]) * scale
        s = s - jnp.max(s, axis=-1, keepdims=True)
        p = jnp.exp(s)
        p = p * (1.0 / jnp.sum(p, axis=-1, keepdims=True))
        o_ref[:, sl] = _dot(p.astype(BF16), mv_ref[0, :, sl]).astype(o_ref.dtype)


def memory_attention(proj, q_col, mk, mv, nseq, t, tq):
    n_mem, w = mk.shape[1:]
    per_seq = t // tq
    mem_spec = pl.BlockSpec((1, n_mem, w), lambda i: (i // per_seq, 0, 0))
    return pl.pallas_call(
        functools.partial(_memattn_kernel, heads=w // HEAD_M),
        grid=(nseq * per_seq,),
        in_specs=[pl.BlockSpec((tq, w), lambda i: (i, q_col)), mem_spec, mem_spec],
        out_specs=pl.BlockSpec((tq, w), lambda i: (i, 0)),
        out_shape=jax.ShapeDtypeStruct((nseq * t, w), BF16),
        compiler_params=_params("parallel"),
        name="memory_attention",
    )(proj, mk, mv)


def _cmlp_kernel(u_ref, v_ref, z_ref, ws_ref, bs_ref, lnw_ref, lnb_ref, o_ref, *maybe_vn_ref, clen, chunks):
    v = v_ref[...]
    mu = jnp.mean(v, axis=-1, keepdims=True)
    var = jnp.mean(jnp.square(v - mu), axis=-1, keepdims=True)
    vn = (v - mu) * lax.rsqrt(var + NORM_EPS) * lnw_ref[...] + lnb_ref[...]
    for vn_ref in maybe_vn_ref:
        vn_ref[...] = vn
    vn16 = vn.astype(BF16)
    groups = vn.shape[1] // GROUP_C
    row = lax.broadcasted_iota(jnp.int32, (clen, clen), 0)
    col = lax.broadcasted_iota(jnp.int32, (clen, clen), 1)
    for g in range(groups):
        wm = jnp.where(row >= col, ws_ref[g], 0.0).astype(BF16)
        cs = slice(g * GROUP_C, (g + 1) * GROUP_C)
        for c in range(chunks):
            rs = slice(c * clen, (c + 1) * clen)
            s = _dot(wm, vn16[rs, cs]) + bs_ref[:, cs]
            o_ref[rs, cs] = (u_ref[rs, cs] * s * jax.nn.silu(z_ref[rs, cs])).astype(o_ref.dtype)


def chunk_mlp(proj, u_col, ws, bs_full, ln_w, ln_b, rows, clen, chunks, want_vn):
    w = ln_w.shape[0]
    tm = clen * chunks
    col = lambda c: pl.BlockSpec((tm, w), lambda i: (i, c))
    const2 = lambda a: pl.BlockSpec(a.shape, lambda i: (0, 0))
    out = pl.BlockSpec((tm, w), lambda i: (i, 0))
    out_dtypes = [BF16, F32] if want_vn else [BF16]
    return pl.pallas_call(
        functools.partial(_cmlp_kernel, clen=clen, chunks=chunks),
        grid=(rows // tm,),
        in_specs=[col(u_col), col(u_col + 1), col(u_col + 2),
                  pl.BlockSpec(ws.shape, lambda i: (0, 0, 0)), const2(bs_full),
                  pl.BlockSpec((1, w), lambda i: (0, 0)), pl.BlockSpec((1, w), lambda i: (0, 0))],
        out_specs=[out] * len(out_dtypes),
        out_shape=[jax.ShapeDtypeStruct((rows, w), dt) for dt in out_dtypes],
        compiler_params=_params("parallel"),
        name="chunk_mlp",
    )(proj, proj, proj, ws, bs_full, ln_w.reshape(1, w), ln_b.reshape(1, w))


SB_HEADS_PER_STEP = 2
SB_BLOCK = 256
SB_GROUP = 1


def _sb_group(q16, k16, v16, tk, u, masks, carry):
    tq = q16.shape[0]
    nb = k16.shape[0] // tk
    z2 = _dot_nt(q16, k16) * (math.log2(math.e) / math.sqrt(HEAD_B))
    neg_abs = lax.bitcast_convert_type(lax.bitcast_convert_type(z2, jnp.uint32) | jnp.uint32(0x80000000), F32)
    neg_l1m = jnp.maximum(z2, 0.0) + jnp.log2(1.0 + jnp.exp2(neg_abs))
    log_beta = z2 - neg_l1m
    nl = []
    for j in range(nb):
        nl_j = neg_l1m[:, j * tk:(j + 1) * tk]
        nl.append(jnp.where(masks[j], nl_j, 0.0) if j in masks else nl_j)
    lhs = jnp.concatenate([jnp.concatenate(_split2(nl_j), axis=1) for nl_j in nl], axis=0)
    cs = _dot(lhs, u)
    att = [None] * nb
    for j in reversed(range(nb)):
        cs_j = cs[j * tq:(j + 1) * tq]
        att_j = jnp.exp2(log_beta[:, j * tk:(j + 1) * tk] - cs_j - carry)
        att[j] = (jnp.where(masks[j], att_j, 0.0) if j in masks else att_j).astype(BF16)
        carry = carry + (cs_j[:, 0:1] + nl[j][:, 0:1])
    return _dot(jnp.concatenate(att, axis=1), v16), carry


def _head_slices(ref):
    return [slice(i * HEAD_B, (i + 1) * HEAD_B) for i in range(ref.shape[1] // HEAD_B)]


def _sb_prompt_kernel(*refs, tq, nq):
    q_ref, z_ref, k_ref, v_ref, u_ref = refs[:5]
    o_ref, kout_ref, vout_ref = refs[-3:]
    qi = pl.program_id(2)

    @pl.when(qi == 0)
    def _emit_kv():
        kout_ref[...] = k_ref[...]
        vout_ref[...] = v_ref[...]

    row = lax.broadcasted_iota(jnp.int32, (tq, tq), 0)
    col = lax.broadcasted_iota(jnp.int32, (tq, tq), 1)
    q_pos = qi * tq + row
    for grp in range(1, pl.cdiv(nq, SB_GROUP) + 1):
        nb = min(grp * SB_GROUP, nq)

        @pl.when(qi // SB_GROUP + 1 == grp)
        def _sweep():
            masks = {j: (j * tq + col) < q_pos for j in range((grp - 1) * SB_GROUP, nb)}
            for s in _head_slices(q_ref):
                out, _ = _sb_group(q_ref[:, s].astype(BF16), k_ref[0:nb * tq, s].astype(BF16),
                                   v_ref[0:nb * tq, s].astype(BF16), tq, u_ref[...], masks, jnp.zeros((tq, 1), F32))
                o_ref[:, s] = (out * jax.nn.silu(z_ref[:, s])).astype(o_ref.dtype)


def _sb_sample_kernel(*refs, tk):
    q_ref, z_ref, kn_ref, vn_ref, kp_ref, vp_ref, ud_ref, uf_ref = refs[:8]
    o_ref, kout_ref, vout_ref = refs[-3:]
    kout_ref[...] = kn_ref[...]
    vout_ref[...] = vn_ref[...]
    tq = q_ref.shape[0]
    row = lax.broadcasted_iota(jnp.int32, (tq, tq), 0)
    col = lax.broadcasted_iota(jnp.int32, (tq, tq), 1)
    for s in _head_slices(q_ref):
        q16 = q_ref[:, s].astype(BF16)
        out_new, carry = _sb_group(q16, kn_ref[:, s].astype(BF16), vn_ref[:, s].astype(BF16), tq, ud_ref[...],
                                   {0: col < row}, jnp.zeros((tq, 1), F32))
        out_past, _ = _sb_group(q16, kp_ref[:, s].astype(BF16), vp_ref[:, s].astype(BF16), tk, uf_ref[...],
                                {}, carry)
        o_ref[:, s] = ((out_new + out_past) * jax.nn.silu(z_ref[:, s])).astype(o_ref.dtype)


def _suffix_matrix(tk):
    s_later = lax.broadcasted_iota(jnp.int32, (tk, tk), 0)
    s_here = lax.broadcasted_iota(jnp.int32, (tk, tk), 1)
    u = (s_later > s_here).astype(BF16)
    return jnp.concatenate([u, u], axis=0)


def stick_breaking(proj, q_col, z_col, k_col, v_col, past_k, past_v, nseq, t, heads, layer, depth, kv_out):
    hp = SB_HEADS_PER_STEP
    wide = hp * HEAD_B
    tq = min(SB_BLOCK, t)
    nq = t // tq
    blk = lambda col: pl.BlockSpec((tq, wide), lambda b, h, i: (b * nq + i, col // hp + h))
    seq = lambda rows, col: pl.BlockSpec((rows, wide), lambda b, h, i: (b, col // hp + h))
    const = lambda a: pl.BlockSpec(a.shape, lambda b, h, i: (0, 0))
    if past_k is None:
        u = _suffix_matrix(tq)
        body = functools.partial(_sb_prompt_kernel, tq=tq, nq=nq)
        in_specs = [blk(q_col), blk(z_col), seq(t, k_col), seq(t, v_col), const(u)]
        args = [proj, proj, proj, proj, u]
    else:
        assert nq == 1
        n_past = past_k.shape[1] // nseq
        past = pl.BlockSpec((None, n_past, wide), lambda b, h, i: (layer, b, h))
        ud, uf = _suffix_matrix(tq), _suffix_matrix(SB_BLOCK)
        body = functools.partial(_sb_sample_kernel, tk=SB_BLOCK)
        in_specs = [blk(q_col), blk(z_col), seq(t, k_col), seq(t, v_col), past, past, const(ud), const(uf)]
        args = [proj, proj, proj, proj, past_k, past_v, ud, uf]
    aliases = {}
    if kv_out is not None:
        aliases = {len(args): 1, len(args) + 1: 2}
        in_specs = in_specs + [pl.BlockSpec(memory_space=pl.ANY)] * 2
        args = args + list(kv_out)
    kv_shape = jax.ShapeDtypeStruct((depth, nseq * t, heads * HEAD_B), F32)
    kv_spec = pl.BlockSpec((None, t, wide), lambda b, h, i: (layer, b, h))
    o_b, k_all, v_all = pl.pallas_call(
        body,
        grid=(nseq, heads // hp, nq),
        in_specs=in_specs,
        out_specs=[pl.BlockSpec((tq, wide), lambda b, h, i: (b * nq + i, h)), kv_spec, kv_spec],
        out_shape=[jax.ShapeDtypeStruct((nseq * t, heads * HEAD_B), BF16), kv_shape, kv_shape],
        input_output_aliases=aliases,
        compiler_params=_params("parallel", "parallel", "arbitrary"),
        name="stick_breaking",
    )(*args)
    return o_b, (k_all, v_all)


def _block_stack(x, lane_masks):
    return jnp.concatenate([jnp.where(m, x, 0.0) for m in lane_masks], axis=0)


def _rwkv_kernel(x_ref, lora_ref, z_ref, sh_main_ref, sh_lora_ref, s0_ref,
                 mu_main_ref, mu_lora_ref, w0_ref, wup_ref, a0_ref, aup_ref, kk_ref, ka_ref, rk_ref,
                 gnw_ref, gnb_ref, e_ref, bd_ref, ltri_ref,
                 o_ref, sout_ref, shm_out_ref, shl_out_ref,
                 xbuf, lbuf, sbd, *, chunk, width):
    c = pl.program_id(1)
    n_chunks = pl.num_programs(1)
    C, W = chunk, width
    n_seq = x_ref.shape[0]
    n_slabs = W // SLAB
    n_heads = W // HEAD_A

    @pl.when(c == 0)
    def _init():
        sbd[...] = jnp.zeros_like(sbd)
        for s in range(n_seq):
            xbuf[s, 7:8, :] = sh_main_ref[s]
            lbuf[s, 7:8, :] = sh_lora_ref[s]
            for hd in range(n_heads):
                g, j = divmod(hd, GROUP_A)
                sbd[s, g, j * HEAD_A:(j + 1) * HEAD_A, j * HEAD_A:(j + 1) * HEAD_A] = s0_ref[s, hd]

    def shifted(ref, buf, mu_ref):
        rows = []
        for s in range(n_seq):
            x = ref[s]
            buf[s, 8:8 + C, :] = x
            rows.append(x + mu_ref[...] * (buf[s, 7:7 + C, :] - x))
            buf[s, 7:8, :] = x[C - 1:C, :]
        return jnp.concatenate(rows, axis=0)

    xs = shifted(x_ref, xbuf, mu_main_ref)
    lo_s = shifted(lora_ref, lbuf, mu_lora_ref)
    seq_rows = [slice(s * C, (s + 1) * C) for s in range(n_seq)]
    slabs = [slice(g * SLAB, (g + 1) * SLAB) for g in range(n_slabs)]

    r, k, v = xs[:, :W], xs[:, W:2 * W], xs[:, 2 * W:]
    w_pre = w0_ref[...] + _dot(jnp.tanh(lo_s).astype(BF16), wup_ref[...])
    ld = -math.exp(-0.5) * jax.nn.sigmoid(w_pre)
    a = jax.nn.sigmoid(a0_ref[...] + _dot(lo_s.astype(BF16), aup_ref[...]))

    e_mat = e_ref[...]

    def seg_sum(val):
        n = val.shape[0]
        hi, lo = _split2(val)
        out = _dot(jnp.concatenate([part[:, sl] for part in (hi, lo) for sl in slabs], axis=0), e_mat)
        return jnp.concatenate([out[g * n:(g + 1) * n] + out[(n_slabs + g) * n:(n_slabs + g + 1) * n]
                                for g in range(n_slabs)], axis=1)

    kk = k * kk_ref[...]
    kk = kk * lax.rsqrt(jnp.maximum(seg_sum(kk * kk), 1e-24))
    kmod = k * (1.0 + (a - 1.0) * ka_ref[...])

    ld_hi = ld.astype(BF16)
    ld_r1 = ld - ld_hi.astype(F32)
    ld_mid = ld_r1.astype(BF16)
    ld_lo = (ld_r1 - ld_mid.astype(F32)).astype(BF16)
    ltri = ltri_ref[...]
    lp = _dot(ltri, ld_hi) + _dot(ltri, ld_mid) + _dot(ltri, ld_lo)
    lp_last = [lp[rs.stop - 1:rs.stop, :] for rs in seq_rows]
    lp_end = jnp.concatenate([jnp.broadcast_to(row, (C, W)) for row in lp_last], axis=0)
    e_neg = jnp.exp(-lp)
    kka = kk * a
    kap = kk * jnp.exp(lp - ld)
    bet = kka * e_neg
    kt = kmod * e_neg
    rt = r * jnp.exp(lp)
    e_end = jnp.exp(lp_end - lp)
    kt_end = kmod * e_end
    bet_end = kka * e_end
    dec_end = [jnp.exp(row) for row in lp_last]

    lane = lax.broadcasted_iota(jnp.int32, (1, SLAB), 1)
    lane_masks = [(lane >= j * HEAD_A) & (lane < (j + 1) * HEAD_A) for j in range(GROUP_A)]
    t_row = lax.broadcasted_iota(jnp.int32, (C, GROUP_A * C), 0)
    s_col = lax.broadcasted_iota(jnp.int32, (C, GROUP_A * C), 1) & (C - 1)
    strict, incl = s_col < t_row, s_col <= t_row
    stack = lambda val16: _block_stack(val16, lane_masks)
    nh = GROUP_A * C
    units = [(s, g) for s in range(n_seq) for g in range(n_slabs)]
    per_slab = lambda fn: [fn(u, seq_rows[s], slabs[g]) for u, (s, g) in enumerate(units)]

    kap16, rt16, bet16, kt16, v16 = (val.astype(BF16) for val in (kap, rt, bet, kt, v))
    s_old = [sbd[s, g] for s, g in units]
    lhs = per_slab(lambda g, rs, sl: jnp.concatenate([kap16[rs, sl], rt16[rs, sl]], axis=0))
    rhs = per_slab(lambda g, rs, sl: jnp.concatenate([stack(bet16[rs, sl]), stack(kt16[rs, sl])], axis=0))
    sc = per_slab(lambda g, rs, sl: _dot_nt(lhs[g], rhs[g]))
    ls = per_slab(lambda g, rs, sl: _dot_nt(lhs[g], s_old[g].astype(BF16)))
    v_stack = per_slab(lambda g, rs, sl: stack(v16[rs, sl]))
    p16 = per_slab(lambda g, rs, sl: jnp.where(strict, -sc[g][:C, :nh], 0.0).astype(BF16))
    xw = per_slab(lambda g, rs, sl: ls[g][:C]
                  + _dot(jnp.where(strict, sc[g][:C, nh:], 0.0).astype(BF16), v_stack[g]))
    for step in range(6):
        xw = per_slab(lambda g, rs, sl: xw[g] + _dot(p16[g], stack(xw[g].astype(BF16))))
        if step < 5:
            p16 = per_slab(lambda g, rs, sl: _dot(p16[g], stack(p16[g])).astype(BF16))
    ab_inc = per_slab(lambda g, rs, sl: jnp.concatenate(
        [jnp.where(incl, sc[g][C:, :nh], 0.0), jnp.where(incl, sc[g][C:, nh:], 0.0)], axis=1).astype(BF16))
    y = per_slab(lambda g, rs, sl: ls[g][C:] + _dot(
        ab_inc[g], jnp.concatenate([stack((-xw[g]).astype(BF16)), v_stack[g]], axis=0)))
    upd = per_slab(lambda g, rs, sl: _dot(
        jnp.concatenate([v[rs, sl], -xw[g]], axis=0).T.astype(BF16),
        jnp.concatenate([kt_end[rs, sl], bet_end[rs, sl]], axis=0).astype(BF16)))
    for u, (s, g) in enumerate(units):
        sbd[s, g] = s_old[u] * dec_end[s][:, slabs[g]] + upd[u] * bd_ref[...]

    y = jnp.concatenate([jnp.concatenate(y[s * n_slabs:(s + 1) * n_slabs], axis=1) for s in range(n_seq)], axis=0)
    inv_n = 1.0 / HEAD_A
    mean = seg_sum(y) * inv_n
    yc = y - mean
    var = seg_sum(yc * yc) * inv_n
    y = yc * lax.rsqrt(var + GN_EPS) * gnw_ref[...] + gnb_ref[...]
    y = y + seg_sum(r * kmod * rk_ref[...]) * v
    for s, rs in enumerate(seq_rows):
        o_ref[s] = (y[rs] * jax.nn.silu(z_ref[s])).astype(o_ref.dtype)

    @pl.when(c == n_chunks - 1)
    def _fin():
        for s in range(n_seq):
            shm_out_ref[s] = xbuf[s, 7:8, :]
            shl_out_ref[s] = lbuf[s, 7:8, :]
            for hd in range(n_heads):
                g, j = divmod(hd, GROUP_A)
                sout_ref[s, hd] = sbd[s, g, j * HEAD_A:(j + 1) * HEAD_A, j * HEAD_A:(j + 1) * HEAD_A]


RWKV_SEQS_PER_STEP = 2


def rwkv7(proj, lora, z_col, shift_main, shift_lora, s0, prm, nseq, t):
    W = prm["a_w0"].shape[0]
    C = RWKV_CHUNK
    S = RWKV_SEQS_PER_STEP
    nc = t // C
    n_heads = W // HEAD_A
    lw = lora.shape[1]
    half = lw // 2
    zpad = jnp.zeros((half, W), F32)
    wup = jnp.concatenate([prm["a_w_up"], zpad], axis=0).astype(BF16)
    aup = jnp.concatenate([zpad, prm["a_a_up"]], axis=0).astype(BF16)
    idx = jnp.arange(SLAB) // HEAD_A
    same = idx[:, None] == idx[None, :]
    ti = jnp.arange(S * C)
    ltri = ((ti[:, None] >= ti[None, :]) & (ti[:, None] // C == ti[None, :] // C)).astype(BF16)
    row1 = lambda a: a.reshape(1, -1)
    vec = lambda n: pl.BlockSpec((1, n), lambda b, c: (0, 0))
    full = lambda a: pl.BlockSpec(a.shape, lambda b, c: (0, 0))
    rows = lambda n, col: pl.BlockSpec((S, C, n), lambda b, c: (b, c, col))
    per_seq = lambda n: pl.BlockSpec((S, 1, n), lambda b, c: (b, 0, 0))
    state = pl.BlockSpec((S, n_heads, HEAD_A, HEAD_A), lambda b, c: (b, 0, 0, 0))
    e_mat, bd_mask = same.astype(BF16), same.astype(F32)
    proj3 = proj.reshape(nseq, t, proj.shape[1])
    o_a, s_new, sh_main, sh_lora = pl.pallas_call(
        functools.partial(_rwkv_kernel, chunk=C, width=W),
        grid=(nseq // S, nc),
        in_specs=[rows(3 * W, 0), rows(lw, 0), rows(W, z_col),
                  per_seq(3 * W), per_seq(lw), state,
                  vec(3 * W), vec(lw), vec(W), full(wup), vec(W), full(aup), vec(W), vec(W), vec(W), vec(W), vec(W),
                  full(e_mat), full(bd_mask), full(ltri)],
        out_specs=[rows(W, 0), state, per_seq(3 * W), per_seq(lw)],
        out_shape=[jax.ShapeDtypeStruct((nseq, t, W), BF16),
                   jax.ShapeDtypeStruct((nseq, n_heads, HEAD_A, HEAD_A), F32),
                   jax.ShapeDtypeStruct((nseq, 1, 3 * W), F32),
                   jax.ShapeDtypeStruct((nseq, 1, lw), F32)],
        scratch_shapes=[pltpu.VMEM((S, C + 8, 3 * W), F32), pltpu.VMEM((S, C + 8, lw), F32),
                        pltpu.VMEM((S, W // SLAB, SLAB, SLAB), F32)],
        compiler_params=_params("parallel", "arbitrary"),
        name="rwkv7",
    )(proj3, lora.reshape(nseq, t, lw), proj3, shift_main, shift_lora, s0,
      row1(prm["a_mu"][:3 * W]), row1(prm["a_mu"][3 * W:]), row1(prm["a_w0"]), wup, row1(prm["a_a0"]), aup,
      row1(prm["a_k_k"]), row1(prm["a_k_a"]), row1(prm["a_r_k"]), row1(prm["a_gn_w"]), row1(prm["a_gn_b"]),
      e_mat, bd_mask, ltri)
    return o_a.reshape(nseq * t, W), s_new, sh_main, sh_lora


COL_AZ, COL_BQ, COL_BK, COL_BV, COL_BZ, COL_CU, COL_MQ = 3, 4, 5, 6, 7, 8, 11


def trunk_layer(x, h, nseq, t, mk, mv, shift_prev, s_prev, past_k, past_v, prm, layer, depth, kv_out, want_vn):
    m, d = x.shape
    W = d // 4
    tm = min(512, m)
    if h is None:
        h = rmsnorm_rows(x, prm["g_pre"], min(256, m))
    proj = matmul(h, prm["w_in_main"], layer, min(1024, m), 1024)
    lora = matmul(h, prm["w_in_lora"], layer, tm, prm["w_in_lora"].shape[2])

    o_a, s_new, sh_main, sh_lora = rwkv7(proj, lora, COL_AZ, shift_prev[..., :3 * W], shift_prev[..., 3 * W:],
                                         s_prev, prm, nseq, t)
    shift_new = jnp.concatenate([sh_main, sh_lora], axis=-1)

    heads_b = W // HEAD_B
    hb = lambda col: col * heads_b
    o_b, kv_out = stick_breaking(proj, hb(COL_BQ), hb(COL_BZ), hb(COL_BK), hb(COL_BV), past_k, past_v,
                                 nseq, t, heads_b, layer, depth, kv_out)

    clen = min(t, prm["c_ws"].shape[1])
    groups = prm["c_ws"].shape[0]
    bs_full = jnp.repeat(prm["c_bs"][:, :clen].T, W // groups, axis=1)
    o_c, *vn_c = chunk_mlp(proj, COL_CU, prm["c_ws"][:, :clen, :clen], bs_full, prm["c_ln_w"], prm["c_ln_b"],
                           m, clen, max(1, min(512, t) // clen), want_vn)

    o_m = memory_attention(proj, COL_MQ, mk, mv, nseq, t, min(512, t))

    merged = gated_merge(h, (o_a, o_b, o_c, o_m), prm["w_gate"], prm["b_gate"], prm["w_br"], layer, tm,
                         512 if m > tm else 256)
    x_new, h_next = out_proj_norm_residual(merged, prm["w_out"], layer, x, prm["g_post"], prm["g_pre_next"], tm, 512)
    return x_new, h_next, shift_new, s_new, kv_out, vn_c


def kernel(x_prompt, x_sample, cache_mem_k, cache_mem_v, cache_sb_k, cache_sb_v, state_rwkv, state_shift, mem_prompt, g_pre, g_post, w_in, a_mu, a_w0, a_w_up, a_a0, a_a_up, a_k_k, a_k_a, a_r_k, a_gn_w, a_gn_b, c_ws, c_bs, c_ln_w, c_ln_b, g_mem, w_mem_kv, w_gate, b_gate, w_br, w_out):
    bp, tp, d = x_prompt.shape
    bs_, ts, _ = x_sample.shape
    depth = w_in.shape[0]
    W = d // 4
    n_mem = mem_prompt.shape[1]
    m_heads = cache_mem_k.shape[3]
    b_heads = cache_sb_k.shape[3]
    a_heads = state_rwkv.shape[2]
    shift_w = state_shift.shape[-1]
    n_past = cache_sb_k.shape[2]
    lora_lo, lora_hi = 3 * W, shift_w

    yp = x_prompt.reshape(bp * tp, d)
    ys = x_sample.reshape(bs_ * ts, d)
    mem_rows = mem_prompt.reshape(bp * n_mem, d)
    mem_out, rw_p, sh_p, rw_s, sh_s, cv_s = [[], []], [], [], [], [], []
    kv_p = kv_s = hp = hs = None
    dense = {
        "w_in_main": drop_cols_cast(w_in, lora_lo, lora_hi, W, 1024),
        "w_in_lora": w_in[:, :, lora_lo:lora_hi].astype(BF16),
        "w_gate": w_gate.astype(BF16), "w_br": w_br.astype(BF16), "w_out": w_out.astype(BF16),
    }
    w_mem16 = w_mem_kv.astype(BF16)
    past_k = cache_sb_k.reshape(depth, bs_ * n_past, W)
    past_v = cache_sb_v.reshape(depth, bs_ * n_past, W)
    for l in range(depth):
        prm = dict(dense)
        prm.update({
            "g_pre": g_pre[l], "g_post": g_post[l], "g_pre_next": g_pre[l + 1] if l + 1 < depth else None,
            "a_mu": a_mu[l], "a_w0": a_w0[l], "a_w_up": a_w_up[l], "a_a0": a_a0[l], "a_a_up": a_a_up[l],
            "a_k_k": a_k_k[l].reshape(-1), "a_k_a": a_k_a[l].reshape(-1), "a_r_k": a_r_k[l].reshape(-1),
            "a_gn_w": a_gn_w[l].reshape(-1), "a_gn_b": a_gn_b[l].reshape(-1),
            "c_ws": c_ws[l], "c_bs": c_bs[l], "c_ln_w": c_ln_w[l], "c_ln_b": c_ln_b[l], "b_gate": b_gate[l],
        })
        kv = matmul(rmsnorm_rows(mem_rows, g_mem[l], 256), w_mem16, l, 512, 1024)
        mk = kv[:, :W].reshape(bp, n_mem, W)
        mv = kv[:, W:].reshape(bp, n_mem, W)
        shift0 = jnp.zeros((bp, 1, shift_w), F32)
        s0 = jnp.zeros((bp, a_heads, HEAD_A, HEAD_A), F32)
        yp, hp, sh, st, kv_p, _ = trunk_layer(yp, hp, bp, tp, mk.astype(BF16), mv.astype(BF16), shift0, s0, None, None,
                                              prm, l, depth, kv_p, False)
        mem_out[0].append(mk.reshape(bp, n_mem, m_heads, W // m_heads))
        mem_out[1].append(mv.reshape(bp, n_mem, m_heads, W // m_heads))
        rw_p.append(st)
        sh_p.append(sh)
        ys, hs, sh, st, kv_s, cvn = trunk_layer(
            ys, hs, bs_, ts, cache_mem_k[l].reshape(bs_, n_mem, W).astype(BF16),
            cache_mem_v[l].reshape(bs_, n_mem, W).astype(BF16), state_shift[l], state_rwkv[l],
            past_k, past_v, prm, l, depth, kv_s, True)
        rw_s.append(st)
        sh_s.append(sh)
        cv_s.append(cvn[0].reshape(bs_, ts, W))
    heads_p = lambda a: a.reshape(depth, bp, tp, b_heads, HEAD_B)
    heads_s = lambda a: a.reshape(depth, bs_, ts, b_heads, HEAD_B)
    return (yp.reshape(bp, tp, d), ys.reshape(bs_, ts, d), jnp.stack(mem_out[0]), jnp.stack(mem_out[1]),
            heads_p(kv_p[0]), heads_p(kv_p[1]), jnp.stack(rw_p), jnp.stack(sh_p),
            heads_s(kv_s[0]), heads_s(kv_s[1]), jnp.stack(rw_s), jnp.stack(sh_s), jnp.stack(cv_s))
```

```python
import functools
import math

import jax
import jax.numpy as jnp
from jax import lax
from jax.experimental import pallas as pl
from jax.experimental.pallas import tpu as pltpu

F32 = jnp.float32
BF16 = jnp.bfloat16

NORM_EPS = 1e-6
GN_EPS = 64e-5
HEAD_A = 64
GROUP_A = 4
SLAB = HEAD_A * GROUP_A
HEAD_B = 128
HEAD_M = 256
GROUP_C = 128
RWKV_CHUNK = 64
VMEM_LIMIT_BYTES = 56 * 1024 * 1024


def _params(*sem):
    return pltpu.CompilerParams(dimension_semantics=sem, vmem_limit_bytes=VMEM_LIMIT_BYTES)


def _split2(x):
    hi = x.astype(BF16)
    lo = (x - hi.astype(F32)).astype(BF16)
    return hi, lo


def _dot(a, b):
    return jnp.dot(a, b, preferred_element_type=F32)


def _dot_nt(a, b):
    return lax.dot_general(a, b, (((1,), (1,)), ((), ())), preferred_element_type=F32)


def _rmsnorm_kernel(x_ref, g_ref, o_ref):
    x = x_ref[...]
    ms = jnp.mean(x * x, axis=-1, keepdims=True)
    o_ref[...] = (x * lax.rsqrt(ms + NORM_EPS) * g_ref[...]).astype(o_ref.dtype)


def rmsnorm_rows(x, g, tm):
    m, d = x.shape
    return pl.pallas_call(
        _rmsnorm_kernel,
        grid=(m // tm,),
        in_specs=[pl.BlockSpec((tm, d), lambda i: (i, 0)), pl.BlockSpec((1, d), lambda i: (0, 0))],
        out_specs=pl.BlockSpec((tm, d), lambda i: (i, 0)),
        out_shape=jax.ShapeDtypeStruct((m, d), BF16),
        compiler_params=_params("parallel"),
        name="rmsnorm_rows",
    )(x, g.reshape(1, d))


POST_ROWS = 64


def _out_proj_kernel(*refs, tn, with_next):
    if with_next:
        m_ref, w_ref, x_ref, g_ref, gn_ref, o_ref, hn_ref, x_rows, ssq = refs
    else:
        m_ref, w_ref, x_ref, g_ref, o_ref, x_rows, ssq = refs
    j = pl.program_id(1)
    cols = pl.ds(pl.multiple_of(j * tn, tn), tn)
    y = _dot(m_ref[...], w_ref[...])
    sq = jnp.sum(y * y, axis=-1, keepdims=True)
    o_ref[:, cols] = y * g_ref[...]
    x_rows[:, cols] = x_ref[...]

    @pl.when(j == 0)
    def _first():
        ssq[...] = sq

    @pl.when(j > 0)
    def _rest():
        ssq[...] += sq

    @pl.when(j == pl.num_programs(1) - 1)
    def _normalise():
        inv_d = 1.0 / o_ref.shape[1]
        for r0 in range(0, o_ref.shape[0], POST_ROWS):
            rows = slice(r0, r0 + POST_ROWS)
            scale = lax.rsqrt(ssq[rows, :] * inv_d + NORM_EPS)
            x_new = x_rows[rows, :] + o_ref[rows, :] * scale
            o_ref[rows, :] = x_new
            if with_next:
                ms = jnp.mean(x_new * x_new, axis=-1, keepdims=True)
                hn_ref[rows, :] = (x_new * lax.rsqrt(ms + NORM_EPS) * gn_ref[...]).astype(hn_ref.dtype)


def out_proj_norm_residual(merged, w, layer, x, g, g_next, tm, tn):
    m, d = x.shape
    with_next = g_next is not None
    row = pl.BlockSpec((tm, d), lambda i, j: (i, 0))
    gain = pl.BlockSpec((1, d), lambda i, j: (0, 0))
    out = pl.pallas_call(
        functools.partial(_out_proj_kernel, tn=tn, with_next=with_next),
        grid=(m // tm, d // tn),
        in_specs=[row, pl.BlockSpec((None, d, tn), lambda i, j: (layer, 0, j)),
                  pl.BlockSpec((tm, tn), lambda i, j: (i, j)), pl.BlockSpec((1, tn), lambda i, j: (0, j))]
                 + ([gain] if with_next else []),
        out_specs=[row, row] if with_next else [row],
        out_shape=[jax.ShapeDtypeStruct((m, d), F32)] + ([jax.ShapeDtypeStruct((m, d), BF16)] if with_next else []),
        scratch_shapes=[pltpu.VMEM((tm, d), F32), pltpu.VMEM((tm, 1), F32)],
        compiler_params=_params("parallel", "arbitrary"),
        name="out_proj_norm_residual",
    )(merged, w, x, g.reshape(1, d), *([g_next.reshape(1, d)] if with_next else []))
    return out if with_next else (out[0], None)


def _drop_cols_kernel(a_ref, b_ref, o_ref, *, first_shifted, gap):
    j = pl.program_id(2)

    @pl.when(j < first_shifted)
    def _before_gap():
        o_ref[...] = a_ref[...].astype(o_ref.dtype)

    @pl.when(j >= first_shifted)
    def _after_gap():
        o_ref[...] = jnp.concatenate([a_ref[:, gap:], b_ref[...]], axis=1).astype(o_ref.dtype)


def drop_cols_cast(w, lo, hi, cw, tk):
    depth, k, n = w.shape
    gap = hi - lo
    n_out = n - gap
    return pl.pallas_call(
        functools.partial(_drop_cols_kernel, first_shifted=lo // cw, gap=gap),
        grid=(depth, k // tk, n_out // cw),
        in_specs=[pl.BlockSpec((None, tk, cw), lambda l, i, j: (l, i, j)),
                  pl.BlockSpec((None, tk, gap), lambda l, i, j: (l, i, (j + 1) * (cw // gap)))],
        out_specs=pl.BlockSpec((None, tk, cw), lambda l, i, j: (l, i, j)),
        out_shape=jax.ShapeDtypeStruct((depth, k, n_out), BF16),
        compiler_params=_params("parallel", "parallel", "parallel"),
        name="drop_cols_cast",
    )(w, w)
def _matmul_kernel(x_ref, w_ref, o_ref):
    o_ref[...] = _dot(x_ref[...], w_ref[...]).astype(o_ref.dtype)


def matmul(x, w, layer, tm, tn, out_dtype=F32):
    m, k = x.shape
    n = w.shape[2]
    return pl.pallas_call(
        _matmul_kernel,
        grid=(n // tn, m // tm),
        in_specs=[pl.BlockSpec((tm, k), lambda j, i: (i, 0)),
                  pl.BlockSpec((None, k, tn), lambda j, i: (layer, 0, j))],
        out_specs=pl.BlockSpec((tm, tn), lambda j, i: (i, j)),
        out_shape=jax.ShapeDtypeStruct((m, n), out_dtype),
        compiler_params=_params("parallel", "parallel"),
        name="matmul",
    )(x, w)


def _merge_kernel(h_ref, oa_ref, ob_ref, oc_ref, om_ref, wg_ref, bg_ref, wbr_ref, out_ref):
    h = h_ref[...]
    acc = None
    for n, o_ref in enumerate((oa_ref, ob_ref, oc_ref, om_ref)):
        gate = jax.nn.sigmoid(_dot(h, wg_ref[n]) + bg_ref[n])
        term = gate * _dot(o_ref[...], wbr_ref[n])
        acc = term if acc is None else acc + term
    out_ref[...] = acc.astype(out_ref.dtype)


def gated_merge(h, branches, wg, bg, wbr, layer, tm, tn):
    m, d = h.shape
    _, nb, w, _ = wbr.shape
    resident = dict(pipeline_mode=pl.Buffered(1)) if m > tm else {}
    o_spec = pl.BlockSpec((tm, w), lambda j, i: (i, 0))
    return pl.pallas_call(
        _merge_kernel,
        grid=(d // tn, m // tm),
        in_specs=[pl.BlockSpec((tm, d), lambda j, i: (i, 0)), o_spec, o_spec, o_spec, o_spec,
                  pl.BlockSpec((None, nb, d, tn), lambda j, i: (layer, 0, 0, j), **resident),
                  pl.BlockSpec((nb, 1, tn), lambda j, i: (0, 0, j)),
                  pl.BlockSpec((None, nb, w, tn), lambda j, i: (layer, 0, 0, j), **resident)],
        out_specs=pl.BlockSpec((tm, tn), lambda j, i: (i, j)),
        out_shape=jax.ShapeDtypeStruct((m, d), BF16),
        compiler_params=_params("parallel", "parallel"),
        name="gated_merge",
    )(h, *branches, wg, bg.reshape(nb, 1, d), wbr)


def _memattn_kernel(q_ref, mk_ref, mv_ref, o_ref, *, heads):
    scale = 1.0 / math.sqrt(HEAD_M)
    for hd in range(heads):
        sl = slice(hd * HEAD_M, (hd + 1) * HEAD_M)
        q = q_ref[:, sl].astype(BF16)
        s = _dot_nt(q, mk_ref[0, :, sl]) * scale
        s = s - jnp.max(s, axis=-1, keepdims=True)
        p = jnp.exp(s)
        p = p * (1.0 / jnp.sum(p, axis=-1, keepdims=True))
        o_ref[:, sl] = _dot(p.astype(BF16), mv_ref[0, :, sl]).astype(o_ref.dtype)


def memory_attention(proj, q_col, mk, mv, nseq, t, tq):
    n_mem, w = mk.shape[1:]
    per_seq = t // tq
    mem_spec = pl.BlockSpec((1, n_mem, w), lambda i: (i // per_seq, 0, 0))
    return pl.pallas_call(
        functools.partial(_memattn_kernel, heads=w // HEAD_M),
        grid=(nseq * per_seq,),
        in_specs=[pl.BlockSpec((tq, w), lambda i: (i, q_col)), mem_spec, mem_spec],
        out_specs=pl.BlockSpec((tq, w), lambda i: (i, 0)),
        out_shape=jax.ShapeDtypeStruct((nseq * t, w), BF16),
        compiler_params=_params("parallel"),
        name="memory_attention",
    )(proj, mk, mv)


def _cmlp_kernel(u_ref, v_ref, z_ref, ws_ref, bs_ref, lnw_ref, lnb_ref, o_ref, *maybe_vn_ref, clen, chunks):
    v = v_ref[...]
    mu = jnp.mean(v, axis=-1, keepdims=True)
    var = jnp.mean(jnp.square(v - mu), axis=-1, keepdims=True)
    vn = (v - mu) * lax.rsqrt(var + NORM_EPS) * lnw_ref[...] + lnb_ref[...]
    for vn_ref in maybe_vn_ref:
        vn_ref[...] = vn
    vn16 = vn.astype(BF16)
    groups = vn.shape[1] // GROUP_C
    row = lax.broadcasted_iota(jnp.int32, (clen, clen), 0)
    col = lax.broadcasted_iota(jnp.int32, (clen, clen), 1)
    for g in range(groups):
        wm = jnp.where(row >= col, ws_ref[g], 0.0).astype(BF16)
        cs = slice(g * GROUP_C, (g + 1) * GROUP_C)
        for c in range(chunks):
            rs = slice(c * clen, (c + 1) * clen)
            s = _dot(wm, vn16[rs, cs]) + bs_ref[:, cs]
            o_ref[rs, cs] = (u_ref[rs, cs] * s * jax.nn.silu(z_ref[rs, cs])).astype(o_ref.dtype)


def chunk_mlp(proj, u_col, ws, bs_full, ln_w, ln_b, rows, clen, chunks, want_vn):
    w = ln_w.shape[0]
    tm = clen * chunks
    col = lambda c: pl.BlockSpec((tm, w), lambda i: (i, c))
    const2 = lambda a: pl.BlockSpec(a.shape, lambda i: (0, 0))
    out = pl.BlockSpec((tm, w), lambda i: (i, 0))
    out_dtypes = [BF16, F32] if want_vn else [BF16]
    return pl.pallas_call(
        functools.partial(_cmlp_kernel, clen=clen, chunks=chunks),
        grid=(rows // tm,),
        in_specs=[col(u_col), col(u_col + 1), col(u_col + 2),
                  pl.BlockSpec(ws.shape, lambda i: (0, 0, 0)), const2(bs_full),
                  pl.BlockSpec((1, w), lambda i: (0, 0)), pl.BlockSpec((1, w), lambda i: (0, 0))],
        out_specs=[out] * len(out_dtypes),
        out_shape=[jax.ShapeDtypeStruct((rows, w), dt) for dt in out_dtypes],
        compiler_params=_params("parallel"),
        name="chunk_mlp",
    )(proj, proj, proj, ws, bs_full, ln_w.reshape(1, w), ln_b.reshape(1, w))


SB_HEADS_PER_STEP = 2
SB_BLOCK = 256
SB_GROUP = 1


def _sb_group(q16, k16, v16, tk, u, masks, carry):
    tq = q16.shape[0]
    nb = k16.shape[0] // tk
    z2 = _dot_nt(q16, k16) * (math.log2(math.e) / math.sqrt(HEAD_B))
    neg_abs = lax.bitcast_convert_type(lax.bitcast_convert_type(z2, jnp.uint32) | jnp.uint32(0x80000000), F32)
    neg_l1m = jnp.maximum(z2, 0.0) + jnp.log2(1.0 + jnp.exp2(neg_abs))
    log_beta = z2 - neg_l1m
    nl = []
    for j in range(nb):
        nl_j = neg_l1m[:, j * tk:(j + 1) * tk]
        nl.append(jnp.where(masks[j], nl_j, 0.0) if j in masks else nl_j)
    lhs = jnp.concatenate([jnp.concatenate(_split2(nl_j), axis=1) for nl_j in nl], axis=0)
    cs = _dot(lhs, u)
    att = [None] * nb
    for j in reversed(range(nb)):
        cs_j = cs[j * tq:(j + 1) * tq]
        att_j = jnp.exp2(log_beta[:, j * tk:(j + 1) * tk] - cs_j - carry)
        att[j] = (jnp.where(masks[j], att_j, 0.0) if j in masks else att_j).astype(BF16)
        carry = carry + (cs_j[:, 0:1] + nl[j][:, 0:1])
    return _dot(jnp.concatenate(att, axis=1), v16), carry


def _head_slices(ref):
    return [slice(i * HEAD_B, (i + 1) * HEAD_B) for i in range(ref.shape[1] // HEAD_B)]


def _sb_prompt_kernel(*refs, tq, nq):
    q_ref, z_ref, k_ref, v_ref, u_ref = refs[:5]
    o_ref, kout_ref, vout_ref = refs[-3:]
    qi = pl.program_id(2)

    @pl.when(qi == 0)
    def _emit_kv():
        kout_ref[...] = k_ref[...]
        vout_ref[...] = v_ref[...]

    row = lax.broadcasted_iota(jnp.int32, (tq, tq), 0)
    col = lax.broadcasted_iota(jnp.int32, (tq, tq), 1)
    q_pos = qi * tq + row
    for grp in range(1, pl.cdiv(nq, SB_GROUP) + 1):
        nb = min(grp * SB_GROUP, nq)

        @pl.when(qi // SB_GROUP + 1 == grp)
        def _sweep():
            masks = {j: (j * tq + col) < q_pos for j in range((grp - 1) * SB_GROUP, nb)}
            for s in _head_slices(q_ref):
                out, _ = _sb_group(q_ref[:, s].astype(BF16), k_ref[0:nb * tq, s].astype(BF16),
                                   v_ref[0:nb * tq, s].astype(BF16), tq, u_ref[...], masks, jnp.zeros((tq, 1), F32))
                o_ref[:, s] = (out * jax.nn.silu(z_ref[:, s])).astype(o_ref.dtype)


def _sb_sample_kernel(*refs, tk):
    q_ref, z_ref, kn_ref, vn_ref, kp_ref, vp_ref, ud_ref, uf_ref = refs[:8]
    o_ref, kout_ref, vout_ref = refs[-3:]
    kout_ref[...] = kn_ref[...]
    vout_ref[...] = vn_ref[...]
    tq = q_ref.shape[0]
    row = lax.broadcasted_iota(jnp.int32, (tq, tq), 0)
    col = lax.broadcasted_iota(jnp.int32, (tq, tq), 1)
    heads = _head_slices(q_ref)
    n_past = kp_ref.shape[0] // len(heads)
    for hd, s in enumerate(heads):
        q16 = q_ref[:, s].astype(BF16)
        out_new, carry = _sb_group(q16, kn_ref[:, s].astype(BF16), vn_ref[:, s].astype(BF16), tq, ud_ref[...],
                                   {0: col < row}, jnp.zeros((tq, 1), F32))
        own_rows = pl.ds(hd, n_past, stride=len(heads))
        out_past, _ = _sb_group(q16, kp_ref[own_rows, :].astype(BF16), vp_ref[own_rows, :].astype(BF16), tk,
                                uf_ref[...], {}, carry)
        o_ref[:, s] = ((out_new + out_past) * jax.nn.silu(z_ref[:, s])).astype(o_ref.dtype)


def _suffix_matrix(tk):
    s_later = lax.broadcasted_iota(jnp.int32, (tk, tk), 0)
    s_here = lax.broadcasted_iota(jnp.int32, (tk, tk), 1)
    u = (s_later > s_here).astype(BF16)
    return jnp.concatenate([u, u], axis=0)


def stick_breaking(proj, q_col, z_col, k_col, v_col, past_k, past_v, nseq, t, heads, layer, depth, kv_out):
    hp = SB_HEADS_PER_STEP if past_k is None else heads
    wide = hp * HEAD_B
    tq = min(SB_BLOCK, t)
    nq = t // tq
    blk = lambda col: pl.BlockSpec((tq, wide), lambda b, h, i: (b * nq + i, col // hp + h))
    seq = lambda rows, col: pl.BlockSpec((rows, wide), lambda b, h, i: (b, col // hp + h))
    const = lambda a: pl.BlockSpec(a.shape, lambda b, h, i: (0, 0))
    if past_k is None:
        u = _suffix_matrix(tq)
        body = functools.partial(_sb_prompt_kernel, tq=tq, nq=nq)
        in_specs = [blk(q_col), blk(z_col), seq(t, k_col), seq(t, v_col), const(u)]
        args = [proj, proj, proj, proj, u]
    else:
        assert nq == 1
        n_past = past_k.shape[2]
        past = pl.BlockSpec((None, None, n_past * heads, HEAD_B), lambda b, h, i: (layer, b, 0, 0))
        past_k, past_v = (a.reshape(depth, nseq, n_past * heads, HEAD_B) for a in (past_k, past_v))
        ud, uf = _suffix_matrix(tq), _suffix_matrix(SB_BLOCK)
        body = functools.partial(_sb_sample_kernel, tk=SB_BLOCK)
        in_specs = [blk(q_col), blk(z_col), seq(t, k_col), seq(t, v_col), past, past, const(ud), const(uf)]
        args = [proj, proj, proj, proj, past_k, past_v, ud, uf]
    aliases = {}
    if kv_out is not None:
        aliases = {len(args): 1, len(args) + 1: 2}
        in_specs = in_specs + [pl.BlockSpec(memory_space=pl.ANY)] * 2
        args = args + list(kv_out)
    kv_shape = jax.ShapeDtypeStruct((depth, nseq * t, heads * HEAD_B), F32)
    kv_spec = pl.BlockSpec((None, t, wide), lambda b, h, i: (layer, b, h))
    o_b, k_all, v_all = pl.pallas_call(
        body,
        grid=(nseq, heads // hp, nq),
        in_specs=in_specs,
        out_specs=[pl.BlockSpec((tq, wide), lambda b, h, i: (b * nq + i, h)), kv_spec, kv_spec],
        out_shape=[jax.ShapeDtypeStruct((nseq * t, heads * HEAD_B), BF16), kv_shape, kv_shape],
        input_output_aliases=aliases,
        compiler_params=_params("parallel", "parallel", "arbitrary"),
        name="stick_breaking",
    )(*args)
    return o_b, (k_all, v_all)


def _block_stack(x, lane_masks):
    return jnp.concatenate([jnp.where(m, x, 0.0) for m in lane_masks], axis=0)


def _rwkv_kernel(x_ref, lora_ref, z_ref, sh_main_ref, sh_lora_ref, s0_ref,
                 mu_main_ref, mu_lora_ref, w0_ref, wup_ref, a0_ref, aup_ref, kk_ref, ka_ref, rk_ref,
                 gnw_ref, gnb_ref, e_ref, bd_ref, ltri_ref,
                 o_ref, sout_ref, shm_out_ref, shl_out_ref,
                 xbuf, lbuf, sbd, *, chunk, width):
    c = pl.program_id(1)
    n_chunks = pl.num_programs(1)
    C, W = chunk, width
    n_seq = x_ref.shape[0]
    n_slabs = W // SLAB
    n_heads = W // HEAD_A

    @pl.when(c == 0)
    def _init():
        sbd[...] = jnp.zeros_like(sbd)
        for s in range(n_seq):
            xbuf[s, 7:8, :] = sh_main_ref[s]
            lbuf[s, 7:8, :] = sh_lora_ref[s]
            for hd in range(n_heads):
                g, j = divmod(hd, GROUP_A)
                sbd[s, g, j * HEAD_A:(j + 1) * HEAD_A, j * HEAD_A:(j + 1) * HEAD_A] = s0_ref[s, hd]

    def shifted(ref, buf, mu_ref):
        rows = []
        for s in range(n_seq):
            x = ref[s]
            buf[s, 8:8 + C, :] = x
            rows.append(x + mu_ref[...] * (buf[s, 7:7 + C, :] - x))
            buf[s, 7:8, :] = x[C - 1:C, :]
        return jnp.concatenate(rows, axis=0)

    xs = shifted(x_ref, xbuf, mu_main_ref)
    lo_s = shifted(lora_ref, lbuf, mu_lora_ref)
    seq_rows = [slice(s * C, (s + 1) * C) for s in range(n_seq)]
    slabs = [slice(g * SLAB, (g + 1) * SLAB) for g in range(n_slabs)]

    r, k, v = xs[:, :W], xs[:, W:2 * W], xs[:, 2 * W:]
    w_pre = w0_ref[...] + _dot(jnp.tanh(lo_s).astype(BF16), wup_ref[...])
    ld = -math.exp(-0.5) * jax.nn.sigmoid(w_pre)
    a = jax.nn.sigmoid(a0_ref[...] + _dot(lo_s.astype(BF16), aup_ref[...]))

    e_mat = e_ref[...]

    def seg_sum(val):
        n = val.shape[0]
        hi, lo = _split2(val)
        out = _dot(jnp.concatenate([part[:, sl] for part in (hi, lo) for sl in slabs], axis=0), e_mat)
        return jnp.concatenate([out[g * n:(g + 1) * n] + out[(n_slabs + g) * n:(n_slabs + g + 1) * n]
                                for g in range(n_slabs)], axis=1)

    kk = k * kk_ref[...]
    kk = kk * lax.rsqrt(jnp.maximum(seg_sum(kk * kk), 1e-24))
    kmod = k * (1.0 + (a - 1.0) * ka_ref[...])

    ld_hi = ld.astype(BF16)
    ld_r1 = ld - ld_hi.astype(F32)
    ld_mid = ld_r1.astype(BF16)
    ld_lo = (ld_r1 - ld_mid.astype(F32)).astype(BF16)
    ltri = ltri_ref[...]
    lp = _dot(ltri, ld_hi) + _dot(ltri, ld_mid) + _dot(ltri, ld_lo)
    lp_last = [lp[rs.stop - 1:rs.stop, :] for rs in seq_rows]
    lp_end = jnp.concatenate([jnp.broadcast_to(row, (C, W)) for row in lp_last], axis=0)
    e_neg = jnp.exp(-lp)
    kka = kk * a
    kap = kk * jnp.exp(lp - ld)
    bet = kka * e_neg
    kt = kmod * e_neg
    rt = r * jnp.exp(lp)
    e_end = jnp.exp(lp_end - lp)
    kt_end = kmod * e_end
    bet_end = kka * e_end
    dec_end = [jnp.exp(row) for row in lp_last]

    lane = lax.broadcasted_iota(jnp.int32, (1, SLAB), 1)
    lane_masks = [(lane >= j * HEAD_A) & (lane < (j + 1) * HEAD_A) for j in range(GROUP_A)]
    t_row = lax.broadcasted_iota(jnp.int32, (C, GROUP_A * C), 0)
    s_col = lax.broadcasted_iota(jnp.int32, (C, GROUP_A * C), 1) & (C - 1)
    strict, incl = s_col < t_row, s_col <= t_row
    stack = lambda val16: _block_stack(val16, lane_masks)
    nh = GROUP_A * C
    units = [(s, g) for s in range(n_seq) for g in range(n_slabs)]
    per_slab = lambda fn: [fn(u, seq_rows[s], slabs[g]) for u, (s, g) in enumerate(units)]

    kap16, rt16, bet16, kt16, v16 = (val.astype(BF16) for val in (kap, rt, bet, kt, v))
    s_old = [sbd[s, g] for s, g in units]
    lhs = per_slab(lambda g, rs, sl: jnp.concatenate([kap16[rs, sl], rt16[rs, sl]], axis=0))
    rhs = per_slab(lambda g, rs, sl: jnp.concatenate([stack(bet16[rs, sl]), stack(kt16[rs, sl])], axis=0))
    sc = per_slab(lambda g, rs, sl: _dot_nt(lhs[g], rhs[g]))
    ls = per_slab(lambda g, rs, sl: _dot_nt(lhs[g], s_old[g].astype(BF16)))
    v_stack = per_slab(lambda g, rs, sl: stack(v16[rs, sl]))
    p16 = per_slab(lambda g, rs, sl: jnp.where(strict, -sc[g][:C, :nh], 0.0).astype(BF16))
    xw = per_slab(lambda g, rs, sl: ls[g][:C]
                  + _dot(jnp.where(strict, sc[g][:C, nh:], 0.0).astype(BF16), v_stack[g]))
    for step in range(6):
        xw = per_slab(lambda g, rs, sl: xw[g] + _dot(p16[g], stack(xw[g].astype(BF16))))
        if step < 5:
            p16 = per_slab(lambda g, rs, sl: _dot(p16[g], stack(p16[g])).astype(BF16))
    ab_inc = per_slab(lambda g, rs, sl: jnp.concatenate(
        [jnp.where(incl, sc[g][C:, :nh], 0.0), jnp.where(incl, sc[g][C:, nh:], 0.0)], axis=1).astype(BF16))
    y = per_slab(lambda g, rs, sl: ls[g][C:] + _dot(
        ab_inc[g], jnp.concatenate([stack((-xw[g]).astype(BF16)), v_stack[g]], axis=0)))
    upd = per_slab(lambda g, rs, sl: _dot(
        jnp.concatenate([v[rs, sl], -xw[g]], axis=0).T.astype(BF16),
        jnp.concatenate([kt_end[rs, sl], bet_end[rs, sl]], axis=0).astype(BF16)))
    for u, (s, g) in enumerate(units):
        sbd[s, g] = s_old[u] * dec_end[s][:, slabs[g]] + upd[u] * bd_ref[...]

    y = jnp.concatenate([jnp.concatenate(y[s * n_slabs:(s + 1) * n_slabs], axis=1) for s in range(n_seq)], axis=0)
    inv_n = 1.0 / HEAD_A
    mean = seg_sum(y) * inv_n
    yc = y - mean
    var = seg_sum(yc * yc) * inv_n
    y = yc * lax.rsqrt(var + GN_EPS) * gnw_ref[...] + gnb_ref[...]
    y = y + seg_sum(r * kmod * rk_ref[...]) * v
    for s, rs in enumerate(seq_rows):
        o_ref[s] = (y[rs] * jax.nn.silu(z_ref[s])).astype(o_ref.dtype)

    @pl.when(c == n_chunks - 1)
    def _fin():
        for s in range(n_seq):
            shm_out_ref[s] = xbuf[s, 7:8, :]
            shl_out_ref[s] = lbuf[s, 7:8, :]
            for hd in range(n_heads):
                g, j = divmod(hd, GROUP_A)
                sout_ref[s, hd] = sbd[s, g, j * HEAD_A:(j + 1) * HEAD_A, j * HEAD_A:(j + 1) * HEAD_A]


RWKV_SEQS_PER_STEP = 2


def rwkv7(proj, lora, z_col, shift_main, shift_lora, s0, prm, nseq, t):
    W = prm["a_w0"].shape[0]
    C = RWKV_CHUNK
    S = RWKV_SEQS_PER_STEP
    nc = t // C
    n_heads = W // HEAD_A
    lw = lora.shape[1]
    half = lw // 2
    zpad = jnp.zeros((half, W), F32)
    wup = jnp.concatenate([prm["a_w_up"], zpad], axis=0).astype(BF16)
    aup = jnp.concatenate([zpad, prm["a_a_up"]], axis=0).astype(BF16)
    idx = jnp.arange(SLAB) // HEAD_A
    same = idx[:, None] == idx[None, :]
    ti = jnp.arange(S * C)
    ltri = ((ti[:, None] >= ti[None, :]) & (ti[:, None] // C == ti[None, :] // C)).astype(BF16)
    row1 = lambda a: a.reshape(1, -1)
    vec = lambda n: pl.BlockSpec((1, n), lambda b, c: (0, 0))
    full = lambda a: pl.BlockSpec(a.shape, lambda b, c: (0, 0))
    rows = lambda n, col: pl.BlockSpec((S, C, n), lambda b, c: (b, c, col))
    per_seq = lambda n: pl.BlockSpec((S, 1, n), lambda b, c: (b, 0, 0))
    state = pl.BlockSpec((S, n_heads, HEAD_A, HEAD_A), lambda b, c: (b, 0, 0, 0))
    e_mat, bd_mask = same.astype(BF16), same.astype(F32)
    proj3 = proj.reshape(nseq, t, proj.shape[1])
    o_a, s_new, sh_main, sh_lora = pl.pallas_call(
        functools.partial(_rwkv_kernel, chunk=C, width=W),
        grid=(nseq // S, nc),
        in_specs=[rows(3 * W, 0), rows(lw, 0), rows(W, z_col),
                  per_seq(3 * W), per_seq(lw), state,
                  vec(3 * W), vec(lw), vec(W), full(wup), vec(W), full(aup), vec(W), vec(W), vec(W), vec(W), vec(W),
                  full(e_mat), full(bd_mask), full(ltri)],
        out_specs=[rows(W, 0), state, per_seq(3 * W), per_seq(lw)],
        out_shape=[jax.ShapeDtypeStruct((nseq, t, W), BF16),
                   jax.ShapeDtypeStruct((nseq, n_heads, HEAD_A, HEAD_A), F32),
                   jax.ShapeDtypeStruct((nseq, 1, 3 * W), F32),
                   jax.ShapeDtypeStruct((nseq, 1, lw), F32)],
        scratch_shapes=[pltpu.VMEM((S, C + 8, 3 * W), F32), pltpu.VMEM((S, C + 8, lw), F32),
                        pltpu.VMEM((S, W // SLAB, SLAB, SLAB), F32)],
        compiler_params=_params("parallel", "arbitrary"),
        name="rwkv7",
    )(proj3, lora.reshape(nseq, t, lw), proj3, shift_main, shift_lora, s0,
      row1(prm["a_mu"][:3 * W]), row1(prm["a_mu"][3 * W:]), row1(prm["a_w0"]), wup, row1(prm["a_a0"]), aup,
      row1(prm["a_k_k"]), row1(prm["a_k_a"]), row1(prm["a_r_k"]), row1(prm["a_gn_w"]), row1(prm["a_gn_b"]),
      e_mat, bd_mask, ltri)
    return o_a.reshape(nseq * t, W), s_new, sh_main, sh_lora


COL_AZ, COL_BQ, COL_BK, COL_BV, COL_BZ, COL_CU, COL_MQ = 3, 4, 5, 6, 7, 8, 11


def trunk_layer(x, h, nseq, t, mk, mv, shift_prev, s_prev, past_k, past_v, prm, layer, depth, kv_out, want_vn):
    m, d = x.shape
    W = d // 4
    tm = min(512, m)
    if h is None:
        h = rmsnorm_rows(x, prm["g_pre"], min(256, m))
    proj = matmul(h, prm["w_in_main"], layer, min(1024, m), 1024)
    lora = matmul(h, prm["w_in_lora"], layer, tm, prm["w_in_lora"].shape[2])

    o_a, s_new, sh_main, sh_lora = rwkv7(proj, lora, COL_AZ, shift_prev[..., :3 * W], shift_prev[..., 3 * W:],
                                         s_prev, prm, nseq, t)
    shift_new = jnp.concatenate([sh_main, sh_lora], axis=-1)

    heads_b = W // HEAD_B
    hb = lambda col: col * heads_b
    o_b, kv_out = stick_breaking(proj, hb(COL_BQ), hb(COL_BZ), hb(COL_BK), hb(COL_BV), past_k, past_v,
                                 nseq, t, heads_b, layer, depth, kv_out)

    clen = min(t, prm["c_ws"].shape[1])
    groups = prm["c_ws"].shape[0]
    bs_full = jnp.repeat(prm["c_bs"][:, :clen].T, W // groups, axis=1)
    o_c, *vn_c = chunk_mlp(proj, COL_CU, prm["c_ws"][:, :clen, :clen], bs_full, prm["c_ln_w"], prm["c_ln_b"],
                           m, clen, max(1, min(512, t) // clen), want_vn)

    o_m = memory_attention(proj, COL_MQ, mk, mv, nseq, t, min(512, t))

    merged = gated_merge(h, (o_a, o_b, o_c, o_m), prm["w_gate"], prm["b_gate"], prm["w_br"], layer, tm,
                         512 if m > tm else 256)
    x_new, h_next = out_proj_norm_residual(merged, prm["w_out"], layer, x, prm["g_post"], prm["g_pre_next"], tm, 512)
    return x_new, h_next, shift_new, s_new, kv_out, vn_c


def kernel(x_prompt, x_sample, cache_mem_k, cache_mem_v, cache_sb_k, cache_sb_v, state_rwkv, state_shift, mem_prompt, g_pre, g_post, w_in, a_mu, a_w0, a_w_up, a_a0, a_a_up, a_k_k, a_k_a, a_r_k, a_gn_w, a_gn_b, c_ws, c_bs, c_ln_w, c_ln_b, g_mem, w_mem_kv, w_gate, b_gate, w_br, w_out):
    bp, tp, d = x_prompt.shape
    bs_, ts, _ = x_sample.shape
    depth = w_in.shape[0]
    W = d // 4
    n_mem = mem_prompt.shape[1]
    m_heads = cache_mem_k.shape[3]
    b_heads = cache_sb_k.shape[3]
    a_heads = state_rwkv.shape[2]
    shift_w = state_shift.shape[-1]
    n_past = cache_sb_k.shape[2]
    lora_lo, lora_hi = 3 * W, shift_w

    yp = x_prompt.reshape(bp * tp, d)
    ys = x_sample.reshape(bs_ * ts, d)
    mem_rows = mem_prompt.reshape(bp * n_mem, d)
    mem_out, rw_p, sh_p, rw_s, sh_s, cv_s = [[], []], [], [], [], [], []
    kv_p = kv_s = hp = hs = None
    dense = {
        "w_in_main": drop_cols_cast(w_in, lora_lo, lora_hi, W, 1024),
        "w_in_lora": w_in[:, :, lora_lo:lora_hi].astype(BF16),
        "w_gate": w_gate.astype(BF16), "w_br": w_br.astype(BF16), "w_out": w_out.astype(BF16),
    }
    w_mem16 = w_mem_kv.astype(BF16)
    for l in range(depth):
        prm = dict(dense)
        prm.update({
            "g_pre": g_pre[l], "g_post": g_post[l], "g_pre_next": g_pre[l + 1] if l + 1 < depth else None,
            "a_mu": a_mu[l], "a_w0": a_w0[l], "a_w_up": a_w_up[l], "a_a0": a_a0[l], "a_a_up": a_a_up[l],
            "a_k_k": a_k_k[l].reshape(-1), "a_k_a": a_k_a[l].reshape(-1), "a_r_k": a_r_k[l].reshape(-1),
            "a_gn_w": a_gn_w[l].reshape(-1), "a_gn_b": a_gn_b[l].reshape(-1),
            "c_ws": c_ws[l], "c_bs": c_bs[l], "c_ln_w": c_ln_w[l], "c_ln_b": c_ln_b[l], "b_gate": b_gate[l],
        })
        kv = matmul(rmsnorm_rows(mem_rows, g_mem[l], 256), w_mem16, l, min(1024, bp * n_mem), 1024)
        mk = kv[:, :W].reshape(bp, n_mem, W)
        mv = kv[:, W:].reshape(bp, n_mem, W)
        shift0 = jnp.zeros((bp, 1, shift_w), F32)
        s0 = jnp.zeros((bp, a_heads, HEAD_A, HEAD_A), F32)
        yp, hp, sh, st, kv_p, _ = trunk_layer(yp, hp, bp, tp, mk.astype(BF16), mv.astype(BF16), shift0, s0, None, None,
                                              prm, l, depth, kv_p, False)
        mem_out[0].append(mk.reshape(bp, n_mem, m_heads, W // m_heads))
        mem_out[1].append(mv.reshape(bp, n_mem, m_heads, W // m_heads))
        rw_p.append(st)
        sh_p.append(sh)
        ys, hs, sh, st, kv_s, cvn = trunk_layer(
            ys, hs, bs_, ts, cache_mem_k[l].reshape(bs_, n_mem, W).astype(BF16),
            cache_mem_v[l].reshape(bs_, n_mem, W).astype(BF16), state_shift[l], state_rwkv[l],
            cache_sb_k, cache_sb_v, prm, l, depth, kv_s, True)
        rw_s.append(st)
        sh_s.append(sh)
        cv_s.append(cvn[0].reshape(bs_, ts, W))
    heads_p = lambda a: a.reshape(depth, bp, tp, b_heads, HEAD_B)
    heads_s = lambda a: a.reshape(depth, bs_, ts, b_heads, HEAD_B)
    return (yp.reshape(bp, tp, d), ys.reshape(bs_, ts, d), jnp.stack(mem_out[0]), jnp.stack(mem_out[1]),
            heads_p(kv_p[0]), heads_p(kv_p[1]), jnp.stack(rw_p), jnp.stack(sh_p),
            heads_s(kv_s[0]), heads_s(kv_s[1]), jnp.stack(rw_s), jnp.stack(sh_s), jnp.stack(cv_s))
```

```python
import functools
import math

import jax
import jax.numpy as jnp
from jax import lax
from jax.experimental import pallas as pl
from jax.experimental.pallas import tpu as pltpu

F32 = jnp.float32
BF16 = jnp.bfloat16

NORM_EPS = 1e-6
GN_EPS = 64e-5
HEAD_A = 64
GROUP_A = 4
SLAB = HEAD_A * GROUP_A
HEAD_B = 128
HEAD_M = 256
GROUP_C = 128
RWKV_CHUNK = 64
VMEM_LIMIT_BYTES = 56 * 1024 * 1024


def _params(*sem):
    return pltpu.CompilerParams(dimension_semantics=sem, vmem_limit_bytes=VMEM_LIMIT_BYTES)


def _split2(x):
    hi = x.astype(BF16)
    lo = (x - hi.astype(F32)).astype(BF16)
    return hi, lo


def _dot(a, b):
    return jnp.dot(a, b, preferred_element_type=F32)


def _dot_nt(a, b):
    return lax.dot_general(a, b, (((1,), (1,)), ((), ())), preferred_element_type=F32)


def _dot_row_halves(dot, a, b):
    half = a.shape[0] // 2
    return jnp.concatenate([dot(a[:half], b), dot(a[half:], b)], axis=0)


def _rmsnorm_kernel(x_ref, g_ref, o_ref):
    x = x_ref[...]
    ms = jnp.mean(x * x, axis=-1, keepdims=True)
    o_ref[...] = (x * lax.rsqrt(ms + NORM_EPS) * g_ref[...]).astype(o_ref.dtype)


def rmsnorm_rows(x, g, tm):
    m, d = x.shape
    return pl.pallas_call(
        _rmsnorm_kernel,
        grid=(m // tm,),
        in_specs=[pl.BlockSpec((tm, d), lambda i: (i, 0)), pl.BlockSpec((1, d), lambda i: (0, 0))],
        out_specs=pl.BlockSpec((tm, d), lambda i: (i, 0)),
        out_shape=jax.ShapeDtypeStruct((m, d), BF16),
        compiler_params=_params("parallel"),
        name="rmsnorm_rows",
    )(x, g.reshape(1, d))


POST_ROWS = 64


def _out_proj_kernel(*refs, tn, with_next):
    if with_next:
        m_ref, w_ref, x_ref, g_ref, gn_ref, o_ref, hn_ref, x_rows, ssq = refs
    else:
        m_ref, w_ref, x_ref, g_ref, o_ref, x_rows, ssq = refs
    j = pl.program_id(1)
    cols = pl.ds(pl.multiple_of(j * tn, tn), tn)
    y = _dot(m_ref[...], w_ref[...])
    sq = jnp.sum(y * y, axis=-1, keepdims=True)
    o_ref[:, cols] = y * g_ref[...]
    x_rows[:, cols] = x_ref[...]

    @pl.when(j == 0)
    def _first():
        ssq[...] = sq

    @pl.when(j > 0)
    def _rest():
        ssq[...] += sq

    @pl.when(j == pl.num_programs(1) - 1)
    def _normalise():
        inv_d = 1.0 / o_ref.shape[1]
        for r0 in range(0, o_ref.shape[0], POST_ROWS):
            rows = slice(r0, r0 + POST_ROWS)
            scale = lax.rsqrt(ssq[rows, :] * inv_d + NORM_EPS)
            x_new = x_rows[rows, :] + o_ref[rows, :] * scale
            o_ref[rows, :] = x_new
            if with_next:
                ms = jnp.mean(x_new * x_new, axis=-1, keepdims=True)
                hn_ref[rows, :] = (x_new * lax.rsqrt(ms + NORM_EPS) * gn_ref[...]).astype(hn_ref.dtype)


def out_proj_norm_residual(merged, w, layer, x, g, g_next, tm, tn):
    m, d = x.shape
    with_next = g_next is not None
    row = pl.BlockSpec((tm, d), lambda i, j: (i, 0))
    gain = pl.BlockSpec((1, d), lambda i, j: (0, 0))
    out = pl.pallas_call(
        functools.partial(_out_proj_kernel, tn=tn, with_next=with_next),
        grid=(m // tm, d // tn),
        in_specs=[row, pl.BlockSpec((None, d, tn), lambda i, j: (layer, 0, j)),
                  pl.BlockSpec((tm, tn), lambda i, j: (i, j)), pl.BlockSpec((1, tn), lambda i, j: (0, j))]
                 + ([gain] if with_next else []),
        out_specs=[row, row] if with_next else [row],
        out_shape=[jax.ShapeDtypeStruct((m, d), F32)] + ([jax.ShapeDtypeStruct((m, d), BF16)] if with_next else []),
        scratch_shapes=[pltpu.VMEM((tm, d), F32), pltpu.VMEM((tm, 1), F32)],
        compiler_params=_params("parallel", "arbitrary"),
        name="out_proj_norm_residual",
    )(merged, w, x, g.reshape(1, d), *([g_next.reshape(1, d)] if with_next else []))
    return out if with_next else (out[0], None)


def _drop_cols_kernel(a_ref, b_ref, o_ref, *, first_shifted, gap):
    j = pl.program_id(2)

    @pl.when(j < first_shifted)
    def _before_gap():
        o_ref[...] = a_ref[...].astype(o_ref.dtype)

    @pl.when(j >= first_shifted)
    def _after_gap():
        o_ref[...] = jnp.concatenate([a_ref[:, gap:], b_ref[...]], axis=1).astype(o_ref.dtype)


def drop_cols_cast(w, lo, hi, cw, tk):
    depth, k, n = w.shape
    gap = hi - lo
    n_out = n - gap
    return pl.pallas_call(
        functools.partial(_drop_cols_kernel, first_shifted=lo // cw, gap=gap),
        grid=(depth, k // tk, n_out // cw),
        in_specs=[pl.BlockSpec((None, tk, cw), lambda l, i, j: (l, i, j)),
                  pl.BlockSpec((None, tk, gap), lambda l, i, j: (l, i, (j + 1) * (cw // gap)))],
        out_specs=pl.BlockSpec((None, tk, cw), lambda l, i, j: (l, i, j)),
        out_shape=jax.ShapeDtypeStruct((depth, k, n_out), BF16),
        compiler_params=_params("parallel", "parallel", "parallel"),
        name="drop_cols_cast",
    )(w, w)
def _matmul_kernel(x_ref, w_ref, o_ref):
    o_ref[...] = _dot(x_ref[...], w_ref[...]).astype(o_ref.dtype)


def matmul(x, w, layer, tm, tn, out_dtype=F32):
    m, k = x.shape
    n = w.shape[2]
    return pl.pallas_call(
        _matmul_kernel,
        grid=(n // tn, m // tm),
        in_specs=[pl.BlockSpec((tm, k), lambda j, i: (i, 0)),
                  pl.BlockSpec((None, k, tn), lambda j, i: (layer, 0, j))],
        out_specs=pl.BlockSpec((tm, tn), lambda j, i: (i, j)),
        out_shape=jax.ShapeDtypeStruct((m, n), out_dtype),
        compiler_params=_params("parallel", "parallel"),
        name="matmul",
    )(x, w)


def _merge_kernel(h_ref, oa_ref, ob_ref, oc_ref, om_ref, wg_ref, bg_ref, wbr_ref, out_ref):
    h = h_ref[...]
    acc = None
    for n, o_ref in enumerate((oa_ref, ob_ref, oc_ref, om_ref)):
        gate = jax.nn.sigmoid(_dot(h, wg_ref[n]) + bg_ref[n])
        term = gate * _dot(o_ref[...], wbr_ref[n])
        acc = term if acc is None else acc + term
    out_ref[...] = acc.astype(out_ref.dtype)


def gated_merge(h, branches, wg, bg, wbr, layer, tm, tn):
    m, d = h.shape
    _, nb, w, _ = wbr.shape
    resident = dict(pipeline_mode=pl.Buffered(1)) if m > tm else {}
    o_spec = pl.BlockSpec((tm, w), lambda j, i: (i, 0))
    return pl.pallas_call(
        _merge_kernel,
        grid=(d // tn, m // tm),
        in_specs=[pl.BlockSpec((tm, d), lambda j, i: (i, 0)), o_spec, o_spec, o_spec, o_spec,
                  pl.BlockSpec((None, nb, d, tn), lambda j, i: (layer, 0, 0, j), **resident),
                  pl.BlockSpec((nb, 1, tn), lambda j, i: (0, 0, j)),
                  pl.BlockSpec((None, nb, w, tn), lambda j, i: (layer, 0, 0, j), **resident)],
        out_specs=pl.BlockSpec((tm, tn), lambda j, i: (i, j)),
        out_shape=jax.ShapeDtypeStruct((m, d), BF16),
        compiler_params=_params("parallel", "parallel"),
        name="gated_merge",
    )(h, *branches, wg, bg.reshape(nb, 1, d), wbr)


def _memattn_kernel(q_ref, mk_ref, mv_ref, o_ref, *, heads):
    scale = 1.0 / math.sqrt(HEAD_M)
    for hd in range(heads):
        sl = slice(hd * HEAD_M, (hd + 1) * HEAD_M)
        q = q_ref[:, sl].astype(BF16)
        s = _dot_nt(q, mk_ref[0, :, sl]) * scale
        s = s - jnp.max(s, axis=-1, keepdims=True)
        p = jnp.exp(s)
        p = p * (1.0 / jnp.sum(p, axis=-1, keepdims=True))
        o_ref[:, sl] = _dot(p.astype(BF16), mv_ref[0, :, sl]).astype(o_ref.dtype)


def memory_attention(proj, q_col, mk, mv, nseq, t, tq):
    n_mem, w = mk.shape[1:]
    per_seq = t // tq
    mem_spec = pl.BlockSpec((1, n_mem, w), lambda i: (i // per_seq, 0, 0))
    return pl.pallas_call(
        functools.partial(_memattn_kernel, heads=w // HEAD_M),
        grid=(nseq * per_seq,),
        in_specs=[pl.BlockSpec((tq, w), lambda i: (i, q_col)), mem_spec, mem_spec],
        out_specs=pl.BlockSpec((tq, w), lambda i: (i, 0)),
        out_shape=jax.ShapeDtypeStruct((nseq * t, w), BF16),
        compiler_params=_params("parallel"),
        name="memory_attention",
    )(proj, mk, mv)


def _cmlp_kernel(u_ref, v_ref, z_ref, ws_ref, bs_ref, lnw_ref, lnb_ref, o_ref, *maybe_vn_ref, clen, chunks):
    v = v_ref[...]
    mu = jnp.mean(v, axis=-1, keepdims=True)
    var = jnp.mean(jnp.square(v - mu), axis=-1, keepdims=True)
    vn = (v - mu) * lax.rsqrt(var + NORM_EPS) * lnw_ref[...] + lnb_ref[...]
    for vn_ref in maybe_vn_ref:
        vn_ref[...] = vn
    vn16 = vn.astype(BF16)
    groups = vn.shape[1] // GROUP_C
    row = lax.broadcasted_iota(jnp.int32, (clen, clen), 0)
    col = lax.broadcasted_iota(jnp.int32, (clen, clen), 1)
    for g in range(groups):
        wm = jnp.where(row >= col, ws_ref[g], 0.0).astype(BF16)
        cs = slice(g * GROUP_C, (g + 1) * GROUP_C)
        for c in range(chunks):
            rs = slice(c * clen, (c + 1) * clen)
            s = _dot(wm, vn16[rs, cs]) + bs_ref[:, cs]
            o_ref[rs, cs] = (u_ref[rs, cs] * s * jax.nn.silu(z_ref[rs, cs])).astype(o_ref.dtype)


def chunk_mlp(proj, u_col, ws, bs_full, ln_w, ln_b, rows, clen, chunks, want_vn):
    w = ln_w.shape[0]
    tm = clen * chunks
    col = lambda c: pl.BlockSpec((tm, w), lambda i: (i, c))
    const2 = lambda a: pl.BlockSpec(a.shape, lambda i: (0, 0))
    out = pl.BlockSpec((tm, w), lambda i: (i, 0))
    out_dtypes = [BF16, F32] if want_vn else [BF16]
    return pl.pallas_call(
        functools.partial(_cmlp_kernel, clen=clen, chunks=chunks),
        grid=(rows // tm,),
        in_specs=[col(u_col), col(u_col + 1), col(u_col + 2),
                  pl.BlockSpec(ws.shape, lambda i: (0, 0, 0)), const2(bs_full),
                  pl.BlockSpec((1, w), lambda i: (0, 0)), pl.BlockSpec((1, w), lambda i: (0, 0))],
        out_specs=[out] * len(out_dtypes),
        out_shape=[jax.ShapeDtypeStruct((rows, w), dt) for dt in out_dtypes],
        compiler_params=_params("parallel"),
        name="chunk_mlp",
    )(proj, proj, proj, ws, bs_full, ln_w.reshape(1, w), ln_b.reshape(1, w))


SB_HEADS_PER_STEP = 2
SB_BLOCK = 256
SB_GROUP = 1


def _sb_group(q16, k16, v16, tk, u, masks, carry):
    tq = q16.shape[0]
    nb = k16.shape[0] // tk
    z_all = _dot_nt(q16, k16)
    nl, log_beta = [], []
    for j in range(nb):
        z2 = z_all[:, j * tk:(j + 1) * tk] * (math.log2(math.e) / math.sqrt(HEAD_B))
        neg_abs = lax.bitcast_convert_type(lax.bitcast_convert_type(z2, jnp.uint32) | jnp.uint32(0x80000000), F32)
        nl_j = jnp.maximum(z2, 0.0) + jnp.log2(1.0 + jnp.exp2(neg_abs))
        log_beta.append(z2 - nl_j)
        nl.append(jnp.where(masks[j], nl_j, 0.0) if j in masks else nl_j)
    lhs = jnp.concatenate([jnp.concatenate(_split2(nl_j), axis=1) for nl_j in nl], axis=0)
    cs = _dot_row_halves(_dot, lhs, u)
    att = [None] * nb
    for j in reversed(range(nb)):
        cs_j = cs[j * tq:(j + 1) * tq]
        att_j = jnp.exp2(log_beta[j] - cs_j - carry)
        att[j] = (jnp.where(masks[j], att_j, 0.0) if j in masks else att_j).astype(BF16)
        carry = carry + (cs_j[:, 0:1] + nl[j][:, 0:1])
    return _dot_row_halves(_dot, jnp.concatenate(att, axis=1), v16), carry


def _head_slices(ref):
    return [slice(i * HEAD_B, (i + 1) * HEAD_B) for i in range(ref.shape[1] // HEAD_B)]


def _sb_prompt_kernel(*refs, tq, nq):
    q_ref, z_ref, k_ref, v_ref, u_ref = refs[:5]
    o_ref, kout_ref, vout_ref = refs[-3:]
    qi = pl.program_id(2)

    @pl.when(qi == 0)
    def _emit_kv():
        kout_ref[...] = k_ref[...]
        vout_ref[...] = v_ref[...]

    row = lax.broadcasted_iota(jnp.int32, (tq, tq), 0)
    col = lax.broadcasted_iota(jnp.int32, (tq, tq), 1)
    q_pos = qi * tq + row
    for grp in range(1, pl.cdiv(nq, SB_GROUP) + 1):
        nb = min(grp * SB_GROUP, nq)

        @pl.when(qi // SB_GROUP + 1 == grp)
        def _sweep():
            masks = {j: (j * tq + col) < q_pos for j in range((grp - 1) * SB_GROUP, nb)}
            for s in _head_slices(q_ref):
                out, _ = _sb_group(q_ref[:, s].astype(BF16), k_ref[0:nb * tq, s].astype(BF16),
                                   v_ref[0:nb * tq, s].astype(BF16), tq, u_ref[...], masks, jnp.zeros((tq, 1), F32))
                o_ref[:, s] = (out * jax.nn.silu(z_ref[:, s])).astype(o_ref.dtype)


def _sb_sample_kernel(*refs, tk):
    q_ref, z_ref, kn_ref, vn_ref, kp_ref, vp_ref, ud_ref, uf_ref = refs[:8]
    o_ref, kout_ref, vout_ref = refs[-3:]
    kout_ref[...] = kn_ref[...]
    vout_ref[...] = vn_ref[...]
    tq = q_ref.shape[0]
    row = lax.broadcasted_iota(jnp.int32, (tq, tq), 0)
    col = lax.broadcasted_iota(jnp.int32, (tq, tq), 1)
    heads = _head_slices(q_ref)
    n_past = kp_ref.shape[0] // len(heads)
    for hd, s in enumerate(heads):
        q16 = q_ref[:, s].astype(BF16)
        out_new, carry = _sb_group(q16, kn_ref[:, s].astype(BF16), vn_ref[:, s].astype(BF16), tq, ud_ref[...],
                                   {0: col < row}, jnp.zeros((tq, 1), F32))
        own_rows = pl.ds(hd, n_past, stride=len(heads))
        out_past, _ = _sb_group(q16, kp_ref[own_rows, :].astype(BF16), vp_ref[own_rows, :].astype(BF16), tk,
                                uf_ref[...], {}, carry)
        o_ref[:, s] = ((out_new + out_past) * jax.nn.silu(z_ref[:, s])).astype(o_ref.dtype)


def _suffix_matrix(tk):
    s_later = lax.broadcasted_iota(jnp.int32, (tk, tk), 0)
    s_here = lax.broadcasted_iota(jnp.int32, (tk, tk), 1)
    u = (s_later > s_here).astype(BF16)
    return jnp.concatenate([u, u], axis=0)


def stick_breaking(proj, q_col, z_col, k_col, v_col, past_k, past_v, nseq, t, heads, layer, depth, kv_out):
    hp = SB_HEADS_PER_STEP if past_k is None else heads
    wide = hp * HEAD_B
    tq = min(SB_BLOCK, t)
    nq = t // tq
    blk = lambda col: pl.BlockSpec((tq, wide), lambda b, h, i: (b * nq + i, col // hp + h))
    seq = lambda rows, col: pl.BlockSpec((rows, wide), lambda b, h, i: (b, col // hp + h))
    const = lambda a: pl.BlockSpec(a.shape, lambda b, h, i: (0, 0))
    if past_k is None:
        u = _suffix_matrix(tq)
        body = functools.partial(_sb_prompt_kernel, tq=tq, nq=nq)
        in_specs = [blk(q_col), blk(z_col), seq(t, k_col), seq(t, v_col), const(u)]
        args = [proj, proj, proj, proj, u]
    else:
        assert nq == 1
        n_past = past_k.shape[2]
        past = pl.BlockSpec((None, None, n_past * heads, HEAD_B), lambda b, h, i: (layer, b, 0, 0))
        past_k, past_v = (a.reshape(depth, nseq, n_past * heads, HEAD_B) for a in (past_k, past_v))
        ud, uf = _suffix_matrix(tq), _suffix_matrix(SB_BLOCK)
        body = functools.partial(_sb_sample_kernel, tk=SB_BLOCK)
        in_specs = [blk(q_col), blk(z_col), seq(t, k_col), seq(t, v_col), past, past, const(ud), const(uf)]
        args = [proj, proj, proj, proj, past_k, past_v, ud, uf]
    aliases = {}
    if kv_out is not None:
        aliases = {len(args): 1, len(args) + 1: 2}
        in_specs = in_specs + [pl.BlockSpec(memory_space=pl.ANY)] * 2
        args = args + list(kv_out)
    kv_shape = jax.ShapeDtypeStruct((depth, nseq * t, heads * HEAD_B), F32)
    kv_spec = pl.BlockSpec((None, t, wide), lambda b, h, i: (layer, b, h))
    o_b, k_all, v_all = pl.pallas_call(
        body,
        grid=(nseq, heads // hp, nq),
        in_specs=in_specs,
        out_specs=[pl.BlockSpec((tq, wide), lambda b, h, i: (b * nq + i, h)), kv_spec, kv_spec],
        out_shape=[jax.ShapeDtypeStruct((nseq * t, heads * HEAD_B), BF16), kv_shape, kv_shape],
        input_output_aliases=aliases,
        compiler_params=_params("parallel", "parallel", "arbitrary"),
        name="stick_breaking",
    )(*args)
    return o_b, (k_all, v_all)


def _block_stack(x, lane_masks):
    return jnp.concatenate([jnp.where(m, x, 0.0) for m in lane_masks], axis=0)


def _rwkv_kernel(x_ref, lora_ref, z_ref, sh_main_ref, sh_lora_ref, s0_ref,
                 mu_main_ref, mu_lora_ref, w0_ref, wup_ref, a0_ref, aup_ref, kk_ref, ka_ref, rk_ref,
                 gnw_ref, gnb_ref, e_ref, bd_ref, ltri_ref,
                 o_ref, sout_ref, shm_out_ref, shl_out_ref,
                 xbuf, lbuf, sbd, *, chunk, width):
    c = pl.program_id(1)
    n_chunks = pl.num_programs(1)
    C, W = chunk, width
    n_seq = x_ref.shape[0]
    n_slabs = W // SLAB
    n_heads = W // HEAD_A

    @pl.when(c == 0)
    def _init():
        sbd[...] = jnp.zeros_like(sbd)
        for s in range(n_seq):
            xbuf[s, 7:8, :] = sh_main_ref[s]
            lbuf[s, 7:8, :] = sh_lora_ref[s]
            for hd in range(n_heads):
                g, j = divmod(hd, GROUP_A)
                sbd[s, g, j * HEAD_A:(j + 1) * HEAD_A, j * HEAD_A:(j + 1) * HEAD_A] = s0_ref[s, hd]

    def shifted(ref, buf, mu_ref):
        rows = []
        for s in range(n_seq):
            x = ref[s]
            buf[s, 8:8 + C, :] = x
            rows.append(x + mu_ref[...] * (buf[s, 7:7 + C, :] - x))
            buf[s, 7:8, :] = x[C - 1:C, :]
        return jnp.concatenate(rows, axis=0)

    xs = shifted(x_ref, xbuf, mu_main_ref)
    lo_s = shifted(lora_ref, lbuf, mu_lora_ref)
    seq_rows = [slice(s * C, (s + 1) * C) for s in range(n_seq)]
    slabs = [slice(g * SLAB, (g + 1) * SLAB) for g in range(n_slabs)]

    r, k, v = xs[:, :W], xs[:, W:2 * W], xs[:, 2 * W:]
    w_pre = w0_ref[...] + _dot(jnp.tanh(lo_s).astype(BF16), wup_ref[...])
    ld = -math.exp(-0.5) * jax.nn.sigmoid(w_pre)
    a = jax.nn.sigmoid(a0_ref[...] + _dot(lo_s.astype(BF16), aup_ref[...]))

    e_mat = e_ref[...]

    def seg_sum(val):
        n = val.shape[0]
        hi, lo = _split2(val)
        out = _dot(jnp.concatenate([part[:, sl] for part in (hi, lo) for sl in slabs], axis=0), e_mat)
        return jnp.concatenate([out[g * n:(g + 1) * n] + out[(n_slabs + g) * n:(n_slabs + g + 1) * n]
                                for g in range(n_slabs)], axis=1)

    kk = k * kk_ref[...]
    kk = kk * lax.rsqrt(jnp.maximum(seg_sum(kk * kk), 1e-24))
    kmod = k * (1.0 + (a - 1.0) * ka_ref[...])

    ld_hi = ld.astype(BF16)
    ld_r1 = ld - ld_hi.astype(F32)
    ld_mid = ld_r1.astype(BF16)
    ld_lo = (ld_r1 - ld_mid.astype(F32)).astype(BF16)
    ltri = ltri_ref[...]
    lp = _dot(ltri, ld_hi) + _dot(ltri, ld_mid) + _dot(ltri, ld_lo)
    lp_last = [lp[rs.stop - 1:rs.stop, :] for rs in seq_rows]
    lp_end = jnp.concatenate([jnp.broadcast_to(row, (C, W)) for row in lp_last], axis=0)
    e_neg = jnp.exp(-lp)
    kka = kk * a
    kap = kk * jnp.exp(lp - ld)
    bet = kka * e_neg
    kt = kmod * e_neg
    rt = r * jnp.exp(lp)
    e_end = jnp.exp(lp_end - lp)
    kt_end = kmod * e_end
    bet_end = kka * e_end
    dec_end = [jnp.exp(row) for row in lp_last]

    lane = lax.broadcasted_iota(jnp.int32, (1, SLAB), 1)
    lane_masks = [(lane >= j * HEAD_A) & (lane < (j + 1) * HEAD_A) for j in range(GROUP_A)]
    t_row = lax.broadcasted_iota(jnp.int32, (C, GROUP_A * C), 0)
    s_col = lax.broadcasted_iota(jnp.int32, (C, GROUP_A * C), 1) & (C - 1)
    strict, incl = s_col < t_row, s_col <= t_row
    stack = lambda val16: _block_stack(val16, lane_masks)
    nh = GROUP_A * C
    units = [(s, g) for s in range(n_seq) for g in range(n_slabs)]
    per_slab = lambda fn: [fn(u, seq_rows[s], slabs[g]) for u, (s, g) in enumerate(units)]

    kap16, rt16, bet16, kt16, v16 = (val.astype(BF16) for val in (kap, rt, bet, kt, v))
    s_old = [sbd[s, g] for s, g in units]
    lhs = per_slab(lambda g, rs, sl: jnp.concatenate([kap16[rs, sl], rt16[rs, sl]], axis=0))
    rhs = per_slab(lambda g, rs, sl: jnp.concatenate([stack(bet16[rs, sl]), stack(kt16[rs, sl])], axis=0))
    sc = per_slab(lambda g, rs, sl: _dot_nt(lhs[g], rhs[g]))
    ls = per_slab(lambda g, rs, sl: _dot_nt(lhs[g], s_old[g].astype(BF16)))
    v_stack = per_slab(lambda g, rs, sl: stack(v16[rs, sl]))
    p16 = per_slab(lambda g, rs, sl: jnp.where(strict, -sc[g][:C, :nh], 0.0).astype(BF16))
    xw = per_slab(lambda g, rs, sl: ls[g][:C]
                  + _dot(jnp.where(strict, sc[g][:C, nh:], 0.0).astype(BF16), v_stack[g]))
    for step in range(6):
        xw = per_slab(lambda g, rs, sl: xw[g] + _dot(p16[g], stack(xw[g].astype(BF16))))
        if step < 5:
            p16 = per_slab(lambda g, rs, sl: _dot(p16[g], stack(p16[g])).astype(BF16))
    ab_inc = per_slab(lambda g, rs, sl: jnp.concatenate(
        [jnp.where(incl, sc[g][C:, :nh], 0.0), jnp.where(incl, sc[g][C:, nh:], 0.0)], axis=1).astype(BF16))
    y = per_slab(lambda g, rs, sl: ls[g][C:] + _dot(
        ab_inc[g], jnp.concatenate([stack((-xw[g]).astype(BF16)), v_stack[g]], axis=0)))
    upd = per_slab(lambda g, rs, sl: _dot(
        jnp.concatenate([v[rs, sl], -xw[g]], axis=0).T.astype(BF16),
        jnp.concatenate([kt_end[rs, sl], bet_end[rs, sl]], axis=0).astype(BF16)))
    for u, (s, g) in enumerate(units):
        sbd[s, g] = s_old[u] * dec_end[s][:, slabs[g]] + upd[u] * bd_ref[...]

    y = jnp.concatenate([jnp.concatenate(y[s * n_slabs:(s + 1) * n_slabs], axis=1) for s in range(n_seq)], axis=0)
    inv_n = 1.0 / HEAD_A
    mean = seg_sum(y) * inv_n
    yc = y - mean
    var = seg_sum(yc * yc) * inv_n
    y = yc * lax.rsqrt(var + GN_EPS) * gnw_ref[...] + gnb_ref[...]
    y = y + seg_sum(r * kmod * rk_ref[...]) * v
    for s, rs in enumerate(seq_rows):
        o_ref[s] = (y[rs] * jax.nn.silu(z_ref[s])).astype(o_ref.dtype)

    @pl.when(c == n_chunks - 1)
    def _fin():
        for s in range(n_seq):
            shm_out_ref[s] = xbuf[s, 7:8, :]
            shl_out_ref[s] = lbuf[s, 7:8, :]
            for hd in range(n_heads):
                g, j = divmod(hd, GROUP_A)
                sout_ref[s, hd] = sbd[s, g, j * HEAD_A:(j + 1) * HEAD_A, j * HEAD_A:(j + 1) * HEAD_A]


RWKV_SEQS_PER_STEP = 2


def rwkv7(proj, lora, z_col, shift_main, shift_lora, s0, prm, nseq, t):
    W = prm["a_w0"].shape[0]
    C = RWKV_CHUNK
    S = RWKV_SEQS_PER_STEP
    nc = t // C
    n_heads = W // HEAD_A
    lw = lora.shape[1]
    half = lw // 2
    zpad = jnp.zeros((half, W), F32)
    wup = jnp.concatenate([prm["a_w_up"], zpad], axis=0).astype(BF16)
    aup = jnp.concatenate([zpad, prm["a_a_up"]], axis=0).astype(BF16)
    idx = jnp.arange(SLAB) // HEAD_A
    same = idx[:, None] == idx[None, :]
    ti = jnp.arange(S * C)
    ltri = ((ti[:, None] >= ti[None, :]) & (ti[:, None] // C == ti[None, :] // C)).astype(BF16)
    row1 = lambda a: a.reshape(1, -1)
    vec = lambda n: pl.BlockSpec((1, n), lambda b, c: (0, 0))
    full = lambda a: pl.BlockSpec(a.shape, lambda b, c: (0, 0))
    rows = lambda n, col: pl.BlockSpec((S, C, n), lambda b, c: (b, c, col))
    per_seq = lambda n: pl.BlockSpec((S, 1, n), lambda b, c: (b, 0, 0))
    state = pl.BlockSpec((S, n_heads, HEAD_A, HEAD_A), lambda b, c: (b, 0, 0, 0))
    e_mat, bd_mask = same.astype(BF16), same.astype(F32)
    proj3 = proj.reshape(nseq, t, proj.shape[1])
    o_a, s_new, sh_main, sh_lora = pl.pallas_call(
        functools.partial(_rwkv_kernel, chunk=C, width=W),
        grid=(nseq // S, nc),
        in_specs=[rows(3 * W, 0), rows(lw, 0), rows(W, z_col),
                  per_seq(3 * W), per_seq(lw), state,
                  vec(3 * W), vec(lw), vec(W), full(wup), vec(W), full(aup), vec(W), vec(W), vec(W), vec(W), vec(W),
                  full(e_mat), full(bd_mask), full(ltri)],
        out_specs=[rows(W, 0), state, per_seq(3 * W), per_seq(lw)],
        out_shape=[jax.ShapeDtypeStruct((nseq, t, W), BF16),
                   jax.ShapeDtypeStruct((nseq, n_heads, HEAD_A, HEAD_A), F32),
                   jax.ShapeDtypeStruct((nseq, 1, 3 * W), F32),
                   jax.ShapeDtypeStruct((nseq, 1, lw), F32)],
        scratch_shapes=[pltpu.VMEM((S, C + 8, 3 * W), F32), pltpu.VMEM((S, C + 8, lw), F32),
                        pltpu.VMEM((S, W // SLAB, SLAB, SLAB), F32)],
        compiler_params=_params("parallel", "arbitrary"),
        name="rwkv7",
    )(proj3, lora.reshape(nseq, t, lw), proj3, shift_main, shift_lora, s0,
      row1(prm["a_mu"][:3 * W]), row1(prm["a_mu"][3 * W:]), row1(prm["a_w0"]), wup, row1(prm["a_a0"]), aup,
      row1(prm["a_k_k"]), row1(prm["a_k_a"]), row1(prm["a_r_k"]), row1(prm["a_gn_w"]), row1(prm["a_gn_b"]),
      e_mat, bd_mask, ltri)
    return o_a.reshape(nseq * t, W), s_new, sh_main, sh_lora


COL_AZ, COL_BQ, COL_BK, COL_BV, COL_BZ, COL_CU, COL_MQ = 3, 4, 5, 6, 7, 8, 11


def trunk_layer(x, h, nseq, t, mk, mv, shift_prev, s_prev, past_k, past_v, prm, layer, depth, kv_out, want_vn):
    m, d = x.shape
    W = d // 4
    tm = min(512, m)
    if h is None:
        h = rmsnorm_rows(x, prm["g_pre"], min(256, m))
    proj = matmul(h, prm["w_in_main"], layer, min(1024, m), 1024)
    lora = matmul(h, prm["w_in_lora"], layer, min(1024, m), prm["w_in_lora"].shape[2])

    o_a, s_new, sh_main, sh_lora = rwkv7(proj, lora, COL_AZ, shift_prev[..., :3 * W], shift_prev[..., 3 * W:],
                                         s_prev, prm, nseq, t)
    shift_new = jnp.concatenate([sh_main, sh_lora], axis=-1)

    heads_b = W // HEAD_B
    hb = lambda col: col * heads_b
    o_b, kv_out = stick_breaking(proj, hb(COL_BQ), hb(COL_BZ), hb(COL_BK), hb(COL_BV), past_k, past_v,
                                 nseq, t, heads_b, layer, depth, kv_out)

    clen = min(t, prm["c_ws"].shape[1])
    groups = prm["c_ws"].shape[0]
    bs_full = jnp.repeat(prm["c_bs"][:, :clen].T, W // groups, axis=1)
    o_c, *vn_c = chunk_mlp(proj, COL_CU, prm["c_ws"][:, :clen, :clen], bs_full, prm["c_ln_w"], prm["c_ln_b"],
                           m, clen, max(1, min(512, t) // clen), want_vn)

    o_m = memory_attention(proj, COL_MQ, mk, mv, nseq, t, min(512, t))

    merged = gated_merge(h, (o_a, o_b, o_c, o_m), prm["w_gate"], prm["b_gate"], prm["w_br"], layer, tm,
                         512 if m > tm else 256)
    x_new, h_next = out_proj_norm_residual(merged, prm["w_out"], layer, x, prm["g_post"], prm["g_pre_next"], tm, 512)
    return x_new, h_next, shift_new, s_new, kv_out, vn_c


def kernel(x_prompt, x_sample, cache_mem_k, cache_mem_v, cache_sb_k, cache_sb_v, state_rwkv, state_shift, mem_prompt, g_pre, g_post, w_in, a_mu, a_w0, a_w_up, a_a0, a_a_up, a_k_k, a_k_a, a_r_k, a_gn_w, a_gn_b, c_ws, c_bs, c_ln_w, c_ln_b, g_mem, w_mem_kv, w_gate, b_gate, w_br, w_out):
    bp, tp, d = x_prompt.shape
    bs_, ts, _ = x_sample.shape
    depth = w_in.shape[0]
    W = d // 4
    n_mem = mem_prompt.shape[1]
    m_heads = cache_mem_k.shape[3]
    b_heads = cache_sb_k.shape[3]
    a_heads = state_rwkv.shape[2]
    shift_w = state_shift.shape[-1]
    n_past = cache_sb_k.shape[2]
    lora_lo, lora_hi = 3 * W, shift_w

    yp = x_prompt.reshape(bp * tp, d)
    ys = x_sample.reshape(bs_ * ts, d)
    mem_rows = mem_prompt.reshape(bp * n_mem, d)
    mem_out, rw_p, sh_p, rw_s, sh_s, cv_s = [[], []], [], [], [], [], []
    kv_p = kv_s = hp = hs = None
    dense = {
        "w_in_main": drop_cols_cast(w_in, lora_lo, lora_hi, W, 1024),
        "w_in_lora": w_in[:, :, lora_lo:lora_hi].astype(BF16),
        "w_gate": w_gate.astype(BF16), "w_br": w_br.astype(BF16), "w_out": w_out.astype(BF16),
    }
    w_mem16 = w_mem_kv.astype(BF16)
    for l in range(depth):
        prm = dict(dense)
        prm.update({
            "g_pre": g_pre[l], "g_post": g_post[l], "g_pre_next": g_pre[l + 1] if l + 1 < depth else None,
            "a_mu": a_mu[l], "a_w0": a_w0[l], "a_w_up": a_w_up[l], "a_a0": a_a0[l], "a_a_up": a_a_up[l],
            "a_k_k": a_k_k[l].reshape(-1), "a_k_a": a_k_a[l].reshape(-1), "a_r_k": a_r_k[l].reshape(-1),
            "a_gn_w": a_gn_w[l].reshape(-1), "a_gn_b": a_gn_b[l].reshape(-1),
            "c_ws": c_ws[l], "c_bs": c_bs[l], "c_ln_w": c_ln_w[l], "c_ln_b": c_ln_b[l], "b_gate": b_gate[l],
        })
        kv = matmul(rmsnorm_rows(mem_rows, g_mem[l], 256), w_mem16, l, min(1024, bp * n_mem), 1024)
        mk = kv[:, :W].reshape(bp, n_mem, W)
        mv = kv[:, W:].reshape(bp, n_mem, W)
        shift0 = jnp.zeros((bp, 1, shift_w), F32)
        s0 = jnp.zeros((bp, a_heads, HEAD_A, HEAD_A), F32)
        yp, hp, sh, st, kv_p, _ = trunk_layer(yp, hp, bp, tp, mk.astype(BF16), mv.astype(BF16), shift0, s0, None, None,
                                              prm, l, depth, kv_p, False)
        mem_out[0].append(mk.reshape(bp, n_mem, m_heads, W // m_heads))
        mem_out[1].append(mv.reshape(bp, n_mem, m_heads, W // m_heads))
        rw_p.append(st)
        sh_p.append(sh)
        ys, hs, sh, st, kv_s, cvn = trunk_layer(
            ys, hs, bs_, ts, cache_mem_k[l].reshape(bs_, n_mem, W).astype(BF16),
            cache_mem_v[l].reshape(bs_, n_mem, W).astype(BF16), state_shift[l], state_rwkv[l],
            cache_sb_k, cache_sb_v, prm, l, depth, kv_s, True)
        rw_s.append(st)
        sh_s.append(sh)
        cv_s.append(cvn[0].reshape(bs_, ts, W))
    heads_p = lambda a: a.reshape(depth, bp, tp, b_heads, HEAD_B)
    heads_s = lambda a: a.reshape(depth, bs_, ts, b_heads, HEAD_B)
    return (yp.reshape(bp, tp, d), ys.reshape(bs_, ts, d), jnp.stack(mem_out[0]), jnp.stack(mem_out[1]),
            heads_p(kv_p[0]), heads_p(kv_p[1]), jnp.stack(rw_p), jnp.stack(sh_p),
            heads_s(kv_s[0]), heads_s(kv_s[1]), jnp.stack(rw_s), jnp.stack(sh_s), jnp.stack(cv_s))
```

```python
import functools
import math

import jax
import jax.numpy as jnp
from jax import lax
from jax.experimental import pallas as pl
from jax.experimental.pallas import tpu as pltpu

F32 = jnp.float32
BF16 = jnp.bfloat16

NORM_EPS = 1e-6
GN_EPS = 64e-5
HEAD_A = 64
GROUP_A = 4
SLAB = HEAD_A * GROUP_A
HEAD_B = 128
HEAD_M = 256
GROUP_C = 128
RWKV_CHUNK = 64
VMEM_LIMIT_BYTES = 56 * 1024 * 1024


def _params(*sem):
    return pltpu.CompilerParams(dimension_semantics=sem, vmem_limit_bytes=VMEM_LIMIT_BYTES)


def _split2(x):
    hi = x.astype(BF16)
    lo = (x - hi.astype(F32)).astype(BF16)
    return hi, lo


def _dot(a, b):
    return jnp.dot(a, b, preferred_element_type=F32)


def _dot_nt(a, b):
    return lax.dot_general(a, b, (((1,), (1,)), ((), ())), preferred_element_type=F32)


def _dot_row_halves(dot, a, b):
    half = a.shape[0] // 2
    return jnp.concatenate([dot(a[:half], b), dot(a[half:], b)], axis=0)


def _rmsnorm_kernel(x_ref, g_ref, o_ref):
    x = x_ref[...]
    ms = jnp.mean(x * x, axis=-1, keepdims=True)
    o_ref[...] = (x * lax.rsqrt(ms + NORM_EPS) * g_ref[...]).astype(o_ref.dtype)


def rmsnorm_rows(x, g, tm):
    m, d = x.shape
    return pl.pallas_call(
        _rmsnorm_kernel,
        grid=(m // tm,),
        in_specs=[pl.BlockSpec((tm, d), lambda i: (i, 0)), pl.BlockSpec((1, d), lambda i: (0, 0))],
        out_specs=pl.BlockSpec((tm, d), lambda i: (i, 0)),
        out_shape=jax.ShapeDtypeStruct((m, d), BF16),
        compiler_params=_params("parallel"),
        name="rmsnorm_rows",
    )(x, g.reshape(1, d))


POST_ROWS = 64


def _out_proj_kernel(*refs, tn, with_next):
    if with_next:
        m_ref, w_ref, x_ref, g_ref, gn_ref, o_ref, hn_ref, x_rows, ssq = refs
    else:
        m_ref, w_ref, x_ref, g_ref, o_ref, x_rows, ssq = refs
    j = pl.program_id(1)
    cols = pl.ds(pl.multiple_of(j * tn, tn), tn)
    y = _dot(m_ref[...], w_ref[...])
    sq = jnp.sum(y * y, axis=-1, keepdims=True)
    o_ref[:, cols] = y * g_ref[...]
    x_rows[:, cols] = x_ref[...]

    @pl.when(j == 0)
    def _first():
        ssq[...] = sq

    @pl.when(j > 0)
    def _rest():
        ssq[...] += sq

    @pl.when(j == pl.num_programs(1) - 1)
    def _normalise():
        inv_d = 1.0 / o_ref.shape[1]
        for r0 in range(0, o_ref.shape[0], POST_ROWS):
            rows = slice(r0, r0 + POST_ROWS)
            scale = lax.rsqrt(ssq[rows, :] * inv_d + NORM_EPS)
            x_new = x_rows[rows, :] + o_ref[rows, :] * scale
            o_ref[rows, :] = x_new
            if with_next:
                ms = jnp.mean(x_new * x_new, axis=-1, keepdims=True)
                hn_ref[rows, :] = (x_new * lax.rsqrt(ms + NORM_EPS) * gn_ref[...]).astype(hn_ref.dtype)


def out_proj_norm_residual(merged, w, layer, x, g, g_next, tm, tn):
    m, d = x.shape
    with_next = g_next is not None
    row = pl.BlockSpec((tm, d), lambda i, j: (i, 0))
    gain = pl.BlockSpec((1, d), lambda i, j: (0, 0))
    out = pl.pallas_call(
        functools.partial(_out_proj_kernel, tn=tn, with_next=with_next),
        grid=(m // tm, d // tn),
        in_specs=[row, pl.BlockSpec((None, d, tn), lambda i, j: (layer, 0, j)),
                  pl.BlockSpec((tm, tn), lambda i, j: (i, j)), pl.BlockSpec((1, tn), lambda i, j: (0, j))]
                 + ([gain] if with_next else []),
        out_specs=[row, row] if with_next else [row],
        out_shape=[jax.ShapeDtypeStruct((m, d), F32)] + ([jax.ShapeDtypeStruct((m, d), BF16)] if with_next else []),
        scratch_shapes=[pltpu.VMEM((tm, d), F32), pltpu.VMEM((tm, 1), F32)],
        compiler_params=_params("parallel", "arbitrary"),
        name="out_proj_norm_residual",
    )(merged, w, x, g.reshape(1, d), *([g_next.reshape(1, d)] if with_next else []))
    return out if with_next else (out[0], None)


def _drop_cols_kernel(a_ref, b_ref, o_ref, *, first_shifted, gap):
    j = pl.program_id(2)

    @pl.when(j < first_shifted)
    def _before_gap():
        o_ref[...] = a_ref[...].astype(o_ref.dtype)

    @pl.when(j >= first_shifted)
    def _after_gap():
        o_ref[...] = jnp.concatenate([a_ref[:, gap:], b_ref[...]], axis=1).astype(o_ref.dtype)


def drop_cols_cast(w, lo, hi, cw, tk):
    depth, k, n = w.shape
    gap = hi - lo
    n_out = n - gap
    return pl.pallas_call(
        functools.partial(_drop_cols_kernel, first_shifted=lo // cw, gap=gap),
        grid=(depth, k // tk, n_out // cw),
        in_specs=[pl.BlockSpec((None, tk, cw), lambda l, i, j: (l, i, j)),
                  pl.BlockSpec((None, tk, gap), lambda l, i, j: (l, i, (j + 1) * (cw // gap)))],
        out_specs=pl.BlockSpec((None, tk, cw), lambda l, i, j: (l, i, j)),
        out_shape=jax.ShapeDtypeStruct((depth, k, n_out), BF16),
        compiler_params=_params("parallel", "parallel", "parallel"),
        name="drop_cols_cast",
    )(w, w)
def _matmul_kernel(x_ref, w_ref, o_ref):
    o_ref[...] = _dot(x_ref[...], w_ref[...]).astype(o_ref.dtype)


def matmul(x, w, layer, tm, tn, out_dtype=F32):
    m, k = x.shape
    n = w.shape[2]
    return pl.pallas_call(
        _matmul_kernel,
        grid=(n // tn, m // tm),
        in_specs=[pl.BlockSpec((tm, k), lambda j, i: (i, 0)),
                  pl.BlockSpec((None, k, tn), lambda j, i: (layer, 0, j))],
        out_specs=pl.BlockSpec((tm, tn), lambda j, i: (i, j)),
        out_shape=jax.ShapeDtypeStruct((m, n), out_dtype),
        compiler_params=_params("parallel", "parallel"),
        name="matmul",
    )(x, w)


def _merge_kernel(h_ref, oa_ref, ob_ref, oc_ref, om_ref, wg_ref, bg_ref, wbr_ref, out_ref):
    h = h_ref[...]
    acc = None
    for n, o_ref in enumerate((oa_ref, ob_ref, oc_ref, om_ref)):
        gate = jax.nn.sigmoid(_dot(h, wg_ref[n]) + bg_ref[n])
        term = gate * _dot(o_ref[...], wbr_ref[n])
        acc = term if acc is None else acc + term
    out_ref[...] = acc.astype(out_ref.dtype)


def gated_merge(h, branches, wg, bg, wbr, layer, tm, tn):
    m, d = h.shape
    _, nb, w, _ = wbr.shape
    resident = dict(pipeline_mode=pl.Buffered(1)) if m > tm else {}
    o_spec = pl.BlockSpec((tm, w), lambda j, i: (i, 0))
    return pl.pallas_call(
        _merge_kernel,
        grid=(d // tn, m // tm),
        in_specs=[pl.BlockSpec((tm, d), lambda j, i: (i, 0)), o_spec, o_spec, o_spec, o_spec,
                  pl.BlockSpec((None, nb, d, tn), lambda j, i: (layer, 0, 0, j), **resident),
                  pl.BlockSpec((nb, 1, tn), lambda j, i: (0, 0, j)),
                  pl.BlockSpec((None, nb, w, tn), lambda j, i: (layer, 0, 0, j), **resident)],
        out_specs=pl.BlockSpec((tm, tn), lambda j, i: (i, j)),
        out_shape=jax.ShapeDtypeStruct((m, d), BF16),
        compiler_params=_params("parallel", "parallel"),
        name="gated_merge",
    )(h, *branches, wg, bg.reshape(nb, 1, d), wbr)


def _memattn_kernel(q_ref, mk_ref, mv_ref, o_ref, *, heads):
    scale = 1.0 / math.sqrt(HEAD_M)
    for hd in range(heads):
        sl = slice(hd * HEAD_M, (hd + 1) * HEAD_M)
        q = q_ref[:, sl].astype(BF16)
        s = _dot_nt(q, mk_ref[0, :, sl]) * scale
        s = s - jnp.max(s, axis=-1, keepdims=True)
        p = jnp.exp(s)
        p = p * (1.0 / jnp.sum(p, axis=-1, keepdims=True))
        o_ref[:, sl] = _dot(p.astype(BF16), mv_ref[0, :, sl]).astype(o_ref.dtype)


def memory_attention(proj, q_col, mk, mv, nseq, t, tq):
    n_mem, w = mk.shape[1:]
    per_seq = t // tq
    mem_spec = pl.BlockSpec((1, n_mem, w), lambda i: (i // per_seq, 0, 0))
    return pl.pallas_call(
        functools.partial(_memattn_kernel, heads=w // HEAD_M),
        grid=(nseq * per_seq,),
        in_specs=[pl.BlockSpec((tq, w), lambda i: (i, q_col)), mem_spec, mem_spec],
        out_specs=pl.BlockSpec((tq, w), lambda i: (i, 0)),
        out_shape=jax.ShapeDtypeStruct((nseq * t, w), BF16),
        compiler_params=_params("parallel"),
        name="memory_attention",
    )(proj, mk, mv)


def _cmlp_kernel(u_ref, v_ref, z_ref, ws_ref, bs_ref, lnw_ref, lnb_ref, o_ref, *maybe_vn_ref, clen, chunks):
    v = v_ref[...]
    mu = jnp.mean(v, axis=-1, keepdims=True)
    var = jnp.mean(jnp.square(v - mu), axis=-1, keepdims=True)
    vn = (v - mu) * lax.rsqrt(var + NORM_EPS) * lnw_ref[...] + lnb_ref[...]
    for vn_ref in maybe_vn_ref:
        vn_ref[...] = vn
    vn16 = vn.astype(BF16)
    groups = vn.shape[1] // GROUP_C
    row = lax.broadcasted_iota(jnp.int32, (clen, clen), 0)
    col = lax.broadcasted_iota(jnp.int32, (clen, clen), 1)
    for g in range(groups):
        wm = jnp.where(row >= col, ws_ref[g], 0.0).astype(BF16)
        cs = slice(g * GROUP_C, (g + 1) * GROUP_C)
        for c in range(chunks):
            rs = slice(c * clen, (c + 1) * clen)
            s = _dot(wm, vn16[rs, cs]) + bs_ref[:, cs]
            o_ref[rs, cs] = (u_ref[rs, cs] * s * jax.nn.silu(z_ref[rs, cs])).astype(o_ref.dtype)


def chunk_mlp(proj, u_col, ws, bs_full, ln_w, ln_b, rows, clen, chunks, want_vn):
    w = ln_w.shape[0]
    tm = clen * chunks
    col = lambda c: pl.BlockSpec((tm, w), lambda i: (i, c))
    const2 = lambda a: pl.BlockSpec(a.shape, lambda i: (0, 0))
    out = pl.BlockSpec((tm, w), lambda i: (i, 0))
    out_dtypes = [BF16, F32] if want_vn else [BF16]
    return pl.pallas_call(
        functools.partial(_cmlp_kernel, clen=clen, chunks=chunks),
        grid=(rows // tm,),
        in_specs=[col(u_col), col(u_col + 1), col(u_col + 2),
                  pl.BlockSpec(ws.shape, lambda i: (0, 0, 0)), const2(bs_full),
                  pl.BlockSpec((1, w), lambda i: (0, 0)), pl.BlockSpec((1, w), lambda i: (0, 0))],
        out_specs=[out] * len(out_dtypes),
        out_shape=[jax.ShapeDtypeStruct((rows, w), dt) for dt in out_dtypes],
        compiler_params=_params("parallel"),
        name="chunk_mlp",
    )(proj, proj, proj, ws, bs_full, ln_w.reshape(1, w), ln_b.reshape(1, w))


SB_HEADS_PER_STEP = 2
SB_BLOCK = 256


def _sb_group(q16, k16, v16, tk, u, masks, carry):
    tq = q16.shape[0]
    nb = k16.shape[0] // tk
    z_all = _dot_nt(q16, k16)
    nl, log_beta = [], []
    for j in range(nb):
        z2 = z_all[:, j * tk:(j + 1) * tk] * (math.log2(math.e) / math.sqrt(HEAD_B))
        neg_abs = lax.bitcast_convert_type(lax.bitcast_convert_type(z2, jnp.uint32) | jnp.uint32(0x80000000), F32)
        nl_j = jnp.maximum(z2, 0.0) + jnp.log2(1.0 + jnp.exp2(neg_abs))
        log_beta.append(z2 - nl_j)
        nl.append(jnp.where(masks[j], nl_j, 0.0) if j in masks else nl_j)
    lhs = jnp.concatenate([jnp.concatenate(_split2(nl_j), axis=1) for nl_j in nl], axis=0)
    cs = _dot_row_halves(_dot, lhs, u)
    att = [None] * nb
    for j in reversed(range(nb)):
        cs_j = cs[j * tq:(j + 1) * tq]
        att_j = jnp.exp2(log_beta[j] - cs_j - carry)
        att[j] = (jnp.where(masks[j], att_j, 0.0) if j in masks else att_j).astype(BF16)
        carry = carry + (cs_j[:, 0:1] + nl[j][:, 0:1])
    return _dot_row_halves(_dot, jnp.concatenate(att, axis=1), v16), carry


def _head_slices(ref):
    return [slice(i * HEAD_B, (i + 1) * HEAD_B) for i in range(ref.shape[1] // HEAD_B)]


def _sb_prompt_kernel(*refs, tq, nq):
    q_ref, z_ref, k_ref, v_ref, u_ref = refs[:5]
    o_ref, kout_ref, vout_ref = refs[-3:]
    kout_ref[...] = k_ref[...]
    vout_ref[...] = v_ref[...]
    row = lax.broadcasted_iota(jnp.int32, (tq, tq), 0)
    col = lax.broadcasted_iota(jnp.int32, (tq, tq), 1)
    diagonal = col < row
    k16 = k_ref[...].astype(BF16)
    v16 = v_ref[...].astype(BF16)
    for qi in range(nq):
        rows = slice(qi * tq, (qi + 1) * tq)
        n_keys = (qi + 1) * tq
        for s in _head_slices(q_ref):
            out, _ = _sb_group(q_ref[rows, s].astype(BF16), k16[0:n_keys, s], v16[0:n_keys, s], tq, u_ref[...],
                               {qi: diagonal}, jnp.zeros((tq, 1), F32))
            o_ref[rows, s] = (out * jax.nn.silu(z_ref[rows, s])).astype(o_ref.dtype)


def _sb_sample_kernel(*refs, tk):
    q_ref, z_ref, kn_ref, vn_ref, kp_ref, vp_ref, ud_ref, uf_ref = refs[:8]
    o_ref, kout_ref, vout_ref = refs[-3:]
    kout_ref[...] = kn_ref[...]
    vout_ref[...] = vn_ref[...]
    tq = q_ref.shape[0]
    row = lax.broadcasted_iota(jnp.int32, (tq, tq), 0)
    col = lax.broadcasted_iota(jnp.int32, (tq, tq), 1)
    heads = _head_slices(q_ref)
    n_past = kp_ref.shape[0] // len(heads)
    for hd, s in enumerate(heads):
        q16 = q_ref[:, s].astype(BF16)
        out_new, carry = _sb_group(q16, kn_ref[:, s].astype(BF16), vn_ref[:, s].astype(BF16), tq, ud_ref[...],
                                   {0: col < row}, jnp.zeros((tq, 1), F32))
        own_rows = pl.ds(hd, n_past, stride=len(heads))
        out_past, _ = _sb_group(q16, kp_ref[own_rows, :].astype(BF16), vp_ref[own_rows, :].astype(BF16), tk,
                                uf_ref[...], {}, carry)
        o_ref[:, s] = ((out_new + out_past) * jax.nn.silu(z_ref[:, s])).astype(o_ref.dtype)


def _suffix_matrix(tk):
    s_later = lax.broadcasted_iota(jnp.int32, (tk, tk), 0)
    s_here = lax.broadcasted_iota(jnp.int32, (tk, tk), 1)
    u = (s_later > s_here).astype(BF16)
    return jnp.concatenate([u, u], axis=0)


def stick_breaking(proj, q_col, z_col, k_col, v_col, past_k, past_v, nseq, t, heads, layer, depth, kv_out):
    hp = SB_HEADS_PER_STEP if past_k is None else heads
    wide = hp * HEAD_B
    tq = min(SB_BLOCK, t)
    seq = lambda rows, col: pl.BlockSpec((rows, wide), lambda b, h: (b, col // hp + h))
    const = lambda a: pl.BlockSpec(a.shape, lambda b, h: (0, 0))
    if past_k is None:
        u = _suffix_matrix(tq)
        body = functools.partial(_sb_prompt_kernel, tq=tq, nq=t // tq)
        in_specs = [seq(t, q_col), seq(t, z_col), seq(t, k_col), seq(t, v_col), const(u)]
        args = [proj, proj, proj, proj, u]
    else:
        assert t == tq
        n_past = past_k.shape[2]
        past = pl.BlockSpec((None, None, n_past * heads, HEAD_B), lambda b, h: (layer, b, 0, 0))
        past_k, past_v = (a.reshape(depth, nseq, n_past * heads, HEAD_B) for a in (past_k, past_v))
        ud, uf = _suffix_matrix(tq), _suffix_matrix(SB_BLOCK)
        body = functools.partial(_sb_sample_kernel, tk=SB_BLOCK)
        in_specs = [seq(t, q_col), seq(t, z_col), seq(t, k_col), seq(t, v_col), past, past, const(ud), const(uf)]
        args = [proj, proj, proj, proj, past_k, past_v, ud, uf]
    aliases = {}
    if kv_out is not None:
        aliases = {len(args): 1, len(args) + 1: 2}
        in_specs = in_specs + [pl.BlockSpec(memory_space=pl.ANY)] * 2
        args = args + list(kv_out)
    kv_shape = jax.ShapeDtypeStruct((depth, nseq * t, heads * HEAD_B), F32)
    kv_spec = pl.BlockSpec((None, t, wide), lambda b, h: (layer, b, h))
    o_b, k_all, v_all = pl.pallas_call(
        body,
        grid=(nseq, heads // hp),
        in_specs=in_specs,
        out_specs=[seq(t, 0), kv_spec, kv_spec],
        out_shape=[jax.ShapeDtypeStruct((nseq * t, heads * HEAD_B), BF16), kv_shape, kv_shape],
        input_output_aliases=aliases,
        compiler_params=_params("parallel", "parallel"),
        name="stick_breaking",
    )(*args)
    return o_b, (k_all, v_all)


def _block_stack(x, lane_masks):
    return jnp.concatenate([jnp.where(m, x, 0.0) for m in lane_masks], axis=0)


def _rwkv_kernel(x_ref, lora_ref, z_ref, sh_main_ref, sh_lora_ref, s0_ref,
                 mu_main_ref, mu_lora_ref, w0_ref, wup_ref, a0_ref, aup_ref, kk_ref, ka_ref, rk_ref,
                 gnw_ref, gnb_ref, e_ref, bd_ref, ltri_ref,
                 o_ref, sout_ref, shm_out_ref, shl_out_ref,
                 xbuf, lbuf, sbd, *, chunk, width):
    c = pl.program_id(1)
    n_chunks = pl.num_programs(1)
    C, W = chunk, width
    n_seq = x_ref.shape[0]
    n_slabs = W // SLAB
    n_heads = W // HEAD_A

    @pl.when(c == 0)
    def _init():
        sbd[...] = jnp.zeros_like(sbd)
        for s in range(n_seq):
            xbuf[s, 7:8, :] = sh_main_ref[s]
            lbuf[s, 7:8, :] = sh_lora_ref[s]
            for hd in range(n_heads):
                g, j = divmod(hd, GROUP_A)
                sbd[s, g, j * HEAD_A:(j + 1) * HEAD_A, j * HEAD_A:(j + 1) * HEAD_A] = s0_ref[s, hd]

    def shifted(ref, buf, mu_ref):
        rows = []
        for s in range(n_seq):
            x = ref[s]
            buf[s, 8:8 + C, :] = x
            rows.append(x + mu_ref[...] * (buf[s, 7:7 + C, :] - x))
            buf[s, 7:8, :] = x[C - 1:C, :]
        return jnp.concatenate(rows, axis=0)

    xs = shifted(x_ref, xbuf, mu_main_ref)
    lo_s = shifted(lora_ref, lbuf, mu_lora_ref)
    seq_rows = [slice(s * C, (s + 1) * C) for s in range(n_seq)]
    slabs = [slice(g * SLAB, (g + 1) * SLAB) for g in range(n_slabs)]

    r, k, v = xs[:, :W], xs[:, W:2 * W], xs[:, 2 * W:]
    w_pre = w0_ref[...] + _dot(jnp.tanh(lo_s).astype(BF16), wup_ref[...])
    ld = -math.exp(-0.5) * jax.nn.sigmoid(w_pre)
    a = jax.nn.sigmoid(a0_ref[...] + _dot(lo_s.astype(BF16), aup_ref[...]))

    e_mat = e_ref[...]

    def seg_sum(val):
        n = val.shape[0]
        hi, lo = _split2(val)
        out = _dot(jnp.concatenate([part[:, sl] for part in (hi, lo) for sl in slabs], axis=0), e_mat)
        return jnp.concatenate([out[g * n:(g + 1) * n] + out[(n_slabs + g) * n:(n_slabs + g + 1) * n]
                                for g in range(n_slabs)], axis=1)

    kk = k * kk_ref[...]
    kk = kk * lax.rsqrt(jnp.maximum(seg_sum(kk * kk), 1e-24))
    kmod = k * (1.0 + (a - 1.0) * ka_ref[...])

    ld_hi = ld.astype(BF16)
    ld_r1 = ld - ld_hi.astype(F32)
    ld_mid = ld_r1.astype(BF16)
    ld_lo = (ld_r1 - ld_mid.astype(F32)).astype(BF16)
    ltri = ltri_ref[...]
    lp = _dot(ltri, ld_hi) + _dot(ltri, ld_mid) + _dot(ltri, ld_lo)
    lp_last = [lp[rs.stop - 1:rs.stop, :] for rs in seq_rows]
    lp_end = jnp.concatenate([jnp.broadcast_to(row, (C, W)) for row in lp_last], axis=0)
    e_neg = jnp.exp(-lp)
    kka = kk * a
    kap = kk * jnp.exp(lp - ld)
    bet = kka * e_neg
    kt = kmod * e_neg
    rt = r * jnp.exp(lp)
    e_end = jnp.exp(lp_end - lp)
    kt_end = kmod * e_end
    bet_end = kka * e_end
    dec_end = [jnp.exp(row) for row in lp_last]

    lane = lax.broadcasted_iota(jnp.int32, (1, SLAB), 1)
    lane_masks = [(lane >= j * HEAD_A) & (lane < (j + 1) * HEAD_A) for j in range(GROUP_A)]
    t_row = lax.broadcasted_iota(jnp.int32, (C, GROUP_A * C), 0)
    s_col = lax.broadcasted_iota(jnp.int32, (C, GROUP_A * C), 1) & (C - 1)
    strict, incl = s_col < t_row, s_col <= t_row
    stack = lambda val16: _block_stack(val16, lane_masks)
    nh = GROUP_A * C
    units = [(s, g) for s in range(n_seq) for g in range(n_slabs)]
    per_slab = lambda fn: [fn(u, seq_rows[s], slabs[g]) for u, (s, g) in enumerate(units)]

    kap16, rt16, bet16, kt16, v16 = (val.astype(BF16) for val in (kap, rt, bet, kt, v))
    s_old = [sbd[s, g] for s, g in units]
    lhs = per_slab(lambda g, rs, sl: jnp.concatenate([kap16[rs, sl], rt16[rs, sl]], axis=0))
    rhs = per_slab(lambda g, rs, sl: jnp.concatenate([stack(bet16[rs, sl]), stack(kt16[rs, sl])], axis=0))
    sc = per_slab(lambda g, rs, sl: _dot_nt(lhs[g], rhs[g]))
    ls = per_slab(lambda g, rs, sl: _dot_nt(lhs[g], s_old[g].astype(BF16)))
    v_stack = per_slab(lambda g, rs, sl: stack(v16[rs, sl]))
    p16 = per_slab(lambda g, rs, sl: jnp.where(strict, -sc[g][:C, :nh], 0.0).astype(BF16))
    xw = per_slab(lambda g, rs, sl: ls[g][:C]
                  + _dot(jnp.where(strict, sc[g][:C, nh:], 0.0).astype(BF16), v_stack[g]))
    for step in range(6):
        xw = per_slab(lambda g, rs, sl: xw[g] + _dot(p16[g], stack(xw[g].astype(BF16))))
        if step < 5:
            p16 = per_slab(lambda g, rs, sl: _dot(p16[g], stack(p16[g])).astype(BF16))
    ab_inc = per_slab(lambda g, rs, sl: jnp.concatenate(
        [jnp.where(incl, sc[g][C:, :nh], 0.0), jnp.where(incl, sc[g][C:, nh:], 0.0)], axis=1).astype(BF16))
    y = per_slab(lambda g, rs, sl: ls[g][C:] + _dot(
        ab_inc[g], jnp.concatenate([stack((-xw[g]).astype(BF16)), v_stack[g]], axis=0)))
    upd = per_slab(lambda g, rs, sl: _dot(
        jnp.concatenate([v[rs, sl], -xw[g]], axis=0).T.astype(BF16),
        jnp.concatenate([kt_end[rs, sl], bet_end[rs, sl]], axis=0).astype(BF16)))
    for u, (s, g) in enumerate(units):
        sbd[s, g] = s_old[u] * dec_end[s][:, slabs[g]] + upd[u] * bd_ref[...]

    y = jnp.concatenate([jnp.concatenate(y[s * n_slabs:(s + 1) * n_slabs], axis=1) for s in range(n_seq)], axis=0)
    inv_n = 1.0 / HEAD_A
    mean = seg_sum(y) * inv_n
    yc = y - mean
    var = seg_sum(yc * yc) * inv_n
    y = yc * lax.rsqrt(var + GN_EPS) * gnw_ref[...] + gnb_ref[...]
    y = y + seg_sum(r * kmod * rk_ref[...]) * v
    for s, rs in enumerate(seq_rows):
        o_ref[s] = (y[rs] * jax.nn.silu(z_ref[s])).astype(o_ref.dtype)

    @pl.when(c == n_chunks - 1)
    def _fin():
        for s in range(n_seq):
            shm_out_ref[s] = xbuf[s, 7:8, :]
            shl_out_ref[s] = lbuf[s, 7:8, :]
            for hd in range(n_heads):
                g, j = divmod(hd, GROUP_A)
                sout_ref[s, hd] = sbd[s, g, j * HEAD_A:(j + 1) * HEAD_A, j * HEAD_A:(j + 1) * HEAD_A]


RWKV_SEQS_PER_STEP = 2


def rwkv7(proj, lora, z_col, shift_main, shift_lora, s0, prm, nseq, t):
    W = prm["a_w0"].shape[0]
    C = RWKV_CHUNK
    S = RWKV_SEQS_PER_STEP
    nc = t // C
    n_heads = W // HEAD_A
    lw = lora.shape[1]
    half = lw // 2
    zpad = jnp.zeros((half, W), F32)
    wup = jnp.concatenate([prm["a_w_up"], zpad], axis=0).astype(BF16)
    aup = jnp.concatenate([zpad, prm["a_a_up"]], axis=0).astype(BF16)
    idx = jnp.arange(SLAB) // HEAD_A
    same = idx[:, None] == idx[None, :]
    ti = jnp.arange(S * C)
    ltri = ((ti[:, None] >= ti[None, :]) & (ti[:, None] // C == ti[None, :] // C)).astype(BF16)
    row1 = lambda a: a.reshape(1, -1)
    vec = lambda n: pl.BlockSpec((1, n), lambda b, c: (0, 0))
    full = lambda a: pl.BlockSpec(a.shape, lambda b, c: (0, 0))
    rows = lambda n, col: pl.BlockSpec((S, C, n), lambda b, c: (b, c, col))
    per_seq = lambda n: pl.BlockSpec((S, 1, n), lambda b, c: (b, 0, 0))
    state = pl.BlockSpec((S, n_heads, HEAD_A, HEAD_A), lambda b, c: (b, 0, 0, 0))
    e_mat, bd_mask = same.astype(BF16), same.astype(F32)
    proj3 = proj.reshape(nseq, t, proj.shape[1])
    o_a, s_new, sh_main, sh_lora = pl.pallas_call(
        functools.partial(_rwkv_kernel, chunk=C, width=W),
        grid=(nseq // S, nc),
        in_specs=[rows(3 * W, 0), rows(lw, 0), rows(W, z_col),
                  per_seq(3 * W), per_seq(lw), state,
                  vec(3 * W), vec(lw), vec(W), full(wup), vec(W), full(aup), vec(W), vec(W), vec(W), vec(W), vec(W),
                  full(e_mat), full(bd_mask), full(ltri)],
        out_specs=[rows(W, 0), state, per_seq(3 * W), per_seq(lw)],
        out_shape=[jax.ShapeDtypeStruct((nseq, t, W), BF16),
                   jax.ShapeDtypeStruct((nseq, n_heads, HEAD_A, HEAD_A), F32),
                   jax.ShapeDtypeStruct((nseq, 1, 3 * W), F32),
                   jax.ShapeDtypeStruct((nseq, 1, lw), F32)],
        scratch_shapes=[pltpu.VMEM((S, C + 8, 3 * W), F32), pltpu.VMEM((S, C + 8, lw), F32),
                        pltpu.VMEM((S, W // SLAB, SLAB, SLAB), F32)],
        compiler_params=_params("parallel", "arbitrary"),
        name="rwkv7",
    )(proj3, lora.reshape(nseq, t, lw), proj3, shift_main, shift_lora, s0,
      row1(prm["a_mu"][:3 * W]), row1(prm["a_mu"][3 * W:]), row1(prm["a_w0"]), wup, row1(prm["a_a0"]), aup,
      row1(prm["a_k_k"]), row1(prm["a_k_a"]), row1(prm["a_r_k"]), row1(prm["a_gn_w"]), row1(prm["a_gn_b"]),
      e_mat, bd_mask, ltri)
    return o_a.reshape(nseq * t, W), s_new, sh_main, sh_lora


COL_AZ, COL_BQ, COL_BK, COL_BV, COL_BZ, COL_CU, COL_MQ = 3, 4, 5, 6, 7, 8, 11


def trunk_layer(x, h, nseq, t, mk, mv, shift_prev, s_prev, past_k, past_v, prm, layer, depth, kv_out, want_vn):
    m, d = x.shape
    W = d // 4
    tm = min(512, m)
    if h is None:
        h = rmsnorm_rows(x, prm["g_pre"], min(256, m))
    proj = matmul(h, prm["w_in_main"], layer, min(1024, m), 1024)
    lora = matmul(h, prm["w_in_lora"], layer, min(1024, m), prm["w_in_lora"].shape[2])

    o_a, s_new, sh_main, sh_lora = rwkv7(proj, lora, COL_AZ, shift_prev[..., :3 * W], shift_prev[..., 3 * W:],
                                         s_prev, prm, nseq, t)
    shift_new = jnp.concatenate([sh_main, sh_lora], axis=-1)

    heads_b = W // HEAD_B
    hb = lambda col: col * heads_b
    o_b, kv_out = stick_breaking(proj, hb(COL_BQ), hb(COL_BZ), hb(COL_BK), hb(COL_BV), past_k, past_v,
                                 nseq, t, heads_b, layer, depth, kv_out)

    clen = min(t, prm["c_ws"].shape[1])
    groups = prm["c_ws"].shape[0]
    bs_full = jnp.repeat(prm["c_bs"][:, :clen].T, W // groups, axis=1)
    o_c, *vn_c = chunk_mlp(proj, COL_CU, prm["c_ws"][:, :clen, :clen], bs_full, prm["c_ln_w"], prm["c_ln_b"],
                           m, clen, max(1, min(512, t) // clen), want_vn)

    o_m = memory_attention(proj, COL_MQ, mk, mv, nseq, t, min(512, t))

    merged = gated_merge(h, (o_a, o_b, o_c, o_m), prm["w_gate"], prm["b_gate"], prm["w_br"], layer, tm,
                         512 if m > tm else 256)
    x_new, h_next = out_proj_norm_residual(merged, prm["w_out"], layer, x, prm["g_post"], prm["g_pre_next"], tm, 512)
    return x_new, h_next, shift_new, s_new, kv_out, vn_c


def kernel(x_prompt, x_sample, cache_mem_k, cache_mem_v, cache_sb_k, cache_sb_v, state_rwkv, state_shift, mem_prompt, g_pre, g_post, w_in, a_mu, a_w0, a_w_up, a_a0, a_a_up, a_k_k, a_k_a, a_r_k, a_gn_w, a_gn_b, c_ws, c_bs, c_ln_w, c_ln_b, g_mem, w_mem_kv, w_gate, b_gate, w_br, w_out):
    bp, tp, d = x_prompt.shape
    bs_, ts, _ = x_sample.shape
    depth = w_in.shape[0]
    W = d // 4
    n_mem = mem_prompt.shape[1]
    m_heads = cache_mem_k.shape[3]
    b_heads = cache_sb_k.shape[3]
    a_heads = state_rwkv.shape[2]
    shift_w = state_shift.shape[-1]
    n_past = cache_sb_k.shape[2]
    lora_lo, lora_hi = 3 * W, shift_w

    yp = x_prompt.reshape(bp * tp, d)
    ys = x_sample.reshape(bs_ * ts, d)
    mem_rows = mem_prompt.reshape(bp * n_mem, d)
    mem_out, rw_p, sh_p, rw_s, sh_s, cv_s = [[], []], [], [], [], [], []
    kv_p = kv_s = hp = hs = None
    dense = {
        "w_in_main": drop_cols_cast(w_in, lora_lo, lora_hi, W, 1024),
        "w_in_lora": w_in[:, :, lora_lo:lora_hi].astype(BF16),
        "w_gate": w_gate.astype(BF16), "w_br": w_br.astype(BF16), "w_out": w_out.astype(BF16),
    }
    w_mem16 = w_mem_kv.astype(BF16)
    for l in range(depth):
        prm = dict(dense)
        prm.update({
            "g_pre": g_pre[l], "g_post": g_post[l], "g_pre_next": g_pre[l + 1] if l + 1 < depth else None,
            "a_mu": a_mu[l], "a_w0": a_w0[l], "a_w_up": a_w_up[l], "a_a0": a_a0[l], "a_a_up": a_a_up[l],
            "a_k_k": a_k_k[l].reshape(-1), "a_k_a": a_k_a[l].reshape(-1), "a_r_k": a_r_k[l].reshape(-1),
            "a_gn_w": a_gn_w[l].reshape(-1), "a_gn_b": a_gn_b[l].reshape(-1),
            "c_ws": c_ws[l], "c_bs": c_bs[l], "c_ln_w": c_ln_w[l], "c_ln_b": c_ln_b[l], "b_gate": b_gate[l],
        })
        kv = matmul(rmsnorm_rows(mem_rows, g_mem[l], 256), w_mem16, l, min(1024, bp * n_mem), 1024)
        mk = kv[:, :W].reshape(bp, n_mem, W)
        mv = kv[:, W:].reshape(bp, n_mem, W)
        shift0 = jnp.zeros((bp, 1, shift_w), F32)
        s0 = jnp.zeros((bp, a_heads, HEAD_A, HEAD_A), F32)
        yp, hp, sh, st, kv_p, _ = trunk_layer(yp, hp, bp, tp, mk.astype(BF16), mv.astype(BF16), shift0, s0, None, None,
                                              prm, l, depth, kv_p, False)
        mem_out[0].append(mk.reshape(bp, n_mem, m_heads, W // m_heads))
        mem_out[1].append(mv.reshape(bp, n_mem, m_heads, W // m_heads))
        rw_p.append(st)
        sh_p.append(sh)
        ys, hs, sh, st, kv_s, cvn = trunk_layer(
            ys, hs, bs_, ts, cache_mem_k[l].reshape(bs_, n_mem, W).astype(BF16),
            cache_mem_v[l].reshape(bs_, n_mem, W).astype(BF16), state_shift[l], state_rwkv[l],
            cache_sb_k, cache_sb_v, prm, l, depth, kv_s, True)
        rw_s.append(st)
        sh_s.append(sh)
        cv_s.append(cvn[0].reshape(bs_, ts, W))
    heads_p = lambda a: a.reshape(depth, bp, tp, b_heads, HEAD_B)
    heads_s = lambda a: a.reshape(depth, bs_, ts, b_heads, HEAD_B)
    return (yp.reshape(bp, tp, d), ys.reshape(bs_, ts, d), jnp.stack(mem_out[0]), jnp.stack(mem_out[1]),
            heads_p(kv_p[0]), heads_p(kv_p[1]), jnp.stack(rw_p), jnp.stack(sh_p),
            heads_s(kv_s[0]), heads_s(kv_s[1]), jnp.stack(rw_s), jnp.stack(sh_s), jnp.stack(cv_s))
```

```python
import functools
import math

import jax
import jax.numpy as jnp
from jax import lax
from jax.experimental import pallas as pl
from jax.experimental.pallas import tpu as pltpu

F32 = jnp.float32
BF16 = jnp.bfloat16

NORM_EPS = 1e-6
GN_EPS = 64e-5
HEAD_A = 64
GROUP_A = 2
SLAB = HEAD_A * GROUP_A
SEG_LANES = 256
HEAD_B = 128
HEAD_M = 256
GROUP_C = 128
RWKV_CHUNK = 64
VMEM_LIMIT_BYTES = 56 * 1024 * 1024


def _params(*sem):
    return pltpu.CompilerParams(dimension_semantics=sem, vmem_limit_bytes=VMEM_LIMIT_BYTES)


def _split2(x):
    hi = x.astype(BF16)
    lo = (x - hi.astype(F32)).astype(BF16)
    return hi, lo


def _dot(a, b):
    return jnp.dot(a, b, preferred_element_type=F32)


def _dot_nt(a, b):
    return lax.dot_general(a, b, (((1,), (1,)), ((), ())), preferred_element_type=F32)


def _dot_row_halves(dot, a, b):
    half = a.shape[0] // 2
    return jnp.concatenate([dot(a[:half], b), dot(a[half:], b)], axis=0)


def _rmsnorm_kernel(x_ref, g_ref, o_ref):
    x = x_ref[...]
    ms = jnp.mean(x * x, axis=-1, keepdims=True)
    o_ref[...] = (x * lax.rsqrt(ms + NORM_EPS) * g_ref[...]).astype(o_ref.dtype)


def rmsnorm_rows(x, g, tm):
    m, d = x.shape
    return pl.pallas_call(
        _rmsnorm_kernel,
        grid=(m // tm,),
        in_specs=[pl.BlockSpec((tm, d), lambda i: (i, 0)), pl.BlockSpec((1, d), lambda i: (0, 0))],
        out_specs=pl.BlockSpec((tm, d), lambda i: (i, 0)),
        out_shape=jax.ShapeDtypeStruct((m, d), BF16),
        compiler_params=_params("parallel"),
        name="rmsnorm_rows",
    )(x, g.reshape(1, d))


POST_ROWS = 64


def _out_proj_kernel(*refs, tn, with_next):
    if with_next:
        m_ref, w_ref, x_ref, g_ref, gn_ref, o_ref, hn_ref, x_rows, ssq = refs
    else:
        m_ref, w_ref, x_ref, g_ref, o_ref, x_rows, ssq = refs
    j = pl.program_id(1)
    cols = pl.ds(pl.multiple_of(j * tn, tn), tn)
    y = _dot(m_ref[...], w_ref[...])
    sq = jnp.sum(y * y, axis=-1, keepdims=True)
    o_ref[:, cols] = y * g_ref[...]
    x_rows[:, cols] = x_ref[...]

    @pl.when(j == 0)
    def _first():
        ssq[...] = sq

    @pl.when(j > 0)
    def _rest():
        ssq[...] += sq

    @pl.when(j == pl.num_programs(1) - 1)
    def _normalise():
        inv_d = 1.0 / o_ref.shape[1]
        for r0 in range(0, o_ref.shape[0], POST_ROWS):
            rows = slice(r0, r0 + POST_ROWS)
            scale = lax.rsqrt(ssq[rows, :] * inv_d + NORM_EPS)
            x_new = x_rows[rows, :] + o_ref[rows, :] * scale
            o_ref[rows, :] = x_new
            if with_next:
                ms = jnp.mean(x_new * x_new, axis=-1, keepdims=True)
                hn_ref[rows, :] = (x_new * lax.rsqrt(ms + NORM_EPS) * gn_ref[...]).astype(hn_ref.dtype)


def out_proj_norm_residual(merged, w, layer, x, g, g_next, tm, tn):
    m, d = x.shape
    with_next = g_next is not None
    row = pl.BlockSpec((tm, d), lambda i, j: (i, 0))
    gain = pl.BlockSpec((1, d), lambda i, j: (0, 0))
    out = pl.pallas_call(
        functools.partial(_out_proj_kernel, tn=tn, with_next=with_next),
        grid=(m // tm, d // tn),
        in_specs=[row, pl.BlockSpec((None, d, tn), lambda i, j: (layer, 0, j)),
                  pl.BlockSpec((tm, tn), lambda i, j: (i, j)), pl.BlockSpec((1, tn), lambda i, j: (0, j))]
                 + ([gain] if with_next else []),
        out_specs=[row, row] if with_next else [row],
        out_shape=[jax.ShapeDtypeStruct((m, d), F32)] + ([jax.ShapeDtypeStruct((m, d), BF16)] if with_next else []),
        scratch_shapes=[pltpu.VMEM((tm, d), F32), pltpu.VMEM((tm, 1), F32)],
        compiler_params=_params("parallel", "arbitrary"),
        name="out_proj_norm_residual",
    )(merged, w, x, g.reshape(1, d), *([g_next.reshape(1, d)] if with_next else []))
    return out if with_next else (out[0], None)


def _drop_cols_kernel(a_ref, b_ref, o_ref, *, first_shifted, gap):
    j = pl.program_id(2)

    @pl.when(j < first_shifted)
    def _before_gap():
        o_ref[...] = a_ref[...].astype(o_ref.dtype)

    @pl.when(j >= first_shifted)
    def _after_gap():
        o_ref[...] = jnp.concatenate([a_ref[:, gap:], b_ref[...]], axis=1).astype(o_ref.dtype)


def drop_cols_cast(w, lo, hi, cw, tk):
    depth, k, n = w.shape
    gap = hi - lo
    n_out = n - gap
    return pl.pallas_call(
        functools.partial(_drop_cols_kernel, first_shifted=lo // cw, gap=gap),
        grid=(depth, k // tk, n_out // cw),
        in_specs=[pl.BlockSpec((None, tk, cw), lambda l, i, j: (l, i, j)),
                  pl.BlockSpec((None, tk, gap), lambda l, i, j: (l, i, (j + 1) * (cw // gap)))],
        out_specs=pl.BlockSpec((None, tk, cw), lambda l, i, j: (l, i, j)),
        out_shape=jax.ShapeDtypeStruct((depth, k, n_out), BF16),
        compiler_params=_params("parallel", "parallel", "parallel"),
        name="drop_cols_cast",
    )(w, w)
def _matmul_kernel(x_ref, w_ref, o_ref):
    o_ref[...] = _dot(x_ref[...], w_ref[...]).astype(o_ref.dtype)


def matmul(x, w, layer, tm, tn, out_dtype=F32):
    m, k = x.shape
    n = w.shape[2]
    return pl.pallas_call(
        _matmul_kernel,
        grid=(n // tn, m // tm),
        in_specs=[pl.BlockSpec((tm, k), lambda j, i: (i, 0)),
                  pl.BlockSpec((None, k, tn), lambda j, i: (layer, 0, j))],
        out_specs=pl.BlockSpec((tm, tn), lambda j, i: (i, j)),
        out_shape=jax.ShapeDtypeStruct((m, n), out_dtype),
        compiler_params=_params("parallel", "parallel"),
        name="matmul",
    )(x, w)


def _merge_kernel(h_ref, oa_ref, ob_ref, oc_ref, om_ref, wg_ref, bg_ref, wbr_ref, out_ref):
    h = h_ref[...]
    acc = None
    for n, o_ref in enumerate((oa_ref, ob_ref, oc_ref, om_ref)):
        gate = jax.nn.sigmoid(_dot(h, wg_ref[n]) + bg_ref[n])
        term = gate * _dot(o_ref[...], wbr_ref[n])
        acc = term if acc is None else acc + term
    out_ref[...] = acc.astype(out_ref.dtype)


def gated_merge(h, branches, wg, bg, wbr, layer, tm, tn):
    m, d = h.shape
    _, nb, w, _ = wbr.shape
    resident = dict(pipeline_mode=pl.Buffered(1)) if m > tm else {}
    o_spec = pl.BlockSpec((tm, w), lambda j, i: (i, 0))
    return pl.pallas_call(
        _merge_kernel,
        grid=(d // tn, m // tm),
        in_specs=[pl.BlockSpec((tm, d), lambda j, i: (i, 0)), o_spec, o_spec, o_spec, o_spec,
                  pl.BlockSpec((None, nb, d, tn), lambda j, i: (layer, 0, 0, j), **resident),
                  pl.BlockSpec((nb, 1, tn), lambda j, i: (0, 0, j)),
                  pl.BlockSpec((None, nb, w, tn), lambda j, i: (layer, 0, 0, j), **resident)],
        out_specs=pl.BlockSpec((tm, tn), lambda j, i: (i, j)),
        out_shape=jax.ShapeDtypeStruct((m, d), BF16),
        compiler_params=_params("parallel", "parallel"),
        name="gated_merge",
    )(h, *branches, wg, bg.reshape(nb, 1, d), wbr)


def _memattn_kernel(q_ref, mk_ref, mv_ref, o_ref, *, heads):
    scale = 1.0 / math.sqrt(HEAD_M)
    for hd in range(heads):
        sl = slice(hd * HEAD_M, (hd + 1) * HEAD_M)
        q = q_ref[:, sl].astype(BF16)
        s = _dot_nt(q, mk_ref[0, :, sl]) * scale
        s = s - jnp.max(s, axis=-1, keepdims=True)
        p = jnp.exp(s)
        p = p * (1.0 / jnp.sum(p, axis=-1, keepdims=True))
        o_ref[:, sl] = _dot(p.astype(BF16), mv_ref[0, :, sl]).astype(o_ref.dtype)


def memory_attention(proj, q_col, mk, mv, nseq, t, tq):
    n_mem, w = mk.shape[1:]
    per_seq = t // tq
    mem_spec = pl.BlockSpec((1, n_mem, w), lambda i: (i // per_seq, 0, 0))
    return pl.pallas_call(
        functools.partial(_memattn_kernel, heads=w // HEAD_M),
        grid=(nseq * per_seq,),
        in_specs=[pl.BlockSpec((tq, w), lambda i: (i, q_col)), mem_spec, mem_spec],
        out_specs=pl.BlockSpec((tq, w), lambda i: (i, 0)),
        out_shape=jax.ShapeDtypeStruct((nseq * t, w), BF16),
        compiler_params=_params("parallel"),
        name="memory_attention",
    )(proj, mk, mv)


def _cmlp_kernel(u_ref, v_ref, z_ref, ws_ref, bs_ref, lnw_ref, lnb_ref, o_ref, *maybe_vn_ref, clen, chunks):
    v = v_ref[...]
    mu = jnp.mean(v, axis=-1, keepdims=True)
    var = jnp.mean(jnp.square(v - mu), axis=-1, keepdims=True)
    vn = (v - mu) * lax.rsqrt(var + NORM_EPS) * lnw_ref[...] + lnb_ref[...]
    for vn_ref in maybe_vn_ref:
        vn_ref[...] = vn
    vn16 = vn.astype(BF16)
    groups = vn.shape[1] // GROUP_C
    row = lax.broadcasted_iota(jnp.int32, (clen, clen), 0)
    col = lax.broadcasted_iota(jnp.int32, (clen, clen), 1)
    for g in range(groups):
        wm = jnp.where(row >= col, ws_ref[g], 0.0).astype(BF16)
        cs = slice(g * GROUP_C, (g + 1) * GROUP_C)
        for c in range(chunks):
            rs = slice(c * clen, (c + 1) * clen)
            s = _dot(wm, vn16[rs, cs]) + bs_ref[:, cs]
            o_ref[rs, cs] = (u_ref[rs, cs] * s * jax.nn.silu(z_ref[rs, cs])).astype(o_ref.dtype)


def chunk_mlp(proj, u_col, ws, bs_full, ln_w, ln_b, rows, clen, chunks, want_vn):
    w = ln_w.shape[0]
    tm = clen * chunks
    col = lambda c: pl.BlockSpec((tm, w), lambda i: (i, c))
    const2 = lambda a: pl.BlockSpec(a.shape, lambda i: (0, 0))
    out = pl.BlockSpec((tm, w), lambda i: (i, 0))
    out_dtypes = [BF16, F32] if want_vn else [BF16]
    return pl.pallas_call(
        functools.partial(_cmlp_kernel, clen=clen, chunks=chunks),
        grid=(rows // tm,),
        in_specs=[col(u_col), col(u_col + 1), col(u_col + 2),
                  pl.BlockSpec(ws.shape, lambda i: (0, 0, 0)), const2(bs_full),
                  pl.BlockSpec((1, w), lambda i: (0, 0)), pl.BlockSpec((1, w), lambda i: (0, 0))],
        out_specs=[out] * len(out_dtypes),
        out_shape=[jax.ShapeDtypeStruct((rows, w), dt) for dt in out_dtypes],
        compiler_params=_params("parallel"),
        name="chunk_mlp",
    )(proj, proj, proj, ws, bs_full, ln_w.reshape(1, w), ln_b.reshape(1, w))


SB_HEADS_PER_STEP = 2
SB_BLOCK = 256


def _sb_group(q16, k16, v16, tk, u, masks, carry):
    tq = q16.shape[0]
    nb = k16.shape[0] // tk
    z_all = _dot_nt(q16, k16)
    nl, log_beta = [], []
    for j in range(nb):
        z2 = z_all[:, j * tk:(j + 1) * tk] * (math.log2(math.e) / math.sqrt(HEAD_B))
        neg_abs = lax.bitcast_convert_type(lax.bitcast_convert_type(z2, jnp.uint32) | jnp.uint32(0x80000000), F32)
        nl_j = jnp.maximum(z2, 0.0) + jnp.log2(1.0 + jnp.exp2(neg_abs))
        log_beta.append(z2 - nl_j)
        nl.append(jnp.where(masks[j], nl_j, 0.0) if j in masks else nl_j)
    lhs = jnp.concatenate([jnp.concatenate(_split2(nl_j), axis=1) for nl_j in nl], axis=0)
    cs = _dot_row_halves(_dot, lhs, u)
    att = [None] * nb
    for j in reversed(range(nb)):
        cs_j = cs[j * tq:(j + 1) * tq]
        att_j = jnp.exp2(log_beta[j] - cs_j - carry)
        att[j] = (jnp.where(masks[j], att_j, 0.0) if j in masks else att_j).astype(BF16)
        carry = carry + (cs_j[:, 0:1] + nl[j][:, 0:1])
    return _dot_row_halves(_dot, jnp.concatenate(att, axis=1), v16), carry


def _head_slices(ref):
    return [slice(i * HEAD_B, (i + 1) * HEAD_B) for i in range(ref.shape[1] // HEAD_B)]


def _sb_prompt_kernel(*refs, tq, nq):
    q_ref, z_ref, k_ref, v_ref, u_ref = refs[:5]
    o_ref, kout_ref, vout_ref = refs[-3:]
    kout_ref[...] = k_ref[...]
    vout_ref[...] = v_ref[...]
    row = lax.broadcasted_iota(jnp.int32, (tq, tq), 0)
    col = lax.broadcasted_iota(jnp.int32, (tq, tq), 1)
    diagonal = col < row
    k16 = k_ref[...].astype(BF16)
    v16 = v_ref[...].astype(BF16)
    for qi in range(nq):
        rows = slice(qi * tq, (qi + 1) * tq)
        n_keys = (qi + 1) * tq
        for s in _head_slices(q_ref):
            out, _ = _sb_group(q_ref[rows, s].astype(BF16), k16[0:n_keys, s], v16[0:n_keys, s], tq, u_ref[...],
                               {qi: diagonal}, jnp.zeros((tq, 1), F32))
            o_ref[rows, s] = (out * jax.nn.silu(z_ref[rows, s])).astype(o_ref.dtype)


def _sb_sample_kernel(*refs, tk):
    q_ref, z_ref, kn_ref, vn_ref, kp_ref, vp_ref, ud_ref, uf_ref = refs[:8]
    o_ref, kout_ref, vout_ref = refs[-3:]
    kout_ref[...] = kn_ref[...]
    vout_ref[...] = vn_ref[...]
    tq = q_ref.shape[0]
    row = lax.broadcasted_iota(jnp.int32, (tq, tq), 0)
    col = lax.broadcasted_iota(jnp.int32, (tq, tq), 1)
    heads = _head_slices(q_ref)
    n_past = kp_ref.shape[0] // len(heads)
    for hd, s in enumerate(heads):
        q16 = q_ref[:, s].astype(BF16)
        out_new, carry = _sb_group(q16, kn_ref[:, s].astype(BF16), vn_ref[:, s].astype(BF16), tq, ud_ref[...],
                                   {0: col < row}, jnp.zeros((tq, 1), F32))
        own_rows = pl.ds(hd, n_past, stride=len(heads))
        out_past, _ = _sb_group(q16, kp_ref[own_rows, :].astype(BF16), vp_ref[own_rows, :].astype(BF16), tk,
                                uf_ref[...], {}, carry)
        o_ref[:, s] = ((out_new + out_past) * jax.nn.silu(z_ref[:, s])).astype(o_ref.dtype)


def _suffix_matrix(tk):
    s_later = lax.broadcasted_iota(jnp.int32, (tk, tk), 0)
    s_here = lax.broadcasted_iota(jnp.int32, (tk, tk), 1)
    u = (s_later > s_here).astype(BF16)
    return jnp.concatenate([u, u], axis=0)


def stick_breaking(proj, q_col, z_col, k_col, v_col, past_k, past_v, nseq, t, heads, layer, depth, kv_out):
    hp = SB_HEADS_PER_STEP if past_k is None else heads
    wide = hp * HEAD_B
    tq = min(SB_BLOCK, t)
    seq = lambda rows, col: pl.BlockSpec((rows, wide), lambda b, h: (b, col // hp + h))
    const = lambda a: pl.BlockSpec(a.shape, lambda b, h: (0, 0))
    if past_k is None:
        u = _suffix_matrix(tq)
        body = functools.partial(_sb_prompt_kernel, tq=tq, nq=t // tq)
        in_specs = [seq(t, q_col), seq(t, z_col), seq(t, k_col), seq(t, v_col), const(u)]
        args = [proj, proj, proj, proj, u]
    else:
        assert t == tq
        n_past = past_k.shape[2]
        past = pl.BlockSpec((None, None, n_past * heads, HEAD_B), lambda b, h: (layer, b, 0, 0))
        past_k, past_v = (a.reshape(depth, nseq, n_past * heads, HEAD_B) for a in (past_k, past_v))
        ud, uf = _suffix_matrix(tq), _suffix_matrix(SB_BLOCK)
        body = functools.partial(_sb_sample_kernel, tk=SB_BLOCK)
        in_specs = [seq(t, q_col), seq(t, z_col), seq(t, k_col), seq(t, v_col), past, past, const(ud), const(uf)]
        args = [proj, proj, proj, proj, past_k, past_v, ud, uf]
    aliases = {}
    if kv_out is not None:
        aliases = {len(args): 1, len(args) + 1: 2}
        in_specs = in_specs + [pl.BlockSpec(memory_space=pl.ANY)] * 2
        args = args + list(kv_out)
    kv_shape = jax.ShapeDtypeStruct((depth, nseq * t, heads * HEAD_B), F32)
    kv_spec = pl.BlockSpec((None, t, wide), lambda b, h: (layer, b, h))
    o_b, k_all, v_all = pl.pallas_call(
        body,
        grid=(nseq, heads // hp),
        in_specs=in_specs,
        out_specs=[seq(t, 0), kv_spec, kv_spec],
        out_shape=[jax.ShapeDtypeStruct((nseq * t, heads * HEAD_B), BF16), kv_shape, kv_shape],
        input_output_aliases=aliases,
        compiler_params=_params("parallel", "parallel"),
        name="stick_breaking",
    )(*args)
    return o_b, (k_all, v_all)


def _block_stack(x, lane_masks):
    return jnp.concatenate([jnp.where(m, x, 0.0) for m in lane_masks], axis=0)


def _rwkv_kernel(x_ref, lora_ref, z_ref, sh_main_ref, sh_lora_ref, s0_ref,
                 mu_main_ref, mu_lora_ref, w0_ref, wup_ref, a0_ref, aup_ref, kk_ref, ka_ref, rk_ref,
                 gnw_ref, gnb_ref, e_ref, bd_ref, ltri_ref,
                 o_ref, sout_ref, shm_out_ref, shl_out_ref,
                 xbuf, lbuf, sbd, *, chunk, width):
    c = pl.program_id(1)
    n_chunks = pl.num_programs(1)
    C, W = chunk, width
    n_seq = x_ref.shape[0]
    n_slabs = W // SLAB
    n_heads = W // HEAD_A

    @pl.when(c == 0)
    def _init():
        sbd[...] = jnp.zeros_like(sbd)
        for s in range(n_seq):
            xbuf[s, 7:8, :] = sh_main_ref[s]
            lbuf[s, 7:8, :] = sh_lora_ref[s]
            for hd in range(n_heads):
                g, j = divmod(hd, GROUP_A)
                sbd[s, g, j * HEAD_A:(j + 1) * HEAD_A, j * HEAD_A:(j + 1) * HEAD_A] = s0_ref[s, hd]

    def shifted(ref, buf, mu_ref):
        rows = []
        for s in range(n_seq):
            x = ref[s]
            buf[s, 8:8 + C, :] = x
            rows.append(x + mu_ref[...] * (buf[s, 7:7 + C, :] - x))
            buf[s, 7:8, :] = x[C - 1:C, :]
        return jnp.concatenate(rows, axis=0)

    xs = shifted(x_ref, xbuf, mu_main_ref)
    lo_s = shifted(lora_ref, lbuf, mu_lora_ref)
    seq_rows = [slice(s * C, (s + 1) * C) for s in range(n_seq)]
    slabs = [slice(g * SLAB, (g + 1) * SLAB) for g in range(n_slabs)]

    r, k, v = xs[:, :W], xs[:, W:2 * W], xs[:, 2 * W:]
    w_pre = w0_ref[...] + _dot(jnp.tanh(lo_s).astype(BF16), wup_ref[...])
    ld = -math.exp(-0.5) * jax.nn.sigmoid(w_pre)
    a = jax.nn.sigmoid(a0_ref[...] + _dot(lo_s.astype(BF16), aup_ref[...]))

    e_mat = e_ref[...]

    segs = [slice(g * SEG_LANES, (g + 1) * SEG_LANES) for g in range(W // SEG_LANES)]

    def seg_sum(val):
        n = val.shape[0]
        hi, lo = _split2(val)
        out = _dot(jnp.concatenate([part[:, sl] for part in (hi, lo) for sl in segs], axis=0), e_mat)
        return jnp.concatenate([out[g * n:(g + 1) * n] + out[(len(segs) + g) * n:(len(segs) + g + 1) * n]
                                for g in range(len(segs))], axis=1)

    kk = k * kk_ref[...]
    kk = kk * lax.rsqrt(jnp.maximum(seg_sum(kk * kk), 1e-24))
    kmod = k * (1.0 + (a - 1.0) * ka_ref[...])

    ld_hi = ld.astype(BF16)
    ld_r1 = ld - ld_hi.astype(F32)
    ld_mid = ld_r1.astype(BF16)
    ld_lo = (ld_r1 - ld_mid.astype(F32)).astype(BF16)
    ltri = ltri_ref[...]
    lp = _dot(ltri, ld_hi) + _dot(ltri, ld_mid) + _dot(ltri, ld_lo)
    lp_last = [lp[rs.stop - 1:rs.stop, :] for rs in seq_rows]
    lp_end = jnp.concatenate([jnp.broadcast_to(row, (C, W)) for row in lp_last], axis=0)
    e_neg = jnp.exp(-lp)
    kka = kk * a
    kap = kk * jnp.exp(lp - ld)
    bet = kka * e_neg
    kt = kmod * e_neg
    rt = r * jnp.exp(lp)
    e_end = jnp.exp(lp_end - lp)
    kt_end = kmod * e_end
    bet_end = kka * e_end
    dec_end = [jnp.exp(row) for row in lp_last]

    lane = lax.broadcasted_iota(jnp.int32, (1, SLAB), 1)
    lane_masks = [(lane >= j * HEAD_A) & (lane < (j + 1) * HEAD_A) for j in range(GROUP_A)]
    t_row = lax.broadcasted_iota(jnp.int32, (C, GROUP_A * C), 0)
    s_col = lax.broadcasted_iota(jnp.int32, (C, GROUP_A * C), 1) & (C - 1)
    strict, incl = s_col < t_row, s_col <= t_row
    stack = lambda val16: _block_stack(val16, lane_masks)
    nh = GROUP_A * C
    units = [(s, g) for s in range(n_seq) for g in range(n_slabs)]
    per_slab = lambda fn: [fn(u, seq_rows[s], slabs[g]) for u, (s, g) in enumerate(units)]

    kap16, rt16, bet16, kt16, v16 = (val.astype(BF16) for val in (kap, rt, bet, kt, v))
    s_old = [sbd[s, g] for s, g in units]
    lhs = per_slab(lambda g, rs, sl: jnp.concatenate([kap16[rs, sl], rt16[rs, sl]], axis=0))
    rhs = per_slab(lambda g, rs, sl: jnp.concatenate([stack(bet16[rs, sl]), stack(kt16[rs, sl])], axis=0))
    sc = per_slab(lambda g, rs, sl: _dot_nt(lhs[g], rhs[g]))
    ls = per_slab(lambda g, rs, sl: _dot_nt(lhs[g], s_old[g].astype(BF16)))
    v_stack = per_slab(lambda g, rs, sl: stack(v16[rs, sl]))
    p16 = per_slab(lambda g, rs, sl: jnp.where(strict, -sc[g][:C, :nh], 0.0).astype(BF16))
    xw = per_slab(lambda g, rs, sl: ls[g][:C]
                  + _dot(jnp.where(strict, sc[g][:C, nh:], 0.0).astype(BF16), v_stack[g]))
    for step in range(6):
        xw = per_slab(lambda g, rs, sl: xw[g] + _dot(p16[g], stack(xw[g].astype(BF16))))
        if step < 5:
            p16 = per_slab(lambda g, rs, sl: _dot(p16[g], stack(p16[g])).astype(BF16))
    ab_inc = per_slab(lambda g, rs, sl: jnp.concatenate(
        [jnp.where(incl, sc[g][C:, :nh], 0.0), jnp.where(incl, sc[g][C:, nh:], 0.0)], axis=1).astype(BF16))
    y = per_slab(lambda g, rs, sl: ls[g][C:] + _dot(
        ab_inc[g], jnp.concatenate([stack((-xw[g]).astype(BF16)), v_stack[g]], axis=0)))
    upd = per_slab(lambda g, rs, sl: _dot(
        jnp.concatenate([v[rs, sl], -xw[g]], axis=0).T.astype(BF16),
        jnp.concatenate([kt_end[rs, sl], bet_end[rs, sl]], axis=0).astype(BF16)))
    for u, (s, g) in enumerate(units):
        sbd[s, g] = s_old[u] * dec_end[s][:, slabs[g]] + upd[u] * bd_ref[...]

    y = jnp.concatenate([jnp.concatenate(y[s * n_slabs:(s + 1) * n_slabs], axis=1) for s in range(n_seq)], axis=0)
    inv_n = 1.0 / HEAD_A
    mean = seg_sum(y) * inv_n
    yc = y - mean
    var = seg_sum(yc * yc) * inv_n
    y = yc * lax.rsqrt(var + GN_EPS) * gnw_ref[...] + gnb_ref[...]
    y = y + seg_sum(r * kmod * rk_ref[...]) * v
    for s, rs in enumerate(seq_rows):
        o_ref[s] = (y[rs] * jax.nn.silu(z_ref[s])).astype(o_ref.dtype)

    @pl.when(c == n_chunks - 1)
    def _fin():
        for s in range(n_seq):
            shm_out_ref[s] = xbuf[s, 7:8, :]
            shl_out_ref[s] = lbuf[s, 7:8, :]
            for hd in range(n_heads):
                g, j = divmod(hd, GROUP_A)
                sout_ref[s, hd] = sbd[s, g, j * HEAD_A:(j + 1) * HEAD_A, j * HEAD_A:(j + 1) * HEAD_A]


RWKV_SEQS_PER_STEP = 2


def rwkv7(proj, lora, z_col, shift_main, shift_lora, s0, prm, nseq, t):
    W = prm["a_w0"].shape[0]
    C = RWKV_CHUNK
    S = RWKV_SEQS_PER_STEP
    nc = t // C
    n_heads = W // HEAD_A
    lw = lora.shape[1]
    half = lw // 2
    zpad = jnp.zeros((half, W), F32)
    wup = jnp.concatenate([prm["a_w_up"], zpad], axis=0).astype(BF16)
    aup = jnp.concatenate([zpad, prm["a_a_up"]], axis=0).astype(BF16)
    same_head = lambda n: (jnp.arange(n)[:, None] // HEAD_A) == (jnp.arange(n)[None, :] // HEAD_A)
    ti = jnp.arange(S * C)
    ltri = ((ti[:, None] >= ti[None, :]) & (ti[:, None] // C == ti[None, :] // C)).astype(BF16)
    row1 = lambda a: a.reshape(1, -1)
    vec = lambda n: pl.BlockSpec((1, n), lambda b, c: (0, 0))
    full = lambda a: pl.BlockSpec(a.shape, lambda b, c: (0, 0))
    rows = lambda n, col: pl.BlockSpec((S, C, n), lambda b, c: (b, c, col))
    per_seq = lambda n: pl.BlockSpec((S, 1, n), lambda b, c: (b, 0, 0))
    state = pl.BlockSpec((S, n_heads, HEAD_A, HEAD_A), lambda b, c: (b, 0, 0, 0))
    e_mat, bd_mask = same_head(SEG_LANES).astype(BF16), same_head(SLAB).astype(F32)
    proj3 = proj.reshape(nseq, t, proj.shape[1])
    o_a, s_new, sh_main, sh_lora = pl.pallas_call(
        functools.partial(_rwkv_kernel, chunk=C, width=W),
        grid=(nseq // S, nc),
        in_specs=[rows(3 * W, 0), rows(lw, 0), rows(W, z_col),
                  per_seq(3 * W), per_seq(lw), state,
                  vec(3 * W), vec(lw), vec(W), full(wup), vec(W), full(aup), vec(W), vec(W), vec(W), vec(W), vec(W),
                  full(e_mat), full(bd_mask), full(ltri)],
        out_specs=[rows(W, 0), state, per_seq(3 * W), per_seq(lw)],
        out_shape=[jax.ShapeDtypeStruct((nseq, t, W), BF16),
                   jax.ShapeDtypeStruct((nseq, n_heads, HEAD_A, HEAD_A), F32),
                   jax.ShapeDtypeStruct((nseq, 1, 3 * W), F32),
                   jax.ShapeDtypeStruct((nseq, 1, lw), F32)],
        scratch_shapes=[pltpu.VMEM((S, C + 8, 3 * W), F32), pltpu.VMEM((S, C + 8, lw), F32),
                        pltpu.VMEM((S, W // SLAB, SLAB, SLAB), F32)],
        compiler_params=_params("parallel", "arbitrary"),
        name="rwkv7",
    )(proj3, lora.reshape(nseq, t, lw), proj3, shift_main, shift_lora, s0,
      row1(prm["a_mu"][:3 * W]), row1(prm["a_mu"][3 * W:]), row1(prm["a_w0"]), wup, row1(prm["a_a0"]), aup,
      row1(prm["a_k_k"]), row1(prm["a_k_a"]), row1(prm["a_r_k"]), row1(prm["a_gn_w"]), row1(prm["a_gn_b"]),
      e_mat, bd_mask, ltri)
    return o_a.reshape(nseq * t, W), s_new, sh_main, sh_lora


COL_AZ, COL_BQ, COL_BK, COL_BV, COL_BZ, COL_CU, COL_MQ = 3, 4, 5, 6, 7, 8, 11


def trunk_layer(x, h, nseq, t, mk, mv, shift_prev, s_prev, past_k, past_v, prm, layer, depth, kv_out, want_vn):
    m, d = x.shape
    W = d // 4
    tm = min(512, m)
    if h is None:
        h = rmsnorm_rows(x, prm["g_pre"], min(256, m))
    proj = matmul(h, prm["w_in_main"], layer, min(1024, m), 1024)
    lora = matmul(h, prm["w_in_lora"], layer, min(1024, m), prm["w_in_lora"].shape[2])

    o_a, s_new, sh_main, sh_lora = rwkv7(proj, lora, COL_AZ, shift_prev[..., :3 * W], shift_prev[..., 3 * W:],
                                         s_prev, prm, nseq, t)
    shift_new = jnp.concatenate([sh_main, sh_lora], axis=-1)

    heads_b = W // HEAD_B
    hb = lambda col: col * heads_b
    o_b, kv_out = stick_breaking(proj, hb(COL_BQ), hb(COL_BZ), hb(COL_BK), hb(COL_BV), past_k, past_v,
                                 nseq, t, heads_b, layer, depth, kv_out)

    clen = min(t, prm["c_ws"].shape[1])
    groups = prm["c_ws"].shape[0]
    bs_full = jnp.repeat(prm["c_bs"][:, :clen].T, W // groups, axis=1)
    o_c, *vn_c = chunk_mlp(proj, COL_CU, prm["c_ws"][:, :clen, :clen], bs_full, prm["c_ln_w"], prm["c_ln_b"],
                           m, clen, max(1, min(512, t) // clen), want_vn)

    o_m = memory_attention(proj, COL_MQ, mk, mv, nseq, t, min(512, t))

    merged = gated_merge(h, (o_a, o_b, o_c, o_m), prm["w_gate"], prm["b_gate"], prm["w_br"], layer, tm,
                         512 if m > tm else 256)
    x_new, h_next = out_proj_norm_residual(merged, prm["w_out"], layer, x, prm["g_post"], prm["g_pre_next"], tm, 512)
    return x_new, h_next, shift_new, s_new, kv_out, vn_c


def kernel(x_prompt, x_sample, cache_mem_k, cache_mem_v, cache_sb_k, cache_sb_v, state_rwkv, state_shift, mem_prompt, g_pre, g_post, w_in, a_mu, a_w0, a_w_up, a_a0, a_a_up, a_k_k, a_k_a, a_r_k, a_gn_w, a_gn_b, c_ws, c_bs, c_ln_w, c_ln_b, g_mem, w_mem_kv, w_gate, b_gate, w_br, w_out):
    bp, tp, d = x_prompt.shape
    bs_, ts, _ = x_sample.shape
    depth = w_in.shape[0]
    W = d // 4
    n_mem = mem_prompt.shape[1]
    m_heads = cache_mem_k.shape[3]
    b_heads = cache_sb_k.shape[3]
    a_heads = state_rwkv.shape[2]
    shift_w = state_shift.shape[-1]
    n_past = cache_sb_k.shape[2]
    lora_lo, lora_hi = 3 * W, shift_w

    yp = x_prompt.reshape(bp * tp, d)
    ys = x_sample.reshape(bs_ * ts, d)
    mem_rows = mem_prompt.reshape(bp * n_mem, d)
    mem_out, rw_p, sh_p, rw_s, sh_s, cv_s = [[], []], [], [], [], [], []
    kv_p = kv_s = hp = hs = None
    dense = {
        "w_in_main": drop_cols_cast(w_in, lora_lo, lora_hi, W, 1024),
        "w_in_lora": w_in[:, :, lora_lo:lora_hi].astype(BF16),
        "w_gate": w_gate.astype(BF16), "w_br": w_br.astype(BF16), "w_out": w_out.astype(BF16),
    }
    w_mem16 = w_mem_kv.astype(BF16)
    for l in range(depth):
        prm = dict(dense)
        prm.update({
            "g_pre": g_pre[l], "g_post": g_post[l], "g_pre_next": g_pre[l + 1] if l + 1 < depth else None,
            "a_mu": a_mu[l], "a_w0": a_w0[l], "a_w_up": a_w_up[l], "a_a0": a_a0[l], "a_a_up": a_a_up[l],
            "a_k_k": a_k_k[l].reshape(-1), "a_k_a": a_k_a[l].reshape(-1), "a_r_k": a_r_k[l].reshape(-1),
            "a_gn_w": a_gn_w[l].reshape(-1), "a_gn_b": a_gn_b[l].reshape(-1),
            "c_ws": c_ws[l], "c_bs": c_bs[l], "c_ln_w": c_ln_w[l], "c_ln_b": c_ln_b[l], "b_gate": b_gate[l],
        })
        kv = matmul(rmsnorm_rows(mem_rows, g_mem[l], 256), w_mem16, l, min(1024, bp * n_mem), 1024)
        mk = kv[:, :W].reshape(bp, n_mem, W)
        mv = kv[:, W:].reshape(bp, n_mem, W)
        shift0 = jnp.zeros((bp, 1, shift_w), F32)
        s0 = jnp.zeros((bp, a_heads, HEAD_A, HEAD_A), F32)
        yp, hp, sh, st, kv_p, _ = trunk_layer(yp, hp, bp, tp, mk.astype(BF16), mv.astype(BF16), shift0, s0, None, None,
                                              prm, l, depth, kv_p, False)
        mem_out[0].append(mk.reshape(bp, n_mem, m_heads, W // m_heads))
        mem_out[1].append(mv.reshape(bp, n_mem, m_heads, W // m_heads))
        rw_p.append(st)
        sh_p.append(sh)
        ys, hs, sh, st, kv_s, cvn = trunk_layer(
            ys, hs, bs_, ts, cache_mem_k[l].reshape(bs_, n_mem, W).astype(BF16),
            cache_mem_v[l].reshape(bs_, n_mem, W).astype(BF16), state_shift[l], state_rwkv[l],
            cache_sb_k, cache_sb_v, prm, l, depth, kv_s, True)
        rw_s.append(st)
        sh_s.append(sh)
        cv_s.append(cvn[0].reshape(bs_, ts, W))
    heads_p = lambda a: a.reshape(depth, bp, tp, b_heads, HEAD_B)
    heads_s = lambda a: a.reshape(depth, bs_, ts, b_heads, HEAD_B)
    return (yp.reshape(bp, tp, d), ys.reshape(bs_, ts, d), jnp.stack(mem_out[0]), jnp.stack(mem_out[1]),
            heads_p(kv_p[0]), heads_p(kv_p[1]), jnp.stack(rw_p), jnp.stack(sh_p),
            heads_s(kv_s[0]), heads_s(kv_s[1]), jnp.stack(rw_s), jnp.stack(sh_s), jnp.stack(cv_s))
```

```python
import functools
import math

import jax
import jax.numpy as jnp
from jax import lax
from jax.experimental import pallas as pl
from jax.experimental.pallas import tpu as pltpu

F32 = jnp.float32
BF16 = jnp.bfloat16

NORM_EPS = 1e-6
GN_EPS = 64e-5
HEAD_A = 64
GROUP_A = 2
SLAB = HEAD_A * GROUP_A
SEG_LANES = 256
HEAD_B = 128
HEAD_M = 256
GROUP_C = 128
RWKV_CHUNK = 64
VMEM_LIMIT_BYTES = 56 * 1024 * 1024


def _params(*sem):
    return pltpu.CompilerParams(dimension_semantics=sem, vmem_limit_bytes=VMEM_LIMIT_BYTES)


def _split2(x):
    hi = x.astype(BF16)
    lo = (x - hi.astype(F32)).astype(BF16)
    return hi, lo


def _dot(a, b):
    return jnp.dot(a, b, preferred_element_type=F32)


def _dot_nt(a, b):
    return lax.dot_general(a, b, (((1,), (1,)), ((), ())), preferred_element_type=F32)


def _dot_row_halves(dot, a, b):
    half = a.shape[0] // 2
    return jnp.concatenate([dot(a[:half], b), dot(a[half:], b)], axis=0)


def _rmsnorm_kernel(x_ref, g_ref, o_ref):
    x = x_ref[...]
    ms = jnp.mean(x * x, axis=-1, keepdims=True)
    o_ref[...] = (x * lax.rsqrt(ms + NORM_EPS) * g_ref[...]).astype(o_ref.dtype)


def rmsnorm_rows(x, g, tm):
    m, d = x.shape
    return pl.pallas_call(
        _rmsnorm_kernel,
        grid=(m // tm,),
        in_specs=[pl.BlockSpec((tm, d), lambda i: (i, 0)), pl.BlockSpec((1, d), lambda i: (0, 0))],
        out_specs=pl.BlockSpec((tm, d), lambda i: (i, 0)),
        out_shape=jax.ShapeDtypeStruct((m, d), BF16),
        compiler_params=_params("parallel"),
        name="rmsnorm_rows",
    )(x, g.reshape(1, d))


POST_ROWS = 64


def _out_proj_kernel(*refs, tn, with_next):
    if with_next:
        m_ref, w_ref, x_ref, g_ref, gn_ref, o_ref, hn_ref, x_rows, ssq = refs
    else:
        m_ref, w_ref, x_ref, g_ref, o_ref, x_rows, ssq = refs
    j = pl.program_id(1)
    cols = pl.ds(pl.multiple_of(j * tn, tn), tn)
    y = _dot(m_ref[...], w_ref[...])
    sq = jnp.sum(y * y, axis=-1, keepdims=True)
    o_ref[:, cols] = y * g_ref[...]
    x_rows[:, cols] = x_ref[...]

    @pl.when(j == 0)
    def _first():
        ssq[...] = sq

    @pl.when(j > 0)
    def _rest():
        ssq[...] += sq

    @pl.when(j == pl.num_programs(1) - 1)
    def _normalise():
        inv_d = 1.0 / o_ref.shape[1]
        for r0 in range(0, o_ref.shape[0], POST_ROWS):
            rows = slice(r0, r0 + POST_ROWS)
            scale = lax.rsqrt(ssq[rows, :] * inv_d + NORM_EPS)
            x_new = x_rows[rows, :] + o_ref[rows, :] * scale
            o_ref[rows, :] = x_new
            if with_next:
                ms = jnp.mean(x_new * x_new, axis=-1, keepdims=True)
                hn_ref[rows, :] = (x_new * lax.rsqrt(ms + NORM_EPS) * gn_ref[...]).astype(hn_ref.dtype)


def out_proj_norm_residual(merged, w, layer, x, g, g_next, tm, tn):
    m, d = x.shape
    with_next = g_next is not None
    row = pl.BlockSpec((tm, d), lambda i, j: (i, 0))
    gain = pl.BlockSpec((1, d), lambda i, j: (0, 0))
    out = pl.pallas_call(
        functools.partial(_out_proj_kernel, tn=tn, with_next=with_next),
        grid=(m // tm, d // tn),
        in_specs=[row, pl.BlockSpec((None, d, tn), lambda i, j: (layer, 0, j)),
                  pl.BlockSpec((tm, tn), lambda i, j: (i, j)), pl.BlockSpec((1, tn), lambda i, j: (0, j))]
                 + ([gain] if with_next else []),
        out_specs=[row, row] if with_next else [row],
        out_shape=[jax.ShapeDtypeStruct((m, d), F32)] + ([jax.ShapeDtypeStruct((m, d), BF16)] if with_next else []),
        scratch_shapes=[pltpu.VMEM((tm, d), F32), pltpu.VMEM((tm, 1), F32)],
        compiler_params=_params("parallel", "arbitrary"),
        name="out_proj_norm_residual",
    )(merged, w, x, g.reshape(1, d), *([g_next.reshape(1, d)] if with_next else []))
    return out if with_next else (out[0], None)


def _drop_cols_kernel(a_ref, b_ref, o_ref, *, first_shifted, gap):
    j = pl.program_id(2)

    @pl.when(j < first_shifted)
    def _before_gap():
        o_ref[...] = a_ref[...].astype(o_ref.dtype)

    @pl.when(j >= first_shifted)
    def _after_gap():
        o_ref[...] = jnp.concatenate([a_ref[:, gap:], b_ref[...]], axis=1).astype(o_ref.dtype)


def drop_cols_cast(w, lo, hi, cw, tk):
    depth, k, n = w.shape
    gap = hi - lo
    n_out = n - gap
    return pl.pallas_call(
        functools.partial(_drop_cols_kernel, first_shifted=lo // cw, gap=gap),
        grid=(depth, k // tk, n_out // cw),
        in_specs=[pl.BlockSpec((None, tk, cw), lambda l, i, j: (l, i, j)),
                  pl.BlockSpec((None, tk, gap), lambda l, i, j: (l, i, (j + 1) * (cw // gap)))],
        out_specs=pl.BlockSpec((None, tk, cw), lambda l, i, j: (l, i, j)),
        out_shape=jax.ShapeDtypeStruct((depth, k, n_out), BF16),
        compiler_params=_params("parallel", "parallel", "parallel"),
        name="drop_cols_cast",
    )(w, w)
def _matmul_kernel(x_ref, w_ref, o_ref):
    o_ref[...] = _dot(x_ref[...], w_ref[...]).astype(o_ref.dtype)


def matmul(x, w, layer, tm, tn, out_dtype=F32):
    m, k = x.shape
    n = w.shape[2]
    return pl.pallas_call(
        _matmul_kernel,
        grid=(n // tn, m // tm),
        in_specs=[pl.BlockSpec((tm, k), lambda j, i: (i, 0)),
                  pl.BlockSpec((None, k, tn), lambda j, i: (layer, 0, j))],
        out_specs=pl.BlockSpec((tm, tn), lambda j, i: (i, j)),
        out_shape=jax.ShapeDtypeStruct((m, n), out_dtype),
        compiler_params=_params("parallel", "parallel"),
        name="matmul",
    )(x, w)


def _merge_kernel(h_ref, oa_ref, ob_ref, oc_ref, om_ref, wg_ref, bg_ref, wbr_ref, out_ref):
    h = h_ref[...]
    acc = None
    for n, o_ref in enumerate((oa_ref, ob_ref, oc_ref, om_ref)):
        gate = jax.nn.sigmoid(_dot(h, wg_ref[n]) + bg_ref[n])
        term = gate * _dot(o_ref[...], wbr_ref[n])
        acc = term if acc is None else acc + term
    out_ref[...] = acc.astype(out_ref.dtype)


def gated_merge(h, branches, wg, bg, wbr, layer, tm, tn):
    m, d = h.shape
    _, nb, w, _ = wbr.shape
    resident = dict(pipeline_mode=pl.Buffered(1)) if m > tm else {}
    o_spec = pl.BlockSpec((tm, w), lambda j, i: (i, 0))
    return pl.pallas_call(
        _merge_kernel,
        grid=(d // tn, m // tm),
        in_specs=[pl.BlockSpec((tm, d), lambda j, i: (i, 0)), o_spec, o_spec, o_spec, o_spec,
                  pl.BlockSpec((None, nb, d, tn), lambda j, i: (layer, 0, 0, j), **resident),
                  pl.BlockSpec((nb, 1, tn), lambda j, i: (0, 0, j)),
                  pl.BlockSpec((None, nb, w, tn), lambda j, i: (layer, 0, 0, j), **resident)],
        out_specs=pl.BlockSpec((tm, tn), lambda j, i: (i, j)),
        out_shape=jax.ShapeDtypeStruct((m, d), BF16),
        compiler_params=_params("parallel", "parallel"),
        name="gated_merge",
    )(h, *branches, wg, bg.reshape(nb, 1, d), wbr)


def _memattn_kernel(q_ref, mk_ref, mv_ref, o_ref, *, heads):
    scale = 1.0 / math.sqrt(HEAD_M)
    for hd in range(heads):
        sl = slice(hd * HEAD_M, (hd + 1) * HEAD_M)
        q = q_ref[:, sl].astype(BF16)
        s = _dot_nt(q, mk_ref[0, :, sl]) * scale
        s = s - jnp.max(s, axis=-1, keepdims=True)
        p = jnp.exp(s)
        p = p * (1.0 / jnp.sum(p, axis=-1, keepdims=True))
        o_ref[:, sl] = _dot(p.astype(BF16), mv_ref[0, :, sl]).astype(o_ref.dtype)


def memory_attention(proj, q_col, mk, mv, nseq, t, tq):
    n_mem, w = mk.shape[1:]
    per_seq = t // tq
    mem_spec = pl.BlockSpec((1, n_mem, w), lambda i: (i // per_seq, 0, 0))
    return pl.pallas_call(
        functools.partial(_memattn_kernel, heads=w // HEAD_M),
        grid=(nseq * per_seq,),
        in_specs=[pl.BlockSpec((tq, w), lambda i: (i, q_col)), mem_spec, mem_spec],
        out_specs=pl.BlockSpec((tq, w), lambda i: (i, 0)),
        out_shape=jax.ShapeDtypeStruct((nseq * t, w), BF16),
        compiler_params=_params("parallel"),
        name="memory_attention",
    )(proj, mk, mv)


def _cmlp_kernel(u_ref, v_ref, z_ref, ws_ref, bs_ref, lnw_ref, lnb_ref, o_ref, *maybe_vn_ref, clen, chunks):
    v = v_ref[...]
    mu = jnp.mean(v, axis=-1, keepdims=True)
    var = jnp.mean(jnp.square(v - mu), axis=-1, keepdims=True)
    vn = (v - mu) * lax.rsqrt(var + NORM_EPS) * lnw_ref[...] + lnb_ref[...]
    for vn_ref in maybe_vn_ref:
        vn_ref[...] = vn
    vn16 = vn.astype(BF16)
    groups = vn.shape[1] // GROUP_C
    row = lax.broadcasted_iota(jnp.int32, (clen, clen), 0)
    col = lax.broadcasted_iota(jnp.int32, (clen, clen), 1)
    for g in range(groups):
        wm = jnp.where(row >= col, ws_ref[g], 0.0).astype(BF16)
        cs = slice(g * GROUP_C, (g + 1) * GROUP_C)
        for c in range(chunks):
            rs = slice(c * clen, (c + 1) * clen)
            s = _dot(wm, vn16[rs, cs]) + bs_ref[:, cs]
            o_ref[rs, cs] = (u_ref[rs, cs] * s * jax.nn.silu(z_ref[rs, cs])).astype(o_ref.dtype)


def chunk_mlp(proj, u_col, ws, bs_full, ln_w, ln_b, rows, clen, chunks, want_vn):
    w = ln_w.shape[0]
    tm = clen * chunks
    col = lambda c: pl.BlockSpec((tm, w), lambda i: (i, c))
    const2 = lambda a: pl.BlockSpec(a.shape, lambda i: (0, 0))
    out = pl.BlockSpec((tm, w), lambda i: (i, 0))
    out_dtypes = [BF16, F32] if want_vn else [BF16]
    return pl.pallas_call(
        functools.partial(_cmlp_kernel, clen=clen, chunks=chunks),
        grid=(rows // tm,),
        in_specs=[col(u_col), col(u_col + 1), col(u_col + 2),
                  pl.BlockSpec(ws.shape, lambda i: (0, 0, 0)), const2(bs_full),
                  pl.BlockSpec((1, w), lambda i: (0, 0)), pl.BlockSpec((1, w), lambda i: (0, 0))],
        out_specs=[out] * len(out_dtypes),
        out_shape=[jax.ShapeDtypeStruct((rows, w), dt) for dt in out_dtypes],
        compiler_params=_params("parallel"),
        name="chunk_mlp",
    )(proj, proj, proj, ws, bs_full, ln_w.reshape(1, w), ln_b.reshape(1, w))


SB_HEADS_PER_STEP = 2
SB_BLOCK = 256


def _sb_group(q16, k16, v16, tk, u, masks, carry):
    tq = q16.shape[0]
    nb = k16.shape[0] // tk
    z_all = _dot_nt(q16, k16)
    nl, log_beta = [], []
    for j in range(nb):
        z2 = z_all[:, j * tk:(j + 1) * tk] * (math.log2(math.e) / math.sqrt(HEAD_B))
        neg_abs = lax.bitcast_convert_type(lax.bitcast_convert_type(z2, jnp.uint32) | jnp.uint32(0x80000000), F32)
        nl_j = jnp.maximum(z2, 0.0) + jnp.log2(1.0 + jnp.exp2(neg_abs))
        log_beta.append(z2 - nl_j)
        nl.append(jnp.where(masks[j], nl_j, 0.0) if j in masks else nl_j)
    lhs = jnp.concatenate([jnp.concatenate(_split2(nl_j), axis=1) for nl_j in nl], axis=0)
    cs = _dot_row_halves(_dot, lhs, u)
    att = [None] * nb
    for j in reversed(range(nb)):
        cs_j = cs[j * tq:(j + 1) * tq]
        att_j = jnp.exp2(log_beta[j] - cs_j - carry)
        att[j] = (jnp.where(masks[j], att_j, 0.0) if j in masks else att_j).astype(BF16)
        carry = carry + (cs_j[:, 0:1] + nl[j][:, 0:1])
    return _dot_row_halves(_dot, jnp.concatenate(att, axis=1), v16), carry


def _head_slices(ref):
    return [slice(i * HEAD_B, (i + 1) * HEAD_B) for i in range(ref.shape[1] // HEAD_B)]


def _sb_prompt_kernel(*refs, tq, nq):
    q_ref, z_ref, k_ref, v_ref, u_ref = refs[:5]
    o_ref, kout_ref, vout_ref = refs[-3:]
    kout_ref[...] = k_ref[...]
    vout_ref[...] = v_ref[...]
    row = lax.broadcasted_iota(jnp.int32, (tq, tq), 0)
    col = lax.broadcasted_iota(jnp.int32, (tq, tq), 1)
    diagonal = col < row
    k16 = k_ref[...].astype(BF16)
    v16 = v_ref[...].astype(BF16)
    for qi in range(nq):
        rows = slice(qi * tq, (qi + 1) * tq)
        n_keys = (qi + 1) * tq
        for s in _head_slices(q_ref):
            out, _ = _sb_group(q_ref[rows, s].astype(BF16), k16[0:n_keys, s], v16[0:n_keys, s], tq, u_ref[...],
                               {qi: diagonal}, jnp.zeros((tq, 1), F32))
            o_ref[rows, s] = (out * jax.nn.silu(z_ref[rows, s])).astype(o_ref.dtype)


def _sb_sample_kernel(*refs, tk):
    q_ref, z_ref, kn_ref, vn_ref, kp_ref, vp_ref, ud_ref, uf_ref = refs[:8]
    o_ref, kout_ref, vout_ref = refs[-3:]
    kout_ref[...] = kn_ref[...]
    vout_ref[...] = vn_ref[...]
    tq = q_ref.shape[0]
    row = lax.broadcasted_iota(jnp.int32, (tq, tq), 0)
    col = lax.broadcasted_iota(jnp.int32, (tq, tq), 1)
    heads = _head_slices(q_ref)
    n_past = kp_ref.shape[0] // len(heads)
    for hd, s in enumerate(heads):
        q16 = q_ref[:, s].astype(BF16)
        out_new, carry = _sb_group(q16, kn_ref[:, s].astype(BF16), vn_ref[:, s].astype(BF16), tq, ud_ref[...],
                                   {0: col < row}, jnp.zeros((tq, 1), F32))
        own_rows = pl.ds(hd, n_past, stride=len(heads))
        out_past, _ = _sb_group(q16, kp_ref[own_rows, :].astype(BF16), vp_ref[own_rows, :].astype(BF16), tk,
                                uf_ref[...], {}, carry)
        o_ref[:, s] = ((out_new + out_past) * jax.nn.silu(z_ref[:, s])).astype(o_ref.dtype)


def _suffix_matrix(tk):
    s_later = lax.broadcasted_iota(jnp.int32, (tk, tk), 0)
    s_here = lax.broadcasted_iota(jnp.int32, (tk, tk), 1)
    u = (s_later > s_here).astype(BF16)
    return jnp.concatenate([u, u], axis=0)


def stick_breaking(proj, q_col, z_col, k_col, v_col, past_k, past_v, nseq, t, heads, layer, depth, kv_out):
    hp = SB_HEADS_PER_STEP if past_k is None else heads
    wide = hp * HEAD_B
    tq = min(SB_BLOCK, t)
    seq = lambda rows, col: pl.BlockSpec((rows, wide), lambda b, h: (b, col // hp + h))
    const = lambda a: pl.BlockSpec(a.shape, lambda b, h: (0, 0))
    if past_k is None:
        u = _suffix_matrix(tq)
        body = functools.partial(_sb_prompt_kernel, tq=tq, nq=t // tq)
        in_specs = [seq(t, q_col), seq(t, z_col), seq(t, k_col), seq(t, v_col), const(u)]
        args = [proj, proj, proj, proj, u]
    else:
        assert t == tq
        n_past = past_k.shape[2]
        past = pl.BlockSpec((None, None, n_past * heads, HEAD_B), lambda b, h: (layer, b, 0, 0))
        past_k, past_v = (a.reshape(depth, nseq, n_past * heads, HEAD_B) for a in (past_k, past_v))
        ud, uf = _suffix_matrix(tq), _suffix_matrix(SB_BLOCK)
        body = functools.partial(_sb_sample_kernel, tk=SB_BLOCK)
        in_specs = [seq(t, q_col), seq(t, z_col), seq(t, k_col), seq(t, v_col), past, past, const(ud), const(uf)]
        args = [proj, proj, proj, proj, past_k, past_v, ud, uf]
    aliases = {}
    if kv_out is not None:
        aliases = {len(args): 1, len(args) + 1: 2}
        in_specs = in_specs + [pl.BlockSpec(memory_space=pl.ANY)] * 2
        args = args + list(kv_out)
    kv_shape = jax.ShapeDtypeStruct((depth, nseq * t, heads * HEAD_B), F32)
    kv_spec = pl.BlockSpec((None, t, wide), lambda b, h: (layer, b, h))
    o_b, k_all, v_all = pl.pallas_call(
        body,
        grid=(nseq, heads // hp),
        in_specs=in_specs,
        out_specs=[seq(t, 0), kv_spec, kv_spec],
        out_shape=[jax.ShapeDtypeStruct((nseq * t, heads * HEAD_B), BF16), kv_shape, kv_shape],
        input_output_aliases=aliases,
        compiler_params=_params("parallel", "parallel"),
        name="stick_breaking",
    )(*args)
    return o_b, (k_all, v_all)


def _block_stack(x, lane_masks):
    return jnp.concatenate([jnp.where(m, x, 0.0) for m in lane_masks], axis=0)


def _rwkv_kernel(*refs, chunk, width, with_cast):
    (x_ref, lora_ref, z_ref, sh_main_ref, sh_lora_ref, s0_ref,
     mu_main_ref, mu_lora_ref, w0_ref, wup_ref, a0_ref, aup_ref, kk_ref, ka_ref, rk_ref,
     gnw_ref, gnb_ref, e_ref, bd_ref, ltri_ref) = refs[:20]
    xbuf, lbuf, sbd = refs[-3:]
    if with_cast:
        cast_in_ref = refs[20]
        o_ref, sout_ref, shm_out_ref, shl_out_ref, cast_out_ref = refs[21:26]
        cast_out_ref[...] = cast_in_ref[...].astype(cast_out_ref.dtype)
    else:
        o_ref, sout_ref, shm_out_ref, shl_out_ref = refs[20:24]
    c = pl.program_id(1)
    n_chunks = pl.num_programs(1)
    C, W = chunk, width
    n_seq = x_ref.shape[0]
    n_slabs = W // SLAB
    n_heads = W // HEAD_A

    @pl.when(c == 0)
    def _init():
        sbd[...] = jnp.zeros_like(sbd)
        for s in range(n_seq):
            xbuf[s, 7:8, :] = sh_main_ref[s]
            lbuf[s, 7:8, :] = sh_lora_ref[s]
            for hd in range(n_heads):
                g, j = divmod(hd, GROUP_A)
                sbd[s, g, j * HEAD_A:(j + 1) * HEAD_A, j * HEAD_A:(j + 1) * HEAD_A] = s0_ref[s, hd]

    def shifted(ref, buf, mu_ref):
        rows = []
        for s in range(n_seq):
            x = ref[s]
            buf[s, 8:8 + C, :] = x
            rows.append(x + mu_ref[...] * (buf[s, 7:7 + C, :] - x))
            buf[s, 7:8, :] = x[C - 1:C, :]
        return jnp.concatenate(rows, axis=0)

    xs = shifted(x_ref, xbuf, mu_main_ref)
    lo_s = shifted(lora_ref, lbuf, mu_lora_ref)
    seq_rows = [slice(s * C, (s + 1) * C) for s in range(n_seq)]
    slabs = [slice(g * SLAB, (g + 1) * SLAB) for g in range(n_slabs)]

    r, k, v = xs[:, :W], xs[:, W:2 * W], xs[:, 2 * W:]
    w_pre = w0_ref[...] + _dot(jnp.tanh(lo_s).astype(BF16), wup_ref[...])
    ld = -math.exp(-0.5) * jax.nn.sigmoid(w_pre)
    a = jax.nn.sigmoid(a0_ref[...] + _dot(lo_s.astype(BF16), aup_ref[...]))

    e_mat = e_ref[...]

    segs = [slice(g * SEG_LANES, (g + 1) * SEG_LANES) for g in range(W // SEG_LANES)]

    def seg_sum(val):
        n = val.shape[0]
        hi, lo = _split2(val)
        out = _dot(jnp.concatenate([part[:, sl] for part in (hi, lo) for sl in segs], axis=0), e_mat)
        return jnp.concatenate([out[g * n:(g + 1) * n] + out[(len(segs) + g) * n:(len(segs) + g + 1) * n]
                                for g in range(len(segs))], axis=1)

    kk = k * kk_ref[...]
    kk = kk * lax.rsqrt(jnp.maximum(seg_sum(kk * kk), 1e-24))
    kmod = k * (1.0 + (a - 1.0) * ka_ref[...])

    ld_hi = ld.astype(BF16)
    ld_r1 = ld - ld_hi.astype(F32)
    ld_mid = ld_r1.astype(BF16)
    ld_lo = (ld_r1 - ld_mid.astype(F32)).astype(BF16)
    ltri = ltri_ref[...]
    lp = _dot(ltri, ld_hi) + _dot(ltri, ld_mid) + _dot(ltri, ld_lo)
    lp_last = [lp[rs.stop - 1:rs.stop, :] for rs in seq_rows]
    lp_end = jnp.concatenate([jnp.broadcast_to(row, (C, W)) for row in lp_last], axis=0)
    e_neg = jnp.exp(-lp)
    kka = kk * a
    kap = kk * jnp.exp(lp - ld)
    bet = kka * e_neg
    kt = kmod * e_neg
    rt = r * jnp.exp(lp)
    e_end = jnp.exp(lp_end - lp)
    kt_end = kmod * e_end
    bet_end = kka * e_end
    dec_end = [jnp.exp(row) for row in lp_last]

    lane = lax.broadcasted_iota(jnp.int32, (1, SLAB), 1)
    lane_masks = [(lane >= j * HEAD_A) & (lane < (j + 1) * HEAD_A) for j in range(GROUP_A)]
    t_row = lax.broadcasted_iota(jnp.int32, (C, GROUP_A * C), 0)
    s_col = lax.broadcasted_iota(jnp.int32, (C, GROUP_A * C), 1) & (C - 1)
    strict, incl = s_col < t_row, s_col <= t_row
    stack = lambda val16: _block_stack(val16, lane_masks)
    nh = GROUP_A * C
    units = [(s, g) for s in range(n_seq) for g in range(n_slabs)]
    per_slab = lambda fn: [fn(u, seq_rows[s], slabs[g]) for u, (s, g) in enumerate(units)]

    kap16, rt16, bet16, kt16, v16 = (val.astype(BF16) for val in (kap, rt, bet, kt, v))
    s_old = [sbd[s, g] for s, g in units]
    lhs = per_slab(lambda g, rs, sl: jnp.concatenate([kap16[rs, sl], rt16[rs, sl]], axis=0))
    rhs = per_slab(lambda g, rs, sl: jnp.concatenate([stack(bet16[rs, sl]), stack(kt16[rs, sl])], axis=0))
    sc = per_slab(lambda g, rs, sl: _dot_nt(lhs[g], rhs[g]))
    ls = per_slab(lambda g, rs, sl: _dot_nt(lhs[g], s_old[g].astype(BF16)))
    v_stack = per_slab(lambda g, rs, sl: stack(v16[rs, sl]))
    p16 = per_slab(lambda g, rs, sl: jnp.where(strict, -sc[g][:C, :nh], 0.0).astype(BF16))
    xw = per_slab(lambda g, rs, sl: ls[g][:C]
                  + _dot(jnp.where(strict, sc[g][:C, nh:], 0.0).astype(BF16), v_stack[g]))
    for step in range(6):
        xw = per_slab(lambda g, rs, sl: xw[g] + _dot(p16[g], stack(xw[g].astype(BF16))))
        if step < 5:
            p16 = per_slab(lambda g, rs, sl: _dot(p16[g], stack(p16[g])).astype(BF16))
    ab_inc = per_slab(lambda g, rs, sl: jnp.concatenate(
        [jnp.where(incl, sc[g][C:, :nh], 0.0), jnp.where(incl, sc[g][C:, nh:], 0.0)], axis=1).astype(BF16))
    y = per_slab(lambda g, rs, sl: ls[g][C:] + _dot(
        ab_inc[g], jnp.concatenate([stack((-xw[g]).astype(BF16)), v_stack[g]], axis=0)))
    upd = per_slab(lambda g, rs, sl: _dot(
        jnp.concatenate([v[rs, sl], -xw[g]], axis=0).T.astype(BF16),
        jnp.concatenate([kt_end[rs, sl], bet_end[rs, sl]], axis=0).astype(BF16)))
    for u, (s, g) in enumerate(units):
        sbd[s, g] = s_old[u] * dec_end[s][:, slabs[g]] + upd[u] * bd_ref[...]

    y = jnp.concatenate([jnp.concatenate(y[s * n_slabs:(s + 1) * n_slabs], axis=1) for s in range(n_seq)], axis=0)
    inv_n = 1.0 / HEAD_A
    mean = seg_sum(y) * inv_n
    yc = y - mean
    var = seg_sum(yc * yc) * inv_n
    y = yc * lax.rsqrt(var + GN_EPS) * gnw_ref[...] + gnb_ref[...]
    y = y + seg_sum(r * kmod * rk_ref[...]) * v
    for s, rs in enumerate(seq_rows):
        o_ref[s] = (y[rs] * jax.nn.silu(z_ref[s])).astype(o_ref.dtype)

    @pl.when(c == n_chunks - 1)
    def _fin():
        for s in range(n_seq):
            shm_out_ref[s] = xbuf[s, 7:8, :]
            shl_out_ref[s] = lbuf[s, 7:8, :]
            for hd in range(n_heads):
                g, j = divmod(hd, GROUP_A)
                sout_ref[s, hd] = sbd[s, g, j * HEAD_A:(j + 1) * HEAD_A, j * HEAD_A:(j + 1) * HEAD_A]


RWKV_SEQS_PER_STEP = 2


def rwkv7(proj, lora, z_col, shift_main, shift_lora, s0, prm, nseq, t, cast_f32=None):
    W = prm["a_w0"].shape[0]
    C = RWKV_CHUNK
    S = RWKV_SEQS_PER_STEP
    nc = t // C
    n_heads = W // HEAD_A
    lw = lora.shape[1]
    half = lw // 2
    zpad = jnp.zeros((half, W), F32)
    wup = jnp.concatenate([prm["a_w_up"], zpad], axis=0).astype(BF16)
    aup = jnp.concatenate([zpad, prm["a_a_up"]], axis=0).astype(BF16)
    same_head = lambda n: (jnp.arange(n)[:, None] // HEAD_A) == (jnp.arange(n)[None, :] // HEAD_A)
    ti = jnp.arange(S * C)
    ltri = ((ti[:, None] >= ti[None, :]) & (ti[:, None] // C == ti[None, :] // C)).astype(BF16)
    row1 = lambda a: a.reshape(1, -1)
    vec = lambda n: pl.BlockSpec((1, n), lambda b, c: (0, 0))
    full = lambda a: pl.BlockSpec(a.shape, lambda b, c: (0, 0))
    rows = lambda n, col: pl.BlockSpec((S, C, n), lambda b, c: (b, c, col))
    per_seq = lambda n: pl.BlockSpec((S, 1, n), lambda b, c: (b, 0, 0))
    state = pl.BlockSpec((S, n_heads, HEAD_A, HEAD_A), lambda b, c: (b, 0, 0, 0))
    e_mat, bd_mask = same_head(SEG_LANES).astype(BF16), same_head(SLAB).astype(F32)
    proj3 = proj.reshape(nseq, t, proj.shape[1])
    in_specs = [rows(3 * W, 0), rows(lw, 0), rows(W, z_col),
                per_seq(3 * W), per_seq(lw), state,
                vec(3 * W), vec(lw), vec(W), full(wup), vec(W), full(aup), vec(W), vec(W), vec(W), vec(W), vec(W),
                full(e_mat), full(bd_mask), full(ltri)]
    args = [proj3, lora.reshape(nseq, t, lw), proj3, shift_main, shift_lora, s0,
            row1(prm["a_mu"][:3 * W]), row1(prm["a_mu"][3 * W:]), row1(prm["a_w0"]), wup, row1(prm["a_a0"]), aup,
            row1(prm["a_k_k"]), row1(prm["a_k_a"]), row1(prm["a_r_k"]), row1(prm["a_gn_w"]), row1(prm["a_gn_b"]),
            e_mat, bd_mask, ltri]
    out_specs = [rows(W, 0), state, per_seq(3 * W), per_seq(lw)]
    out_shape = [jax.ShapeDtypeStruct((nseq, t, W), BF16),
                 jax.ShapeDtypeStruct((nseq, n_heads, HEAD_A, HEAD_A), F32),
                 jax.ShapeDtypeStruct((nseq, 1, 3 * W), F32),
                 jax.ShapeDtypeStruct((nseq, 1, lw), F32)]
    if cast_f32 is not None:
        steps = (nseq // S) * nc
        flat = cast_f32.reshape(-1, cast_f32.shape[-1])
        slab = pl.BlockSpec((flat.shape[0] // steps, flat.shape[1]), lambda b, c: (b * nc + c, 0))
        in_specs, args = in_specs + [slab], args + [flat]
        out_specs, out_shape = out_specs + [slab], out_shape + [jax.ShapeDtypeStruct(flat.shape, BF16)]
    o_a, s_new, sh_main, sh_lora, *cast16 = pl.pallas_call(
        functools.partial(_rwkv_kernel, chunk=C, width=W, with_cast=cast_f32 is not None),
        grid=(nseq // S, nc),
        in_specs=in_specs,
        out_specs=out_specs,
        out_shape=out_shape,
        scratch_shapes=[pltpu.VMEM((S, C + 8, 3 * W), F32), pltpu.VMEM((S, C + 8, lw), F32),
                        pltpu.VMEM((S, W // SLAB, SLAB, SLAB), F32)],
        compiler_params=_params("parallel", "arbitrary"),
        name="rwkv7",
    )(*args)
    cast16 = cast16[0].reshape(cast_f32.shape) if cast16 else None
    return o_a.reshape(nseq * t, W), s_new, sh_main, sh_lora, cast16


COL_AZ, COL_BQ, COL_BK, COL_BV, COL_BZ, COL_CU, COL_MQ = 3, 4, 5, 6, 7, 8, 11


def trunk_layer(x, h, nseq, t, mk, mv, shift_prev, s_prev, past_k, past_v, prm, layer, depth, kv_out, want_vn):
    m, d = x.shape
    W = d // 4
    tm = min(512, m)
    if h is None:
        h = rmsnorm_rows(x, prm["g_pre"], min(256, m))
    proj = matmul(h, prm["w_in_main"], layer, min(1024, m), 1024)
    lora = matmul(h, prm["w_in_lora"], layer, min(1024, m), prm["w_in_lora"].shape[2])

    w_gate = prm["w_gate"]
    o_a, s_new, sh_main, sh_lora, w_gate16 = rwkv7(
        proj, lora, COL_AZ, shift_prev[..., :3 * W], shift_prev[..., 3 * W:], s_prev, prm, nseq, t,
        cast_f32=w_gate if w_gate.dtype == F32 else None)
    w_gate = w_gate if w_gate16 is None else w_gate16
    shift_new = jnp.concatenate([sh_main, sh_lora], axis=-1)

    heads_b = W // HEAD_B
    hb = lambda col: col * heads_b
    o_b, kv_out = stick_breaking(proj, hb(COL_BQ), hb(COL_BZ), hb(COL_BK), hb(COL_BV), past_k, past_v,
                                 nseq, t, heads_b, layer, depth, kv_out)

    clen = min(t, prm["c_ws"].shape[1])
    groups = prm["c_ws"].shape[0]
    bs_full = jnp.repeat(prm["c_bs"][:, :clen].T, W // groups, axis=1)
    o_c, *vn_c = chunk_mlp(proj, COL_CU, prm["c_ws"][:, :clen, :clen], bs_full, prm["c_ln_w"], prm["c_ln_b"],
                           m, clen, max(1, min(512, t) // clen), want_vn)

    o_m = memory_attention(proj, COL_MQ, mk, mv, nseq, t, min(512, t))

    merged = gated_merge(h, (o_a, o_b, o_c, o_m), w_gate, prm["b_gate"], prm["w_br"], layer, tm,
                         512 if m > tm else 256)
    x_new, h_next = out_proj_norm_residual(merged, prm["w_out"], layer, x, prm["g_post"], prm["g_pre_next"], tm, 512)
    return x_new, h_next, shift_new, s_new, kv_out, vn_c, w_gate


def kernel(x_prompt, x_sample, cache_mem_k, cache_mem_v, cache_sb_k, cache_sb_v, state_rwkv, state_shift, mem_prompt, g_pre, g_post, w_in, a_mu, a_w0, a_w_up, a_a0, a_a_up, a_k_k, a_k_a, a_r_k, a_gn_w, a_gn_b, c_ws, c_bs, c_ln_w, c_ln_b, g_mem, w_mem_kv, w_gate, b_gate, w_br, w_out):
    bp, tp, d = x_prompt.shape
    bs_, ts, _ = x_sample.shape
    depth = w_in.shape[0]
    W = d // 4
    n_mem = mem_prompt.shape[1]
    m_heads = cache_mem_k.shape[3]
    b_heads = cache_sb_k.shape[3]
    a_heads = state_rwkv.shape[2]
    shift_w = state_shift.shape[-1]
    n_past = cache_sb_k.shape[2]
    lora_lo, lora_hi = 3 * W, shift_w

    yp = x_prompt.reshape(bp * tp, d)
    ys = x_sample.reshape(bs_ * ts, d)
    mem_rows = mem_prompt.reshape(bp * n_mem, d)
    mem_out, rw_p, sh_p, rw_s, sh_s, cv_s = [[], []], [], [], [], [], []
    kv_p = kv_s = hp = hs = None
    dense = {
        "w_in_main": drop_cols_cast(w_in, lora_lo, lora_hi, W, 1024),
        "w_in_lora": w_in[:, :, lora_lo:lora_hi].astype(BF16),
        "w_gate": w_gate, "w_br": w_br.astype(BF16), "w_out": w_out.astype(BF16),
    }
    w_mem16 = w_mem_kv.astype(BF16)
    for l in range(depth):
        prm = dict(dense)
        prm.update({
            "g_pre": g_pre[l], "g_post": g_post[l], "g_pre_next": g_pre[l + 1] if l + 1 < depth else None,
            "a_mu": a_mu[l], "a_w0": a_w0[l], "a_w_up": a_w_up[l], "a_a0": a_a0[l], "a_a_up": a_a_up[l],
            "a_k_k": a_k_k[l].reshape(-1), "a_k_a": a_k_a[l].reshape(-1), "a_r_k": a_r_k[l].reshape(-1),
            "a_gn_w": a_gn_w[l].reshape(-1), "a_gn_b": a_gn_b[l].reshape(-1),
            "c_ws": c_ws[l], "c_bs": c_bs[l], "c_ln_w": c_ln_w[l], "c_ln_b": c_ln_b[l], "b_gate": b_gate[l],
        })
        kv = matmul(rmsnorm_rows(mem_rows, g_mem[l], 256), w_mem16, l, min(1024, bp * n_mem), 1024)
        mk = kv[:, :W].reshape(bp, n_mem, W)
        mv = kv[:, W:].reshape(bp, n_mem, W)
        shift0 = jnp.zeros((bp, 1, shift_w), F32)
        s0 = jnp.zeros((bp, a_heads, HEAD_A, HEAD_A), F32)
        yp, hp, sh, st, kv_p, _, dense["w_gate"] = trunk_layer(
            yp, hp, bp, tp, mk.astype(BF16), mv.astype(BF16), shift0, s0, None, None, prm, l, depth, kv_p, False)
        prm["w_gate"] = dense["w_gate"]
        mem_out[0].append(mk.reshape(bp, n_mem, m_heads, W // m_heads))
        mem_out[1].append(mv.reshape(bp, n_mem, m_heads, W // m_heads))
        rw_p.append(st)
        sh_p.append(sh)
        ys, hs, sh, st, kv_s, cvn, _ = trunk_layer(
            ys, hs, bs_, ts, cache_mem_k[l].reshape(bs_, n_mem, W).astype(BF16),
            cache_mem_v[l].reshape(bs_, n_mem, W).astype(BF16), state_shift[l], state_rwkv[l],
            cache_sb_k, cache_sb_v, prm, l, depth, kv_s, True)
        rw_s.append(st)
        sh_s.append(sh)
        cv_s.append(cvn[0].reshape(bs_, ts, W))
    heads_p = lambda a: a.reshape(depth, bp, tp, b_heads, HEAD_B)
    heads_s = lambda a: a.reshape(depth, bs_, ts, b_heads, HEAD_B)
    return (yp.reshape(bp, tp, d), ys.reshape(bs_, ts, d), jnp.stack(mem_out[0]), jnp.stack(mem_out[1]),
            heads_p(kv_p[0]), heads_p(kv_p[1]), jnp.stack(rw_p), jnp.stack(sh_p),
            heads_s(kv_s[0]), heads_s(kv_s[1]), jnp.stack(rw_s), jnp.stack(sh_s), jnp.stack(cv_s))
```

```python
import functools
import math

import jax
import jax.numpy as jnp
from jax import lax
from jax.experimental import pallas as pl
from jax.experimental.pallas import tpu as pltpu

F32 = jnp.float32
BF16 = jnp.bfloat16

NORM_EPS = 1e-6
GN_EPS = 64e-5
HEAD_A = 64
GROUP_A = 2
SLAB = HEAD_A * GROUP_A
SEG_LANES = 256
HEAD_B = 128
HEAD_M = 256
GROUP_C = 128
RWKV_CHUNK = 64
VMEM_LIMIT_BYTES = 56 * 1024 * 1024


def _params(*sem):
    return pltpu.CompilerParams(dimension_semantics=sem, vmem_limit_bytes=VMEM_LIMIT_BYTES)


def _split2(x):
    hi = x.astype(BF16)
    lo = (x - hi.astype(F32)).astype(BF16)
    return hi, lo


def _dot(a, b):
    return jnp.dot(a, b, preferred_element_type=F32)


def _dot_nt(a, b):
    return lax.dot_general(a, b, (((1,), (1,)), ((), ())), preferred_element_type=F32)


def _dot_row_halves(dot, a, b):
    half = a.shape[0] // 2
    return jnp.concatenate([dot(a[:half], b), dot(a[half:], b)], axis=0)


def _rmsnorm_kernel(x_ref, g_ref, o_ref):
    x = x_ref[...]
    ms = jnp.mean(x * x, axis=-1, keepdims=True)
    o_ref[...] = (x * lax.rsqrt(ms + NORM_EPS) * g_ref[...]).astype(o_ref.dtype)


def rmsnorm_rows(x, g, tm):
    m, d = x.shape
    return pl.pallas_call(
        _rmsnorm_kernel,
        grid=(m // tm,),
        in_specs=[pl.BlockSpec((tm, d), lambda i: (i, 0)), pl.BlockSpec((1, d), lambda i: (0, 0))],
        out_specs=pl.BlockSpec((tm, d), lambda i: (i, 0)),
        out_shape=jax.ShapeDtypeStruct((m, d), BF16),
        compiler_params=_params("parallel"),
        name="rmsnorm_rows",
    )(x, g.reshape(1, d))


POST_ROWS = 64


def _out_proj_kernel(*refs, tn, with_next):
    if with_next:
        m_ref, w_ref, x_ref, g_ref, gn_ref, o_ref, hn_ref, x_rows, ssq = refs
    else:
        m_ref, w_ref, x_ref, g_ref, o_ref, x_rows, ssq = refs
    j = pl.program_id(1)
    cols = pl.ds(pl.multiple_of(j * tn, tn), tn)
    y = _dot(m_ref[...], w_ref[...])
    sq = jnp.sum(y * y, axis=-1, keepdims=True)
    o_ref[:, cols] = y * g_ref[...]
    x_rows[:, cols] = x_ref[...]

    @pl.when(j == 0)
    def _first():
        ssq[...] = sq

    @pl.when(j > 0)
    def _rest():
        ssq[...] += sq

    @pl.when(j == pl.num_programs(1) - 1)
    def _normalise():
        inv_d = 1.0 / o_ref.shape[1]
        for r0 in range(0, o_ref.shape[0], POST_ROWS):
            rows = slice(r0, r0 + POST_ROWS)
            scale = lax.rsqrt(ssq[rows, :] * inv_d + NORM_EPS)
            x_new = x_rows[rows, :] + o_ref[rows, :] * scale
            o_ref[rows, :] = x_new
            if with_next:
                ms = jnp.mean(x_new * x_new, axis=-1, keepdims=True)
                hn_ref[rows, :] = (x_new * lax.rsqrt(ms + NORM_EPS) * gn_ref[...]).astype(hn_ref.dtype)


def out_proj_norm_residual(merged, w, layer, x, g, g_next, tm, tn):
    m, d = x.shape
    with_next = g_next is not None
    row = pl.BlockSpec((tm, d), lambda i, j: (i, 0))
    gain = pl.BlockSpec((1, d), lambda i, j: (0, 0))
    out = pl.pallas_call(
        functools.partial(_out_proj_kernel, tn=tn, with_next=with_next),
        grid=(m // tm, d // tn),
        in_specs=[row, pl.BlockSpec((None, d, tn), lambda i, j: (layer, 0, j)),
                  pl.BlockSpec((tm, tn), lambda i, j: (i, j)), pl.BlockSpec((1, tn), lambda i, j: (0, j))]
                 + ([gain] if with_next else []),
        out_specs=[row, row] if with_next else [row],
        out_shape=[jax.ShapeDtypeStruct((m, d), F32)] + ([jax.ShapeDtypeStruct((m, d), BF16)] if with_next else []),
        scratch_shapes=[pltpu.VMEM((tm, d), F32), pltpu.VMEM((tm, 1), F32)],
        compiler_params=_params("parallel", "arbitrary"),
        name="out_proj_norm_residual",
    )(merged, w, x, g.reshape(1, d), *([g_next.reshape(1, d)] if with_next else []))
    return out if with_next else (out[0], None)


def _drop_cols_kernel(a_ref, b_ref, o_ref, *, first_shifted, gap):
    j = pl.program_id(2)

    @pl.when(j < first_shifted)
    def _before_gap():
        o_ref[...] = a_ref[...].astype(o_ref.dtype)

    @pl.when(j >= first_shifted)
    def _after_gap():
        o_ref[...] = jnp.concatenate([a_ref[:, gap:], b_ref[...]], axis=1).astype(o_ref.dtype)


def drop_cols_cast(w, lo, hi, cw, tk):
    depth, k, n = w.shape
    gap = hi - lo
    n_out = n - gap
    return pl.pallas_call(
        functools.partial(_drop_cols_kernel, first_shifted=lo // cw, gap=gap),
        grid=(depth, k // tk, n_out // cw),
        in_specs=[pl.BlockSpec((None, tk, cw), lambda l, i, j: (l, i, j)),
                  pl.BlockSpec((None, tk, gap), lambda l, i, j: (l, i, (j + 1) * (cw // gap)))],
        out_specs=pl.BlockSpec((None, tk, cw), lambda l, i, j: (l, i, j)),
        out_shape=jax.ShapeDtypeStruct((depth, k, n_out), BF16),
        compiler_params=_params("parallel", "parallel", "parallel"),
        name="drop_cols_cast",
    )(w, w)
def _matmul_kernel(x_ref, w_ref, o_ref):
    o_ref[...] = _dot(x_ref[...], w_ref[...]).astype(o_ref.dtype)


def matmul(x, w, layer, tm, tn, out_dtype=F32):
    m, k = x.shape
    n = w.shape[2]
    return pl.pallas_call(
        _matmul_kernel,
        grid=(n // tn, m // tm),
        in_specs=[pl.BlockSpec((tm, k), lambda j, i: (i, 0)),
                  pl.BlockSpec((None, k, tn), lambda j, i: (layer, 0, j))],
        out_specs=pl.BlockSpec((tm, tn), lambda j, i: (i, j)),
        out_shape=jax.ShapeDtypeStruct((m, n), out_dtype),
        compiler_params=_params("parallel", "parallel"),
        name="matmul",
    )(x, w)


def _merge_kernel(h_ref, oa_ref, ob_ref, oc_ref, om_ref, wg_ref, bg_ref, wbr_ref, out_ref):
    h = h_ref[...]
    acc = None
    for n, o_ref in enumerate((oa_ref, ob_ref, oc_ref, om_ref)):
        gate = jax.nn.sigmoid(_dot(h, wg_ref[n]) + bg_ref[n])
        term = gate * _dot(o_ref[...], wbr_ref[n])
        acc = term if acc is None else acc + term
    out_ref[...] = acc.astype(out_ref.dtype)


def gated_merge(h, branches, wg, bg, wbr, layer, tm, tn):
    m, d = h.shape
    _, nb, w, _ = wbr.shape
    resident = dict(pipeline_mode=pl.Buffered(1)) if m > tm else {}
    o_spec = pl.BlockSpec((tm, w), lambda j, i: (i, 0))
    return pl.pallas_call(
        _merge_kernel,
        grid=(d // tn, m // tm),
        in_specs=[pl.BlockSpec((tm, d), lambda j, i: (i, 0)), o_spec, o_spec, o_spec, o_spec,
                  pl.BlockSpec((None, nb, d, tn), lambda j, i: (layer, 0, 0, j), **resident),
                  pl.BlockSpec((nb, 1, tn), lambda j, i: (0, 0, j)),
                  pl.BlockSpec((None, nb, w, tn), lambda j, i: (layer, 0, 0, j), **resident)],
        out_specs=pl.BlockSpec((tm, tn), lambda j, i: (i, j)),
        out_shape=jax.ShapeDtypeStruct((m, d), BF16),
        compiler_params=_params("parallel", "parallel"),
        name="gated_merge",
    )(h, *branches, wg, bg.reshape(nb, 1, d), wbr)


def _memattn_kernel(q_ref, mk_ref, mv_ref, o_ref, *, heads):
    scale = 1.0 / math.sqrt(HEAD_M)
    for hd in range(heads):
        sl = slice(hd * HEAD_M, (hd + 1) * HEAD_M)
        q = q_ref[:, sl].astype(BF16)
        s = _dot_nt(q, mk_ref[0, :, sl]) * scale
        s = s - jnp.max(s, axis=-1, keepdims=True)
        p = jnp.exp(s)
        p = p * (1.0 / jnp.sum(p, axis=-1, keepdims=True))
        o_ref[:, sl] = _dot(p.astype(BF16), mv_ref[0, :, sl]).astype(o_ref.dtype)


def memory_attention(proj, q_col, mk, mv, nseq, t, tq):
    n_mem, w = mk.shape[1:]
    per_seq = t // tq
    mem_spec = pl.BlockSpec((1, n_mem, w), lambda i: (i // per_seq, 0, 0))
    return pl.pallas_call(
        functools.partial(_memattn_kernel, heads=w // HEAD_M),
        grid=(nseq * per_seq,),
        in_specs=[pl.BlockSpec((tq, w), lambda i: (i, q_col)), mem_spec, mem_spec],
        out_specs=pl.BlockSpec((tq, w), lambda i: (i, 0)),
        out_shape=jax.ShapeDtypeStruct((nseq * t, w), BF16),
        compiler_params=_params("parallel"),
        name="memory_attention",
    )(proj, mk, mv)


def _cmlp_kernel(u_ref, v_ref, z_ref, ws_ref, bs_ref, lnw_ref, lnb_ref, o_ref, *maybe_vn_ref, clen, chunks):
    v = v_ref[...]
    mu = jnp.mean(v, axis=-1, keepdims=True)
    var = jnp.mean(jnp.square(v - mu), axis=-1, keepdims=True)
    vn = (v - mu) * lax.rsqrt(var + NORM_EPS) * lnw_ref[...] + lnb_ref[...]
    for vn_ref in maybe_vn_ref:
        vn_ref[...] = vn
    vn16 = vn.astype(BF16)
    groups = vn.shape[1] // GROUP_C
    row = lax.broadcasted_iota(jnp.int32, (clen, clen), 0)
    col = lax.broadcasted_iota(jnp.int32, (clen, clen), 1)
    for g in range(groups):
        wm = jnp.where(row >= col, ws_ref[g], 0.0).astype(BF16)
        cs = slice(g * GROUP_C, (g + 1) * GROUP_C)
        for c in range(chunks):
            rs = slice(c * clen, (c + 1) * clen)
            s = _dot(wm, vn16[rs, cs]) + bs_ref[:, cs]
            o_ref[rs, cs] = (u_ref[rs, cs] * s * jax.nn.silu(z_ref[rs, cs])).astype(o_ref.dtype)


def chunk_mlp(proj, u_col, ws, bs_full, ln_w, ln_b, rows, clen, chunks, want_vn):
    w = ln_w.shape[0]
    tm = clen * chunks
    col = lambda c: pl.BlockSpec((tm, w), lambda i: (i, c))
    const2 = lambda a: pl.BlockSpec(a.shape, lambda i: (0, 0))
    out = pl.BlockSpec((tm, w), lambda i: (i, 0))
    out_dtypes = [BF16, F32] if want_vn else [BF16]
    return pl.pallas_call(
        functools.partial(_cmlp_kernel, clen=clen, chunks=chunks),
        grid=(rows // tm,),
        in_specs=[col(u_col), col(u_col + 1), col(u_col + 2),
                  pl.BlockSpec(ws.shape, lambda i: (0, 0, 0)), const2(bs_full),
                  pl.BlockSpec((1, w), lambda i: (0, 0)), pl.BlockSpec((1, w), lambda i: (0, 0))],
        out_specs=[out] * len(out_dtypes),
        out_shape=[jax.ShapeDtypeStruct((rows, w), dt) for dt in out_dtypes],
        compiler_params=_params("parallel"),
        name="chunk_mlp",
    )(proj, proj, proj, ws, bs_full, ln_w.reshape(1, w), ln_b.reshape(1, w))


SB_HEADS_PER_STEP = 2
SB_BLOCK = 256


def _sb_group(q16, k16, v16, tk, u, masks, carry):
    tq = q16.shape[0]
    nb = k16.shape[0] // tk
    z_all = _dot_nt(q16, k16)
    nl, log_beta = [], []
    for j in range(nb):
        z2 = z_all[:, j * tk:(j + 1) * tk] * (math.log2(math.e) / math.sqrt(HEAD_B))
        neg_abs = lax.bitcast_convert_type(lax.bitcast_convert_type(z2, jnp.uint32) | jnp.uint32(0x80000000), F32)
        nl_j = jnp.maximum(z2, 0.0) + jnp.log2(1.0 + jnp.exp2(neg_abs))
        log_beta.append(z2 - nl_j)
        nl.append(jnp.where(masks[j], nl_j, 0.0) if j in masks else nl_j)
    lhs = jnp.concatenate([jnp.concatenate(_split2(nl_j), axis=1) for nl_j in nl], axis=0)
    cs = _dot_row_halves(_dot, lhs, u)
    att = [None] * nb
    for j in reversed(range(nb)):
        cs_j = cs[j * tq:(j + 1) * tq]
        att_j = jnp.exp2(log_beta[j] - cs_j - carry)
        att[j] = (jnp.where(masks[j], att_j, 0.0) if j in masks else att_j).astype(BF16)
        carry = carry + (cs_j[:, 0:1] + nl[j][:, 0:1])
    return _dot_row_halves(_dot, jnp.concatenate(att, axis=1), v16), carry


def _head_slices(ref):
    return [slice(i * HEAD_B, (i + 1) * HEAD_B) for i in range(ref.shape[1] // HEAD_B)]


def _sb_prompt_kernel(*refs, tq, nq):
    q_ref, z_ref, k_ref, v_ref, u_ref = refs[:5]
    o_ref, kout_ref, vout_ref = refs[-3:]
    kout_ref[...] = k_ref[...]
    vout_ref[...] = v_ref[...]
    row = lax.broadcasted_iota(jnp.int32, (tq, tq), 0)
    col = lax.broadcasted_iota(jnp.int32, (tq, tq), 1)
    diagonal = col < row
    k16 = k_ref[...].astype(BF16)
    v16 = v_ref[...].astype(BF16)
    for qi in range(nq):
        rows = slice(qi * tq, (qi + 1) * tq)
        n_keys = (qi + 1) * tq
        for s in _head_slices(q_ref):
            out, _ = _sb_group(q_ref[rows, s].astype(BF16), k16[0:n_keys, s], v16[0:n_keys, s], tq, u_ref[...],
                               {qi: diagonal}, jnp.zeros((tq, 1), F32))
            o_ref[rows, s] = (out * jax.nn.silu(z_ref[rows, s])).astype(o_ref.dtype)


def _sb_sample_kernel(*refs, tk):
    q_ref, z_ref, kn_ref, vn_ref, kp_ref, vp_ref, ud_ref, uf_ref = refs[:8]
    o_ref, kout_ref, vout_ref = refs[-3:]
    kout_ref[...] = kn_ref[...]
    vout_ref[...] = vn_ref[...]
    tq = q_ref.shape[0]
    row = lax.broadcasted_iota(jnp.int32, (tq, tq), 0)
    col = lax.broadcasted_iota(jnp.int32, (tq, tq), 1)
    heads = _head_slices(q_ref)
    n_past = kp_ref.shape[0] // len(heads)
    for hd, s in enumerate(heads):
        q16 = q_ref[:, s].astype(BF16)
        out_new, carry = _sb_group(q16, kn_ref[:, s].astype(BF16), vn_ref[:, s].astype(BF16), tq, ud_ref[...],
                                   {0: col < row}, jnp.zeros((tq, 1), F32))
        own_rows = pl.ds(hd, n_past, stride=len(heads))
        out_past, _ = _sb_group(q16, kp_ref[own_rows, :].astype(BF16), vp_ref[own_rows, :].astype(BF16), tk,
                                uf_ref[...], {}, carry)
        o_ref[:, s] = ((out_new + out_past) * jax.nn.silu(z_ref[:, s])).astype(o_ref.dtype)


def _suffix_matrix(tk):
    s_later = lax.broadcasted_iota(jnp.int32, (tk, tk), 0)
    s_here = lax.broadcasted_iota(jnp.int32, (tk, tk), 1)
    u = (s_later > s_here).astype(BF16)
    return jnp.concatenate([u, u], axis=0)


def stick_breaking(proj, q_col, z_col, k_col, v_col, past_k, past_v, nseq, t, heads, layer, depth, kv_out):
    hp = SB_HEADS_PER_STEP if past_k is None else heads
    wide = hp * HEAD_B
    tq = min(SB_BLOCK, t)
    seq = lambda rows, col: pl.BlockSpec((rows, wide), lambda b, h: (b, col // hp + h))
    const = lambda a: pl.BlockSpec(a.shape, lambda b, h: (0, 0))
    if past_k is None:
        u = _suffix_matrix(tq)
        body = functools.partial(_sb_prompt_kernel, tq=tq, nq=t // tq)
        in_specs = [seq(t, q_col), seq(t, z_col), seq(t, k_col), seq(t, v_col), const(u)]
        args = [proj, proj, proj, proj, u]
    else:
        assert t == tq
        n_past = past_k.shape[2]
        past = pl.BlockSpec((None, None, n_past * heads, HEAD_B), lambda b, h: (layer, b, 0, 0))
        past_k, past_v = (a.reshape(depth, nseq, n_past * heads, HEAD_B) for a in (past_k, past_v))
        ud, uf = _suffix_matrix(tq), _suffix_matrix(SB_BLOCK)
        body = functools.partial(_sb_sample_kernel, tk=SB_BLOCK)
        in_specs = [seq(t, q_col), seq(t, z_col), seq(t, k_col), seq(t, v_col), past, past, const(ud), const(uf)]
        args = [proj, proj, proj, proj, past_k, past_v, ud, uf]
    aliases = {}
    if kv_out is not None:
        aliases = {len(args): 1, len(args) + 1: 2}
        in_specs = in_specs + [pl.BlockSpec(memory_space=pl.ANY)] * 2
        args = args + list(kv_out)
    kv_shape = jax.ShapeDtypeStruct((depth, nseq * t, heads * HEAD_B), F32)
    kv_spec = pl.BlockSpec((None, t, wide), lambda b, h: (layer, b, h))
    o_b, k_all, v_all = pl.pallas_call(
        body,
        grid=(nseq, heads // hp),
        in_specs=in_specs,
        out_specs=[seq(t, 0), kv_spec, kv_spec],
        out_shape=[jax.ShapeDtypeStruct((nseq * t, heads * HEAD_B), BF16), kv_shape, kv_shape],
        input_output_aliases=aliases,
        compiler_params=_params("parallel", "parallel"),
        name="stick_breaking",
    )(*args)
    return o_b, (k_all, v_all)


def _block_stack(x, lane_masks):
    return jnp.concatenate([jnp.where(m, x, 0.0) for m in lane_masks], axis=0)


def _rwkv_kernel(*refs, chunk, width, n_cast):
    (x_ref, lora_ref, z_ref, sh_main_ref, sh_lora_ref, s0_ref,
     mu_main_ref, mu_lora_ref, w0_ref, wup_ref, a0_ref, aup_ref, kk_ref, ka_ref, rk_ref,
     gnw_ref, gnb_ref, e_ref, bd_ref, ltri_ref) = refs[:20]
    xbuf, lbuf, sbd = refs[-3:]
    o_ref, sout_ref, shm_out_ref, shl_out_ref = refs[20 + n_cast:24 + n_cast]
    for cast_in_ref, cast_out_ref in zip(refs[20:20 + n_cast], refs[24 + n_cast:24 + 2 * n_cast]):
        cast_out_ref[...] = cast_in_ref[...].astype(cast_out_ref.dtype)
    c = pl.program_id(1)
    n_chunks = pl.num_programs(1)
    C, W = chunk, width
    n_seq = x_ref.shape[0]
    n_slabs = W // SLAB
    n_heads = W // HEAD_A

    @pl.when(c == 0)
    def _init():
        sbd[...] = jnp.zeros_like(sbd)
        for s in range(n_seq):
            xbuf[s, 7:8, :] = sh_main_ref[s]
            lbuf[s, 7:8, :] = sh_lora_ref[s]
            for hd in range(n_heads):
                g, j = divmod(hd, GROUP_A)
                sbd[s, g, j * HEAD_A:(j + 1) * HEAD_A, j * HEAD_A:(j + 1) * HEAD_A] = s0_ref[s, hd]

    def shifted(ref, buf, mu_ref):
        rows = []
        for s in range(n_seq):
            x = ref[s]
            buf[s, 8:8 + C, :] = x
            rows.append(x + mu_ref[...] * (buf[s, 7:7 + C, :] - x))
            buf[s, 7:8, :] = x[C - 1:C, :]
        return jnp.concatenate(rows, axis=0)

    xs = shifted(x_ref, xbuf, mu_main_ref)
    lo_s = shifted(lora_ref, lbuf, mu_lora_ref)
    seq_rows = [slice(s * C, (s + 1) * C) for s in range(n_seq)]
    slabs = [slice(g * SLAB, (g + 1) * SLAB) for g in range(n_slabs)]

    r, k, v = xs[:, :W], xs[:, W:2 * W], xs[:, 2 * W:]
    w_pre = w0_ref[...] + _dot(jnp.tanh(lo_s).astype(BF16), wup_ref[...])
    ld = -math.exp(-0.5) * jax.nn.sigmoid(w_pre)
    a = jax.nn.sigmoid(a0_ref[...] + _dot(lo_s.astype(BF16), aup_ref[...]))

    e_mat = e_ref[...]

    segs = [slice(g * SEG_LANES, (g + 1) * SEG_LANES) for g in range(W // SEG_LANES)]

    def seg_sum(val):
        n = val.shape[0]
        hi, lo = _split2(val)
        out = _dot(jnp.concatenate([part[:, sl] for part in (hi, lo) for sl in segs], axis=0), e_mat)
        return jnp.concatenate([out[g * n:(g + 1) * n] + out[(len(segs) + g) * n:(len(segs) + g + 1) * n]
                                for g in range(len(segs))], axis=1)

    kk = k * kk_ref[...]
    kk = kk * lax.rsqrt(jnp.maximum(seg_sum(kk * kk), 1e-24))
    kmod = k * (1.0 + (a - 1.0) * ka_ref[...])

    ld_hi = ld.astype(BF16)
    ld_r1 = ld - ld_hi.astype(F32)
    ld_mid = ld_r1.astype(BF16)
    ld_lo = (ld_r1 - ld_mid.astype(F32)).astype(BF16)
    ltri = ltri_ref[...]
    lp = _dot(ltri, ld_hi) + _dot(ltri, ld_mid) + _dot(ltri, ld_lo)
    lp_last = [lp[rs.stop - 1:rs.stop, :] for rs in seq_rows]
    lp_end = jnp.concatenate([jnp.broadcast_to(row, (C, W)) for row in lp_last], axis=0)
    e_neg = jnp.exp(-lp)
    kka = kk * a
    kap = kk * jnp.exp(lp - ld)
    bet = kka * e_neg
    kt = kmod * e_neg
    rt = r * jnp.exp(lp)
    e_end = jnp.exp(lp_end - lp)
    kt_end = kmod * e_end
    bet_end = kka * e_end
    dec_end = [jnp.exp(row) for row in lp_last]

    lane = lax.broadcasted_iota(jnp.int32, (1, SLAB), 1)
    lane_masks = [(lane >= j * HEAD_A) & (lane < (j + 1) * HEAD_A) for j in range(GROUP_A)]
    t_row = lax.broadcasted_iota(jnp.int32, (C, GROUP_A * C), 0)
    s_col = lax.broadcasted_iota(jnp.int32, (C, GROUP_A * C), 1) & (C - 1)
    strict, incl = s_col < t_row, s_col <= t_row
    stack = lambda val16: _block_stack(val16, lane_masks)
    nh = GROUP_A * C
    units = [(s, g) for s in range(n_seq) for g in range(n_slabs)]
    per_slab = lambda fn: [fn(u, seq_rows[s], slabs[g]) for u, (s, g) in enumerate(units)]

    kap16, rt16, bet16, kt16, v16 = (val.astype(BF16) for val in (kap, rt, bet, kt, v))
    s_old = [sbd[s, g] for s, g in units]
    lhs = per_slab(lambda g, rs, sl: jnp.concatenate([kap16[rs, sl], rt16[rs, sl]], axis=0))
    rhs = per_slab(lambda g, rs, sl: jnp.concatenate([stack(bet16[rs, sl]), stack(kt16[rs, sl])], axis=0))
    sc = per_slab(lambda g, rs, sl: _dot_nt(lhs[g], rhs[g]))
    ls = per_slab(lambda g, rs, sl: _dot_nt(lhs[g], s_old[g].astype(BF16)))
    v_stack = per_slab(lambda g, rs, sl: stack(v16[rs, sl]))
    p16 = per_slab(lambda g, rs, sl: jnp.where(strict, -sc[g][:C, :nh], 0.0).astype(BF16))
    xw = per_slab(lambda g, rs, sl: ls[g][:C]
                  + _dot(jnp.where(strict, sc[g][:C, nh:], 0.0).astype(BF16), v_stack[g]))
    for step in range(6):
        xw = per_slab(lambda g, rs, sl: xw[g] + _dot(p16[g], stack(xw[g].astype(BF16))))
        if step < 5:
            p16 = per_slab(lambda g, rs, sl: _dot(p16[g], stack(p16[g])).astype(BF16))
    ab_inc = per_slab(lambda g, rs, sl: jnp.concatenate(
        [jnp.where(incl, sc[g][C:, :nh], 0.0), jnp.where(incl, sc[g][C:, nh:], 0.0)], axis=1).astype(BF16))
    y = per_slab(lambda g, rs, sl: ls[g][C:] + _dot(
        ab_inc[g], jnp.concatenate([stack((-xw[g]).astype(BF16)), v_stack[g]], axis=0)))
    upd = per_slab(lambda g, rs, sl: _dot(
        jnp.concatenate([v[rs, sl], -xw[g]], axis=0).T.astype(BF16),
        jnp.concatenate([kt_end[rs, sl], bet_end[rs, sl]], axis=0).astype(BF16)))
    for u, (s, g) in enumerate(units):
        sbd[s, g] = s_old[u] * dec_end[s][:, slabs[g]] + upd[u] * bd_ref[...]

    y = jnp.concatenate([jnp.concatenate(y[s * n_slabs:(s + 1) * n_slabs], axis=1) for s in range(n_seq)], axis=0)
    inv_n = 1.0 / HEAD_A
    mean = seg_sum(y) * inv_n
    yc = y - mean
    var = seg_sum(yc * yc) * inv_n
    y = yc * lax.rsqrt(var + GN_EPS) * gnw_ref[...] + gnb_ref[...]
    y = y + seg_sum(r * kmod * rk_ref[...]) * v
    for s, rs in enumerate(seq_rows):
        o_ref[s] = (y[rs] * jax.nn.silu(z_ref[s])).astype(o_ref.dtype)

    @pl.when(c == n_chunks - 1)
    def _fin():
        for s in range(n_seq):
            shm_out_ref[s] = xbuf[s, 7:8, :]
            shl_out_ref[s] = lbuf[s, 7:8, :]
            for hd in range(n_heads):
                g, j = divmod(hd, GROUP_A)
                sout_ref[s, hd] = sbd[s, g, j * HEAD_A:(j + 1) * HEAD_A, j * HEAD_A:(j + 1) * HEAD_A]


RWKV_SEQS_PER_STEP = 2


def rwkv7(proj, lora, z_col, shift_main, shift_lora, s0, prm, nseq, t, cast_f32=()):
    W = prm["a_w0"].shape[0]
    C = RWKV_CHUNK
    S = RWKV_SEQS_PER_STEP
    nc = t // C
    n_heads = W // HEAD_A
    lw = lora.shape[1]
    half = lw // 2
    zpad = jnp.zeros((half, W), F32)
    wup = jnp.concatenate([prm["a_w_up"], zpad], axis=0).astype(BF16)
    aup = jnp.concatenate([zpad, prm["a_a_up"]], axis=0).astype(BF16)
    same_head = lambda n: (jnp.arange(n)[:, None] // HEAD_A) == (jnp.arange(n)[None, :] // HEAD_A)
    ti = jnp.arange(S * C)
    ltri = ((ti[:, None] >= ti[None, :]) & (ti[:, None] // C == ti[None, :] // C)).astype(BF16)
    row1 = lambda a: a.reshape(1, -1)
    vec = lambda n: pl.BlockSpec((1, n), lambda b, c: (0, 0))
    full = lambda a: pl.BlockSpec(a.shape, lambda b, c: (0, 0))
    rows = lambda n, col: pl.BlockSpec((S, C, n), lambda b, c: (b, c, col))
    per_seq = lambda n: pl.BlockSpec((S, 1, n), lambda b, c: (b, 0, 0))
    state = pl.BlockSpec((S, n_heads, HEAD_A, HEAD_A), lambda b, c: (b, 0, 0, 0))
    e_mat, bd_mask = same_head(SEG_LANES).astype(BF16), same_head(SLAB).astype(F32)
    proj3 = proj.reshape(nseq, t, proj.shape[1])
    in_specs = [rows(3 * W, 0), rows(lw, 0), rows(W, z_col),
                per_seq(3 * W), per_seq(lw), state,
                vec(3 * W), vec(lw), vec(W), full(wup), vec(W), full(aup), vec(W), vec(W), vec(W), vec(W), vec(W),
                full(e_mat), full(bd_mask), full(ltri)]
    args = [proj3, lora.reshape(nseq, t, lw), proj3, shift_main, shift_lora, s0,
            row1(prm["a_mu"][:3 * W]), row1(prm["a_mu"][3 * W:]), row1(prm["a_w0"]), wup, row1(prm["a_a0"]), aup,
            row1(prm["a_k_k"]), row1(prm["a_k_a"]), row1(prm["a_r_k"]), row1(prm["a_gn_w"]), row1(prm["a_gn_b"]),
            e_mat, bd_mask, ltri]
    out_specs = [rows(W, 0), state, per_seq(3 * W), per_seq(lw)]
    out_shape = [jax.ShapeDtypeStruct((nseq, t, W), BF16),
                 jax.ShapeDtypeStruct((nseq, n_heads, HEAD_A, HEAD_A), F32),
                 jax.ShapeDtypeStruct((nseq, 1, 3 * W), F32),
                 jax.ShapeDtypeStruct((nseq, 1, lw), F32)]
    steps = (nseq // S) * nc
    for arr in cast_f32:
        flat = arr.reshape(-1, arr.shape[-1])
        slab = pl.BlockSpec((flat.shape[0] // steps, flat.shape[1]), lambda b, c: (b * nc + c, 0))
        in_specs, args = in_specs + [slab], args + [flat]
        out_specs, out_shape = out_specs + [slab], out_shape + [jax.ShapeDtypeStruct(flat.shape, BF16)]
    o_a, s_new, sh_main, sh_lora, *cast16 = pl.pallas_call(
        functools.partial(_rwkv_kernel, chunk=C, width=W, n_cast=len(cast_f32)),
        grid=(nseq // S, nc),
        in_specs=in_specs,
        out_specs=out_specs,
        out_shape=out_shape,
        scratch_shapes=[pltpu.VMEM((S, C + 8, 3 * W), F32), pltpu.VMEM((S, C + 8, lw), F32),
                        pltpu.VMEM((S, W // SLAB, SLAB, SLAB), F32)],
        compiler_params=_params("parallel", "arbitrary"),
        name="rwkv7",
    )(*args)
    cast16 = [c16.reshape(arr.shape) for c16, arr in zip(cast16, cast_f32)]
    return o_a.reshape(nseq * t, W), s_new, sh_main, sh_lora, cast16


COL_AZ, COL_BQ, COL_BK, COL_BV, COL_BZ, COL_CU, COL_MQ = 3, 4, 5, 6, 7, 8, 11


def trunk_layer(x, h, nseq, t, mk, mv, shift_prev, s_prev, past_k, past_v, prm, layer, depth, kv_out, want_vn):
    m, d = x.shape
    W = d // 4
    tm = min(512, m)
    if h is None:
        h = rmsnorm_rows(x, prm["g_pre"], min(256, m))
    proj = matmul(h, prm["w_in_main"], layer, min(1024, m), 1024)
    lora = matmul(h, prm["w_in_lora"], layer, min(1024, m), prm["w_in_lora"].shape[2])

    to_cast = [name for name in ("w_gate", "w_br", "w_out") if prm[name].dtype == F32]
    o_a, s_new, sh_main, sh_lora, cast16 = rwkv7(
        proj, lora, COL_AZ, shift_prev[..., :3 * W], shift_prev[..., 3 * W:], s_prev, prm, nseq, t,
        cast_f32=[prm[name] for name in to_cast])
    converted = dict(zip(to_cast, cast16))
    weight = lambda name: converted.get(name, prm[name])
    shift_new = jnp.concatenate([sh_main, sh_lora], axis=-1)

    heads_b = W // HEAD_B
    hb = lambda col: col * heads_b
    o_b, kv_out = stick_breaking(proj, hb(COL_BQ), hb(COL_BZ), hb(COL_BK), hb(COL_BV), past_k, past_v,
                                 nseq, t, heads_b, layer, depth, kv_out)

    clen = min(t, prm["c_ws"].shape[1])
    groups = prm["c_ws"].shape[0]
    bs_full = jnp.repeat(prm["c_bs"][:, :clen].T, W // groups, axis=1)
    o_c, *vn_c = chunk_mlp(proj, COL_CU, prm["c_ws"][:, :clen, :clen], bs_full, prm["c_ln_w"], prm["c_ln_b"],
                           m, clen, max(1, min(512, t) // clen), want_vn)

    o_m = memory_attention(proj, COL_MQ, mk, mv, nseq, t, min(512, t))

    merged = gated_merge(h, (o_a, o_b, o_c, o_m), weight("w_gate"), prm["b_gate"], weight("w_br"), layer, tm,
                         512 if m > tm else 256)
    x_new, h_next = out_proj_norm_residual(merged, weight("w_out"), layer, x, prm["g_post"], prm["g_pre_next"], tm,
                                           512)
    return x_new, h_next, shift_new, s_new, kv_out, vn_c, converted


def kernel(x_prompt, x_sample, cache_mem_k, cache_mem_v, cache_sb_k, cache_sb_v, state_rwkv, state_shift, mem_prompt, g_pre, g_post, w_in, a_mu, a_w0, a_w_up, a_a0, a_a_up, a_k_k, a_k_a, a_r_k, a_gn_w, a_gn_b, c_ws, c_bs, c_ln_w, c_ln_b, g_mem, w_mem_kv, w_gate, b_gate, w_br, w_out):
    bp, tp, d = x_prompt.shape
    bs_, ts, _ = x_sample.shape
    depth = w_in.shape[0]
    W = d // 4
    n_mem = mem_prompt.shape[1]
    m_heads = cache_mem_k.shape[3]
    b_heads = cache_sb_k.shape[3]
    a_heads = state_rwkv.shape[2]
    shift_w = state_shift.shape[-1]
    n_past = cache_sb_k.shape[2]
    lora_lo, lora_hi = 3 * W, shift_w

    yp = x_prompt.reshape(bp * tp, d)
    ys = x_sample.reshape(bs_ * ts, d)
    mem_rows = mem_prompt.reshape(bp * n_mem, d)
    mem_out, rw_p, sh_p, rw_s, sh_s, cv_s = [[], []], [], [], [], [], []
    kv_p = kv_s = hp = hs = None
    dense = {
        "w_in_main": drop_cols_cast(w_in, lora_lo, lora_hi, W, 1024),
        "w_in_lora": w_in[:, :, lora_lo:lora_hi].astype(BF16),
        "w_gate": w_gate, "w_br": w_br, "w_out": w_out,
    }
    w_mem16 = w_mem_kv.astype(BF16)
    for l in range(depth):
        prm = dict(dense)
        prm.update({
            "g_pre": g_pre[l], "g_post": g_post[l], "g_pre_next": g_pre[l + 1] if l + 1 < depth else None,
            "a_mu": a_mu[l], "a_w0": a_w0[l], "a_w_up": a_w_up[l], "a_a0": a_a0[l], "a_a_up": a_a_up[l],
            "a_k_k": a_k_k[l].reshape(-1), "a_k_a": a_k_a[l].reshape(-1), "a_r_k": a_r_k[l].reshape(-1),
            "a_gn_w": a_gn_w[l].reshape(-1), "a_gn_b": a_gn_b[l].reshape(-1),
            "c_ws": c_ws[l], "c_bs": c_bs[l], "c_ln_w": c_ln_w[l], "c_ln_b": c_ln_b[l], "b_gate": b_gate[l],
        })
        kv = matmul(rmsnorm_rows(mem_rows, g_mem[l], 256), w_mem16, l, min(1024, bp * n_mem), 1024)
        mk = kv[:, :W].reshape(bp, n_mem, W)
        mv = kv[:, W:].reshape(bp, n_mem, W)
        shift0 = jnp.zeros((bp, 1, shift_w), F32)
        s0 = jnp.zeros((bp, a_heads, HEAD_A, HEAD_A), F32)
        yp, hp, sh, st, kv_p, _, converted = trunk_layer(
            yp, hp, bp, tp, mk.astype(BF16), mv.astype(BF16), shift0, s0, None, None, prm, l, depth, kv_p, False)
        dense.update(converted)
        prm.update(converted)
        mem_out[0].append(mk.reshape(bp, n_mem, m_heads, W // m_heads))
        mem_out[1].append(mv.reshape(bp, n_mem, m_heads, W // m_heads))
        rw_p.append(st)
        sh_p.append(sh)
        ys, hs, sh, st, kv_s, cvn, _ = trunk_layer(
            ys, hs, bs_, ts, cache_mem_k[l].reshape(bs_, n_mem, W).astype(BF16),
            cache_mem_v[l].reshape(bs_, n_mem, W).astype(BF16), state_shift[l], state_rwkv[l],
            cache_sb_k, cache_sb_v, prm, l, depth, kv_s, True)
        rw_s.append(st)
        sh_s.append(sh)
        cv_s.append(cvn[0].reshape(bs_, ts, W))
    heads_p = lambda a: a.reshape(depth, bp, tp, b_heads, HEAD_B)
    heads_s = lambda a: a.reshape(depth, bs_, ts, b_heads, HEAD_B)
    return (yp.reshape(bp, tp, d), ys.reshape(bs_, ts, d), jnp.stack(mem_out[0]), jnp.stack(mem_out[1]),
            heads_p(kv_p[0]), heads_p(kv_p[1]), jnp.stack(rw_p), jnp.stack(sh_p),
            heads_s(kv_s[0]), heads_s(kv_s[1]), jnp.stack(rw_s), jnp.stack(sh_s), jnp.stack(cv_s))
```

```python
import functools
import math

import jax
import jax.numpy as jnp
from jax import lax
from jax.experimental import pallas as pl
from jax.experimental.pallas import tpu as pltpu

F32 = jnp.float32
BF16 = jnp.bfloat16

NORM_EPS = 1e-6
GN_EPS = 64e-5
HEAD_A = 64
GROUP_A = 2
SLAB = HEAD_A * GROUP_A
SEG_LANES = 256
HEAD_B = 128
HEAD_M = 256
GROUP_C = 128
RWKV_CHUNK = 64
VMEM_LIMIT_BYTES = 56 * 1024 * 1024


def _params(*sem):
    return pltpu.CompilerParams(dimension_semantics=sem, vmem_limit_bytes=VMEM_LIMIT_BYTES)


def _split2(x):
    hi = x.astype(BF16)
    lo = (x - hi.astype(F32)).astype(BF16)
    return hi, lo


def _dot(a, b):
    return jnp.dot(a, b, preferred_element_type=F32)


def _dot_nt(a, b):
    return lax.dot_general(a, b, (((1,), (1,)), ((), ())), preferred_element_type=F32)


def _dot_row_halves(dot, a, b):
    half = a.shape[0] // 2
    return jnp.concatenate([dot(a[:half], b), dot(a[half:], b)], axis=0)


def _rmsnorm_kernel(x_ref, g_ref, o_ref):
    x = x_ref[...]
    ms = jnp.mean(x * x, axis=-1, keepdims=True)
    o_ref[...] = (x * lax.rsqrt(ms + NORM_EPS) * g_ref[...]).astype(o_ref.dtype)


def rmsnorm_rows(x, g, tm):
    m, d = x.shape
    return pl.pallas_call(
        _rmsnorm_kernel,
        grid=(m // tm,),
        in_specs=[pl.BlockSpec((tm, d), lambda i: (i, 0)), pl.BlockSpec((1, d), lambda i: (0, 0))],
        out_specs=pl.BlockSpec((tm, d), lambda i: (i, 0)),
        out_shape=jax.ShapeDtypeStruct((m, d), BF16),
        compiler_params=_params("parallel"),
        name="rmsnorm_rows",
    )(x, g.reshape(1, d))


POST_ROWS = 64


def _out_proj_kernel(*refs, tn, with_next):
    if with_next:
        m_ref, w_ref, x_ref, g_ref, gn_ref, o_ref, hn_ref, x_rows, ssq = refs
    else:
        m_ref, w_ref, x_ref, g_ref, o_ref, x_rows, ssq = refs
    j = pl.program_id(1)
    cols = pl.ds(pl.multiple_of(j * tn, tn), tn)
    y = _dot(m_ref[...], w_ref[...])
    sq = jnp.sum(y * y, axis=-1, keepdims=True)
    o_ref[:, cols] = y * g_ref[...]
    x_rows[:, cols] = x_ref[...]

    @pl.when(j == 0)
    def _first():
        ssq[...] = sq

    @pl.when(j > 0)
    def _rest():
        ssq[...] += sq

    @pl.when(j == pl.num_programs(1) - 1)
    def _normalise():
        inv_d = 1.0 / o_ref.shape[1]
        for r0 in range(0, o_ref.shape[0], POST_ROWS):
            rows = slice(r0, r0 + POST_ROWS)
            scale = lax.rsqrt(ssq[rows, :] * inv_d + NORM_EPS)
            x_new = x_rows[rows, :] + o_ref[rows, :] * scale
            o_ref[rows, :] = x_new
            if with_next:
                ms = jnp.mean(x_new * x_new, axis=-1, keepdims=True)
                hn_ref[rows, :] = (x_new * lax.rsqrt(ms + NORM_EPS) * gn_ref[...]).astype(hn_ref.dtype)


def out_proj_norm_residual(merged, w, layer, x, g, g_next, tm, tn):
    m, d = x.shape
    with_next = g_next is not None
    row = pl.BlockSpec((tm, d), lambda i, j: (i, 0))
    gain = pl.BlockSpec((1, d), lambda i, j: (0, 0))
    out = pl.pallas_call(
        functools.partial(_out_proj_kernel, tn=tn, with_next=with_next),
        grid=(m // tm, d // tn),
        in_specs=[row, pl.BlockSpec((None, d, tn), lambda i, j: (layer, 0, j)),
                  pl.BlockSpec((tm, tn), lambda i, j: (i, j)), pl.BlockSpec((1, tn), lambda i, j: (0, j))]
                 + ([gain] if with_next else []),
        out_specs=[row, row] if with_next else [row],
        out_shape=[jax.ShapeDtypeStruct((m, d), F32)] + ([jax.ShapeDtypeStruct((m, d), BF16)] if with_next else []),
        scratch_shapes=[pltpu.VMEM((tm, d), F32), pltpu.VMEM((tm, 1), F32)],
        compiler_params=_params("parallel", "arbitrary"),
        name="out_proj_norm_residual",
    )(merged, w, x, g.reshape(1, d), *([g_next.reshape(1, d)] if with_next else []))
    return out if with_next else (out[0], None)


def _drop_cols_kernel(a_ref, b_ref, o_ref, *, first_shifted, gap):
    j = pl.program_id(2)

    @pl.when(j < first_shifted)
    def _before_gap():
        o_ref[...] = a_ref[...].astype(o_ref.dtype)

    @pl.when(j >= first_shifted)
    def _after_gap():
        o_ref[...] = jnp.concatenate([a_ref[:, gap:], b_ref[...]], axis=1).astype(o_ref.dtype)


def drop_cols_cast(w, n_layers, lo, hi, cw, tk):
    _, k, n = w.shape
    depth = n_layers
    gap = hi - lo
    n_out = n - gap
    return pl.pallas_call(
        functools.partial(_drop_cols_kernel, first_shifted=lo // cw, gap=gap),
        grid=(depth, k // tk, n_out // cw),
        in_specs=[pl.BlockSpec((None, tk, cw), lambda l, i, j: (l, i, j)),
                  pl.BlockSpec((None, tk, gap), lambda l, i, j: (l, i, (j + 1) * (cw // gap)))],
        out_specs=pl.BlockSpec((None, tk, cw), lambda l, i, j: (l, i, j)),
        out_shape=jax.ShapeDtypeStruct((depth, k, n_out), BF16),
        compiler_params=_params("parallel", "parallel", "parallel"),
        name="drop_cols_cast",
    )(w, w)
def _matmul_kernel(x_ref, w_ref, o_ref):
    o_ref[...] = _dot(x_ref[...], w_ref[...]).astype(o_ref.dtype)


def matmul(x, w, layer, tm, tn, out_dtype=F32):
    m, k = x.shape
    n = w.shape[2]
    return pl.pallas_call(
        _matmul_kernel,
        grid=(n // tn, m // tm),
        in_specs=[pl.BlockSpec((tm, k), lambda j, i: (i, 0)),
                  pl.BlockSpec((None, k, tn), lambda j, i: (layer, 0, j))],
        out_specs=pl.BlockSpec((tm, tn), lambda j, i: (i, j)),
        out_shape=jax.ShapeDtypeStruct((m, n), out_dtype),
        compiler_params=_params("parallel", "parallel"),
        name="matmul",
    )(x, w)


def _merge_kernel(h_ref, oa_ref, ob_ref, oc_ref, om_ref, wg_ref, bg_ref, wbr_ref, out_ref):
    h = h_ref[...]
    acc = None
    for n, o_ref in enumerate((oa_ref, ob_ref, oc_ref, om_ref)):
        gate = jax.nn.sigmoid(_dot(h, wg_ref[n]) + bg_ref[n])
        term = gate * _dot(o_ref[...], wbr_ref[n])
        acc = term if acc is None else acc + term
    out_ref[...] = acc.astype(out_ref.dtype)


def gated_merge(h, branches, wg, bg, wbr, layer, tm, tn):
    m, d = h.shape
    _, nb, w, _ = wbr.shape
    resident = dict(pipeline_mode=pl.Buffered(1)) if m > tm else {}
    o_spec = pl.BlockSpec((tm, w), lambda j, i: (i, 0))
    return pl.pallas_call(
        _merge_kernel,
        grid=(d // tn, m // tm),
        in_specs=[pl.BlockSpec((tm, d), lambda j, i: (i, 0)), o_spec, o_spec, o_spec, o_spec,
                  pl.BlockSpec((None, nb, d, tn), lambda j, i: (layer, 0, 0, j), **resident),
                  pl.BlockSpec((nb, 1, tn), lambda j, i: (0, 0, j)),
                  pl.BlockSpec((None, nb, w, tn), lambda j, i: (layer, 0, 0, j), **resident)],
        out_specs=pl.BlockSpec((tm, tn), lambda j, i: (i, j)),
        out_shape=jax.ShapeDtypeStruct((m, d), BF16),
        compiler_params=_params("parallel", "parallel"),
        name="gated_merge",
    )(h, *branches, wg, bg.reshape(nb, 1, d), wbr)


def _memattn_kernel(q_ref, mk_ref, mv_ref, o_ref, *, heads):
    scale = 1.0 / math.sqrt(HEAD_M)
    for hd in range(heads):
        sl = slice(hd * HEAD_M, (hd + 1) * HEAD_M)
        q = q_ref[:, sl].astype(BF16)
        s = _dot_nt(q, mk_ref[0, :, sl]) * scale
        s = s - jnp.max(s, axis=-1, keepdims=True)
        p = jnp.exp(s)
        p = p * (1.0 / jnp.sum(p, axis=-1, keepdims=True))
        o_ref[:, sl] = _dot(p.astype(BF16), mv_ref[0, :, sl]).astype(o_ref.dtype)


def memory_attention(proj, q_col, mk, mv, nseq, t, tq):
    n_mem, w = mk.shape[1:]
    per_seq = t // tq
    mem_spec = pl.BlockSpec((1, n_mem, w), lambda i: (i // per_seq, 0, 0))
    return pl.pallas_call(
        functools.partial(_memattn_kernel, heads=w // HEAD_M),
        grid=(nseq * per_seq,),
        in_specs=[pl.BlockSpec((tq, w), lambda i: (i, q_col)), mem_spec, mem_spec],
        out_specs=pl.BlockSpec((tq, w), lambda i: (i, 0)),
        out_shape=jax.ShapeDtypeStruct((nseq * t, w), BF16),
        compiler_params=_params("parallel"),
        name="memory_attention",
    )(proj, mk, mv)


def _cmlp_kernel(u_ref, v_ref, z_ref, ws_ref, bs_ref, lnw_ref, lnb_ref, o_ref, *maybe_vn_ref, clen, chunks):
    v = v_ref[...]
    mu = jnp.mean(v, axis=-1, keepdims=True)
    var = jnp.mean(jnp.square(v - mu), axis=-1, keepdims=True)
    vn = (v - mu) * lax.rsqrt(var + NORM_EPS) * lnw_ref[...] + lnb_ref[...]
    for vn_ref in maybe_vn_ref:
        vn_ref[...] = vn
    vn16 = vn.astype(BF16)
    groups = vn.shape[1] // GROUP_C
    row = lax.broadcasted_iota(jnp.int32, (clen, clen), 0)
    col = lax.broadcasted_iota(jnp.int32, (clen, clen), 1)
    for g in range(groups):
        wm = jnp.where(row >= col, ws_ref[g], 0.0).astype(BF16)
        cs = slice(g * GROUP_C, (g + 1) * GROUP_C)
        for c in range(chunks):
            rs = slice(c * clen, (c + 1) * clen)
            s = _dot(wm, vn16[rs, cs]) + bs_ref[:, cs]
            o_ref[rs, cs] = (u_ref[rs, cs] * s * jax.nn.silu(z_ref[rs, cs])).astype(o_ref.dtype)


def chunk_mlp(proj, u_col, ws, bs_full, ln_w, ln_b, rows, clen, chunks, want_vn):
    w = ln_w.shape[0]
    tm = clen * chunks
    col = lambda c: pl.BlockSpec((tm, w), lambda i: (i, c))
    const2 = lambda a: pl.BlockSpec(a.shape, lambda i: (0, 0))
    out = pl.BlockSpec((tm, w), lambda i: (i, 0))
    out_dtypes = [BF16, F32] if want_vn else [BF16]
    return pl.pallas_call(
        functools.partial(_cmlp_kernel, clen=clen, chunks=chunks),
        grid=(rows // tm,),
        in_specs=[col(u_col), col(u_col + 1), col(u_col + 2),
                  pl.BlockSpec(ws.shape, lambda i: (0, 0, 0)), const2(bs_full),
                  pl.BlockSpec((1, w), lambda i: (0, 0)), pl.BlockSpec((1, w), lambda i: (0, 0))],
        out_specs=[out] * len(out_dtypes),
        out_shape=[jax.ShapeDtypeStruct((rows, w), dt) for dt in out_dtypes],
        compiler_params=_params("parallel"),
        name="chunk_mlp",
    )(proj, proj, proj, ws, bs_full, ln_w.reshape(1, w), ln_b.reshape(1, w))


SB_HEADS_PER_STEP = 2
SB_BLOCK = 256


def _sb_group(q16, k16, v16, tk, u, masks, carry):
    tq = q16.shape[0]
    nb = k16.shape[0] // tk
    z_all = _dot_nt(q16, k16)
    nl, log_beta = [], []
    for j in range(nb):
        z2 = z_all[:, j * tk:(j + 1) * tk] * (math.log2(math.e) / math.sqrt(HEAD_B))
        neg_abs = lax.bitcast_convert_type(lax.bitcast_convert_type(z2, jnp.uint32) | jnp.uint32(0x80000000), F32)
        nl_j = jnp.maximum(z2, 0.0) + jnp.log2(1.0 + jnp.exp2(neg_abs))
        log_beta.append(z2 - nl_j)
        nl.append(jnp.where(masks[j], nl_j, 0.0) if j in masks else nl_j)
    lhs = jnp.concatenate([jnp.concatenate(_split2(nl_j), axis=1) for nl_j in nl], axis=0)
    cs = _dot_row_halves(_dot, lhs, u)
    att = [None] * nb
    for j in reversed(range(nb)):
        cs_j = cs[j * tq:(j + 1) * tq]
        att_j = jnp.exp2(log_beta[j] - cs_j - carry)
        att[j] = (jnp.where(masks[j], att_j, 0.0) if j in masks else att_j).astype(BF16)
        carry = carry + (cs_j[:, 0:1] + nl[j][:, 0:1])
    return _dot_row_halves(_dot, jnp.concatenate(att, axis=1), v16), carry


def _head_slices(ref):
    return [slice(i * HEAD_B, (i + 1) * HEAD_B) for i in range(ref.shape[1] // HEAD_B)]


def _sb_prompt_kernel(*refs, tq, nq):
    q_ref, z_ref, k_ref, v_ref, u_ref = refs[:5]
    o_ref, kout_ref, vout_ref = refs[-3:]
    kout_ref[...] = k_ref[...]
    vout_ref[...] = v_ref[...]
    row = lax.broadcasted_iota(jnp.int32, (tq, tq), 0)
    col = lax.broadcasted_iota(jnp.int32, (tq, tq), 1)
    diagonal = col < row
    k16 = k_ref[...].astype(BF16)
    v16 = v_ref[...].astype(BF16)
    for qi in range(nq):
        rows = slice(qi * tq, (qi + 1) * tq)
        n_keys = (qi + 1) * tq
        for s in _head_slices(q_ref):
            out, _ = _sb_group(q_ref[rows, s].astype(BF16), k16[0:n_keys, s], v16[0:n_keys, s], tq, u_ref[...],
                               {qi: diagonal}, jnp.zeros((tq, 1), F32))
            o_ref[rows, s] = (out * jax.nn.silu(z_ref[rows, s])).astype(o_ref.dtype)


def _sb_sample_kernel(*refs, tk):
    q_ref, z_ref, kn_ref, vn_ref, kp_ref, vp_ref, ud_ref, uf_ref = refs[:8]
    o_ref, kout_ref, vout_ref = refs[-3:]
    kout_ref[...] = kn_ref[...]
    vout_ref[...] = vn_ref[...]
    tq = q_ref.shape[0]
    row = lax.broadcasted_iota(jnp.int32, (tq, tq), 0)
    col = lax.broadcasted_iota(jnp.int32, (tq, tq), 1)
    heads = _head_slices(q_ref)
    n_past = kp_ref.shape[0] // len(heads)
    for hd, s in enumerate(heads):
        q16 = q_ref[:, s].astype(BF16)
        out_new, carry = _sb_group(q16, kn_ref[:, s].astype(BF16), vn_ref[:, s].astype(BF16), tq, ud_ref[...],
                                   {0: col < row}, jnp.zeros((tq, 1), F32))
        own_rows = pl.ds(hd, n_past, stride=len(heads))
        out_past, _ = _sb_group(q16, kp_ref[own_rows, :].astype(BF16), vp_ref[own_rows, :].astype(BF16), tk,
                                uf_ref[...], {}, carry)
        o_ref[:, s] = ((out_new + out_past) * jax.nn.silu(z_ref[:, s])).astype(o_ref.dtype)


def _suffix_matrix(tk):
    s_later = lax.broadcasted_iota(jnp.int32, (tk, tk), 0)
    s_here = lax.broadcasted_iota(jnp.int32, (tk, tk), 1)
    u = (s_later > s_here).astype(BF16)
    return jnp.concatenate([u, u], axis=0)


def stick_breaking(proj, q_col, z_col, k_col, v_col, past_k, past_v, nseq, t, heads, layer, depth, kv_out):
    hp = SB_HEADS_PER_STEP if past_k is None else heads
    wide = hp * HEAD_B
    tq = min(SB_BLOCK, t)
    seq = lambda rows, col: pl.BlockSpec((rows, wide), lambda b, h: (b, col // hp + h))
    const = lambda a: pl.BlockSpec(a.shape, lambda b, h: (0, 0))
    if past_k is None:
        u = _suffix_matrix(tq)
        body = functools.partial(_sb_prompt_kernel, tq=tq, nq=t // tq)
        in_specs = [seq(t, q_col), seq(t, z_col), seq(t, k_col), seq(t, v_col), const(u)]
        args = [proj, proj, proj, proj, u]
    else:
        assert t == tq
        n_past = past_k.shape[2]
        past = pl.BlockSpec((None, None, n_past * heads, HEAD_B), lambda b, h: (layer, b, 0, 0))
        past_k, past_v = (a.reshape(depth, nseq, n_past * heads, HEAD_B) for a in (past_k, past_v))
        ud, uf = _suffix_matrix(tq), _suffix_matrix(SB_BLOCK)
        body = functools.partial(_sb_sample_kernel, tk=SB_BLOCK)
        in_specs = [seq(t, q_col), seq(t, z_col), seq(t, k_col), seq(t, v_col), past, past, const(ud), const(uf)]
        args = [proj, proj, proj, proj, past_k, past_v, ud, uf]
    aliases = {}
    if kv_out is not None:
        aliases = {len(args): 1, len(args) + 1: 2}
        in_specs = in_specs + [pl.BlockSpec(memory_space=pl.ANY)] * 2
        args = args + list(kv_out)
    kv_shape = jax.ShapeDtypeStruct((depth, nseq * t, heads * HEAD_B), F32)
    kv_spec = pl.BlockSpec((None, t, wide), lambda b, h: (layer, b, h))
    o_b, k_all, v_all = pl.pallas_call(
        body,
        grid=(nseq, heads // hp),
        in_specs=in_specs,
        out_specs=[seq(t, 0), kv_spec, kv_spec],
        out_shape=[jax.ShapeDtypeStruct((nseq * t, heads * HEAD_B), BF16), kv_shape, kv_shape],
        input_output_aliases=aliases,
        compiler_params=_params("parallel", "parallel"),
        name="stick_breaking",
    )(*args)
    return o_b, (k_all, v_all)


def _block_stack(x, lane_masks):
    return jnp.concatenate([jnp.where(m, x, 0.0) for m in lane_masks], axis=0)


def _rwkv_kernel(*refs, chunk, width, n_cast, drop):
    (x_ref, lora_ref, z_ref, sh_main_ref, sh_lora_ref, s0_ref,
     mu_main_ref, mu_lora_ref, w0_ref, wup_ref, a0_ref, aup_ref, kk_ref, ka_ref, rk_ref,
     gnw_ref, gnb_ref, e_ref, bd_ref, ltri_ref) = refs[:20]
    xbuf, lbuf, sbd = refs[-3:]
    o_ref, sout_ref, shm_out_ref, shl_out_ref = refs[20 + n_cast:24 + n_cast]
    cast_pairs = list(zip(refs[20:20 + n_cast], refs[24 + n_cast:24 + 2 * n_cast]))
    for i, (cast_in_ref, cast_out_ref) in enumerate(cast_pairs):
        val = cast_in_ref[...]
        if drop is not None and i == n_cast - 1:
            val = jnp.concatenate([val[:, :drop[0]], val[:, drop[1]:]], axis=1)
        cast_out_ref[...] = val.astype(cast_out_ref.dtype)
    c = pl.program_id(1)
    n_chunks = pl.num_programs(1)
    C, W = chunk, width
    n_seq = x_ref.shape[0]
    n_slabs = W // SLAB
    n_heads = W // HEAD_A

    @pl.when(c == 0)
    def _init():
        sbd[...] = jnp.zeros_like(sbd)
        for s in range(n_seq):
            xbuf[s, 7:8, :] = sh_main_ref[s]
            lbuf[s, 7:8, :] = sh_lora_ref[s]
            for hd in range(n_heads):
                g, j = divmod(hd, GROUP_A)
                sbd[s, g, j * HEAD_A:(j + 1) * HEAD_A, j * HEAD_A:(j + 1) * HEAD_A] = s0_ref[s, hd]

    def shifted(ref, buf, mu_ref):
        rows = []
        for s in range(n_seq):
            x = ref[s]
            buf[s, 8:8 + C, :] = x
            rows.append(x + mu_ref[...] * (buf[s, 7:7 + C, :] - x))
            buf[s, 7:8, :] = x[C - 1:C, :]
        return jnp.concatenate(rows, axis=0)

    xs = shifted(x_ref, xbuf, mu_main_ref)
    lo_s = shifted(lora_ref, lbuf, mu_lora_ref)
    seq_rows = [slice(s * C, (s + 1) * C) for s in range(n_seq)]
    slabs = [slice(g * SLAB, (g + 1) * SLAB) for g in range(n_slabs)]

    r, k, v = xs[:, :W], xs[:, W:2 * W], xs[:, 2 * W:]
    w_pre = w0_ref[...] + _dot(jnp.tanh(lo_s).astype(BF16), wup_ref[...])
    ld = -math.exp(-0.5) * jax.nn.sigmoid(w_pre)
    a = jax.nn.sigmoid(a0_ref[...] + _dot(lo_s.astype(BF16), aup_ref[...]))

    e_mat = e_ref[...]

    segs = [slice(g * SEG_LANES, (g + 1) * SEG_LANES) for g in range(W // SEG_LANES)]

    def seg_sum(val):
        n = val.shape[0]
        hi, lo = _split2(val)
        out = _dot(jnp.concatenate([part[:, sl] for part in (hi, lo) for sl in segs], axis=0), e_mat)
        return jnp.concatenate([out[g * n:(g + 1) * n] + out[(len(segs) + g) * n:(len(segs) + g + 1) * n]
                                for g in range(len(segs))], axis=1)

    kk = k * kk_ref[...]
    kk = kk * lax.rsqrt(jnp.maximum(seg_sum(kk * kk), 1e-24))
    kmod = k * (1.0 + (a - 1.0) * ka_ref[...])

    ld_hi = ld.astype(BF16)
    ld_r1 = ld - ld_hi.astype(F32)
    ld_mid = ld_r1.astype(BF16)
    ld_lo = (ld_r1 - ld_mid.astype(F32)).astype(BF16)
    ltri = ltri_ref[...]
    lp = _dot(ltri, ld_hi) + _dot(ltri, ld_mid) + _dot(ltri, ld_lo)
    lp_last = [lp[rs.stop - 1:rs.stop, :] for rs in seq_rows]
    lp_end = jnp.concatenate([jnp.broadcast_to(row, (C, W)) for row in lp_last], axis=0)
    e_neg = jnp.exp(-lp)
    kka = kk * a
    kap = kk * jnp.exp(lp - ld)
    bet = kka * e_neg
    kt = kmod * e_neg
    rt = r * jnp.exp(lp)
    e_end = jnp.exp(lp_end - lp)
    kt_end = kmod * e_end
    bet_end = kka * e_end
    dec_end = [jnp.exp(row) for row in lp_last]

    lane = lax.broadcasted_iota(jnp.int32, (1, SLAB), 1)
    lane_masks = [(lane >= j * HEAD_A) & (lane < (j + 1) * HEAD_A) for j in range(GROUP_A)]
    t_row = lax.broadcasted_iota(jnp.int32, (C, GROUP_A * C), 0)
    s_col = lax.broadcasted_iota(jnp.int32, (C, GROUP_A * C), 1) & (C - 1)
    strict, incl = s_col < t_row, s_col <= t_row
    stack = lambda val16: _block_stack(val16, lane_masks)
    nh = GROUP_A * C
    units = [(s, g) for s in range(n_seq) for g in range(n_slabs)]
    per_slab = lambda fn: [fn(u, seq_rows[s], slabs[g]) for u, (s, g) in enumerate(units)]

    kap16, rt16, bet16, kt16, v16 = (val.astype(BF16) for val in (kap, rt, bet, kt, v))
    s_old = [sbd[s, g] for s, g in units]
    lhs = per_slab(lambda g, rs, sl: jnp.concatenate([kap16[rs, sl], rt16[rs, sl]], axis=0))
    rhs = per_slab(lambda g, rs, sl: jnp.concatenate([stack(bet16[rs, sl]), stack(kt16[rs, sl])], axis=0))
    sc = per_slab(lambda g, rs, sl: _dot_nt(lhs[g], rhs[g]))
    ls = per_slab(lambda g, rs, sl: _dot_nt(lhs[g], s_old[g].astype(BF16)))
    v_stack = per_slab(lambda g, rs, sl: stack(v16[rs, sl]))
    p16 = per_slab(lambda g, rs, sl: jnp.where(strict, -sc[g][:C, :nh], 0.0).astype(BF16))
    xw = per_slab(lambda g, rs, sl: ls[g][:C]
                  + _dot(jnp.where(strict, sc[g][:C, nh:], 0.0).astype(BF16), v_stack[g]))
    for step in range(6):
        xw = per_slab(lambda g, rs, sl: xw[g] + _dot(p16[g], stack(xw[g].astype(BF16))))
        if step < 5:
            p16 = per_slab(lambda g, rs, sl: _dot(p16[g], stack(p16[g])).astype(BF16))
    ab_inc = per_slab(lambda g, rs, sl: jnp.concatenate(
        [jnp.where(incl, sc[g][C:, :nh], 0.0), jnp.where(incl, sc[g][C:, nh:], 0.0)], axis=1).astype(BF16))
    y = per_slab(lambda g, rs, sl: ls[g][C:] + _dot(
        ab_inc[g], jnp.concatenate([stack((-xw[g]).astype(BF16)), v_stack[g]], axis=0)))
    upd = per_slab(lambda g, rs, sl: _dot(
        jnp.concatenate([v[rs, sl], -xw[g]], axis=0).T.astype(BF16),
        jnp.concatenate([kt_end[rs, sl], bet_end[rs, sl]], axis=0).astype(BF16)))
    for u, (s, g) in enumerate(units):
        sbd[s, g] = s_old[u] * dec_end[s][:, slabs[g]] + upd[u] * bd_ref[...]

    y = jnp.concatenate([jnp.concatenate(y[s * n_slabs:(s + 1) * n_slabs], axis=1) for s in range(n_seq)], axis=0)
    inv_n = 1.0 / HEAD_A
    mean = seg_sum(y) * inv_n
    yc = y - mean
    var = seg_sum(yc * yc) * inv_n
    y = yc * lax.rsqrt(var + GN_EPS) * gnw_ref[...] + gnb_ref[...]
    y = y + seg_sum(r * kmod * rk_ref[...]) * v
    for s, rs in enumerate(seq_rows):
        o_ref[s] = (y[rs] * jax.nn.silu(z_ref[s])).astype(o_ref.dtype)

    @pl.when(c == n_chunks - 1)
    def _fin():
        for s in range(n_seq):
            shm_out_ref[s] = xbuf[s, 7:8, :]
            shl_out_ref[s] = lbuf[s, 7:8, :]
            for hd in range(n_heads):
                g, j = divmod(hd, GROUP_A)
                sout_ref[s, hd] = sbd[s, g, j * HEAD_A:(j + 1) * HEAD_A, j * HEAD_A:(j + 1) * HEAD_A]


RWKV_SEQS_PER_STEP = 2


def rwkv7(proj, lora, z_col, shift_main, shift_lora, s0, prm, nseq, t, cast_f32=(), cast_drop=None):
    W = prm["a_w0"].shape[0]
    C = RWKV_CHUNK
    S = RWKV_SEQS_PER_STEP
    nc = t // C
    n_heads = W // HEAD_A
    lw = lora.shape[1]
    half = lw // 2
    zpad = jnp.zeros((half, W), F32)
    wup = jnp.concatenate([prm["a_w_up"], zpad], axis=0).astype(BF16)
    aup = jnp.concatenate([zpad, prm["a_a_up"]], axis=0).astype(BF16)
    same_head = lambda n: (jnp.arange(n)[:, None] // HEAD_A) == (jnp.arange(n)[None, :] // HEAD_A)
    ti = jnp.arange(S * C)
    ltri = ((ti[:, None] >= ti[None, :]) & (ti[:, None] // C == ti[None, :] // C)).astype(BF16)
    row1 = lambda a: a.reshape(1, -1)
    vec = lambda n: pl.BlockSpec((1, n), lambda b, c: (0, 0))
    full = lambda a: pl.BlockSpec(a.shape, lambda b, c: (0, 0))
    rows = lambda n, col: pl.BlockSpec((S, C, n), lambda b, c: (b, c, col))
    per_seq = lambda n: pl.BlockSpec((S, 1, n), lambda b, c: (b, 0, 0))
    state = pl.BlockSpec((S, n_heads, HEAD_A, HEAD_A), lambda b, c: (b, 0, 0, 0))
    e_mat, bd_mask = same_head(SEG_LANES).astype(BF16), same_head(SLAB).astype(F32)
    proj3 = proj.reshape(nseq, t, proj.shape[1])
    in_specs = [rows(3 * W, 0), rows(lw, 0), rows(W, z_col),
                per_seq(3 * W), per_seq(lw), state,
                vec(3 * W), vec(lw), vec(W), full(wup), vec(W), full(aup), vec(W), vec(W), vec(W), vec(W), vec(W),
                full(e_mat), full(bd_mask), full(ltri)]
    args = [proj3, lora.reshape(nseq, t, lw), proj3, shift_main, shift_lora, s0,
            row1(prm["a_mu"][:3 * W]), row1(prm["a_mu"][3 * W:]), row1(prm["a_w0"]), wup, row1(prm["a_a0"]), aup,
            row1(prm["a_k_k"]), row1(prm["a_k_a"]), row1(prm["a_r_k"]), row1(prm["a_gn_w"]), row1(prm["a_gn_b"]),
            e_mat, bd_mask, ltri]
    out_specs = [rows(W, 0), state, per_seq(3 * W), per_seq(lw)]
    out_shape = [jax.ShapeDtypeStruct((nseq, t, W), BF16),
                 jax.ShapeDtypeStruct((nseq, n_heads, HEAD_A, HEAD_A), F32),
                 jax.ShapeDtypeStruct((nseq, 1, 3 * W), F32),
                 jax.ShapeDtypeStruct((nseq, 1, lw), F32)]
    steps = (nseq // S) * nc
    for arr in cast_f32:
        flat = arr.reshape(-1, arr.shape[-1])
        slab = pl.BlockSpec((flat.shape[0] // steps, flat.shape[1]), lambda b, c: (b * nc + c, 0))
        in_specs, args = in_specs + [slab], args + [flat]
        out_specs, out_shape = out_specs + [slab], out_shape + [jax.ShapeDtypeStruct(flat.shape, BF16)]
    n_cast = len(cast_f32)
    if cast_drop is not None:
        w_all, w_layer, lo, hi = cast_drop
        _, k_rows, n_cols = w_all.shape
        in_specs = in_specs + [pl.BlockSpec((None, k_rows // steps, n_cols), lambda b, c: (w_layer, b * nc + c, 0))]
        out_specs = out_specs + [pl.BlockSpec((k_rows // steps, n_cols - (hi - lo)), lambda b, c: (b * nc + c, 0))]
        out_shape = out_shape + [jax.ShapeDtypeStruct((k_rows, n_cols - (hi - lo)), BF16)]
        args = args + [w_all]
        n_cast += 1
    o_a, s_new, sh_main, sh_lora, *cast16 = pl.pallas_call(
        functools.partial(_rwkv_kernel, chunk=C, width=W, n_cast=n_cast,
                          drop=None if cast_drop is None else (lo, hi)),
        grid=(nseq // S, nc),
        in_specs=in_specs,
        out_specs=out_specs,
        out_shape=out_shape,
        scratch_shapes=[pltpu.VMEM((S, C + 8, 3 * W), F32), pltpu.VMEM((S, C + 8, lw), F32),
                        pltpu.VMEM((S, W // SLAB, SLAB, SLAB), F32)],
        compiler_params=_params("parallel", "arbitrary"),
        name="rwkv7",
    )(*args)
    converted = [c16.reshape(arr.shape) for c16, arr in zip(cast16, cast_f32)]
    if cast_drop is not None:
        converted.append(cast16[-1][None])
    return o_a.reshape(nseq * t, W), s_new, sh_main, sh_lora, converted


COL_AZ, COL_BQ, COL_BK, COL_BV, COL_BZ, COL_CU, COL_MQ = 3, 4, 5, 6, 7, 8, 11


def trunk_layer(x, h, nseq, t, mk, mv, shift_prev, s_prev, past_k, past_v, prm, layer, depth, kv_out, want_vn):
    m, d = x.shape
    W = d // 4
    tm = min(512, m)
    if h is None:
        h = rmsnorm_rows(x, prm["g_pre"], min(256, m))
    proj = matmul(h, prm["w_in_main"], prm["w_in_main_layer"], min(1024, m), 1024)
    lora = matmul(h, prm["w_in_lora"], layer, min(1024, m), prm["w_in_lora"].shape[2])

    to_cast = [name for name in ("w_gate", "w_br", "w_out") if prm[name].dtype == F32]
    o_a, s_new, sh_main, sh_lora, cast16 = rwkv7(
        proj, lora, COL_AZ, shift_prev[..., :3 * W], shift_prev[..., 3 * W:], s_prev, prm, nseq, t,
        cast_f32=[prm[name] for name in to_cast], cast_drop=prm["w_in_next"])
    converted = dict(zip(to_cast + ["w_in_main_next"], cast16))
    weight = lambda name: converted.get(name, prm[name])
    shift_new = jnp.concatenate([sh_main, sh_lora], axis=-1)

    heads_b = W // HEAD_B
    hb = lambda col: col * heads_b
    o_b, kv_out = stick_breaking(proj, hb(COL_BQ), hb(COL_BZ), hb(COL_BK), hb(COL_BV), past_k, past_v,
                                 nseq, t, heads_b, layer, depth, kv_out)

    clen = min(t, prm["c_ws"].shape[1])
    groups = prm["c_ws"].shape[0]
    bs_full = jnp.repeat(prm["c_bs"][:, :clen].T, W // groups, axis=1)
    o_c, *vn_c = chunk_mlp(proj, COL_CU, prm["c_ws"][:, :clen, :clen], bs_full, prm["c_ln_w"], prm["c_ln_b"],
                           m, clen, max(1, min(512, t) // clen), want_vn)

    o_m = memory_attention(proj, COL_MQ, mk, mv, nseq, t, min(512, t))

    merged = gated_merge(h, (o_a, o_b, o_c, o_m), weight("w_gate"), prm["b_gate"], weight("w_br"), layer, tm,
                         512 if m > tm else 256)
    x_new, h_next = out_proj_norm_residual(merged, weight("w_out"), layer, x, prm["g_post"], prm["g_pre_next"], tm,
                                           512)
    return x_new, h_next, shift_new, s_new, kv_out, vn_c, converted


def kernel(x_prompt, x_sample, cache_mem_k, cache_mem_v, cache_sb_k, cache_sb_v, state_rwkv, state_shift, mem_prompt, g_pre, g_post, w_in, a_mu, a_w0, a_w_up, a_a0, a_a_up, a_k_k, a_k_a, a_r_k, a_gn_w, a_gn_b, c_ws, c_bs, c_ln_w, c_ln_b, g_mem, w_mem_kv, w_gate, b_gate, w_br, w_out):
    bp, tp, d = x_prompt.shape
    bs_, ts, _ = x_sample.shape
    depth = w_in.shape[0]
    W = d // 4
    n_mem = mem_prompt.shape[1]
    m_heads = cache_mem_k.shape[3]
    b_heads = cache_sb_k.shape[3]
    a_heads = state_rwkv.shape[2]
    shift_w = state_shift.shape[-1]
    n_past = cache_sb_k.shape[2]
    lora_lo, lora_hi = 3 * W, shift_w

    yp = x_prompt.reshape(bp * tp, d)
    ys = x_sample.reshape(bs_ * ts, d)
    mem_rows = mem_prompt.reshape(bp * n_mem, d)
    mem_out, rw_p, sh_p, rw_s, sh_s, cv_s = [[], []], [], [], [], [], []
    kv_p = kv_s = hp = hs = None
    dense = {
        "w_in_main": drop_cols_cast(w_in, 1, lora_lo, lora_hi, W, 1024), "w_in_main_layer": 0,
        "w_in_lora": w_in[:, :, lora_lo:lora_hi].astype(BF16),
        "w_gate": w_gate, "w_br": w_br, "w_out": w_out,
    }
    w_mem16 = w_mem_kv.astype(BF16)
    for l in range(depth):
        prm = dict(dense)
        prm.update({
            "w_in_next": (w_in, l + 1, lora_lo, lora_hi) if l + 1 < depth else None,
            "g_pre": g_pre[l], "g_post": g_post[l], "g_pre_next": g_pre[l + 1] if l + 1 < depth else None,
            "a_mu": a_mu[l], "a_w0": a_w0[l], "a_w_up": a_w_up[l], "a_a0": a_a0[l], "a_a_up": a_a_up[l],
            "a_k_k": a_k_k[l].reshape(-1), "a_k_a": a_k_a[l].reshape(-1), "a_r_k": a_r_k[l].reshape(-1),
            "a_gn_w": a_gn_w[l].reshape(-1), "a_gn_b": a_gn_b[l].reshape(-1),
            "c_ws": c_ws[l], "c_bs": c_bs[l], "c_ln_w": c_ln_w[l], "c_ln_b": c_ln_b[l], "b_gate": b_gate[l],
        })
        kv = matmul(rmsnorm_rows(mem_rows, g_mem[l], 256), w_mem16, l, min(1024, bp * n_mem), 1024)
        mk = kv[:, :W].reshape(bp, n_mem, W)
        mv = kv[:, W:].reshape(bp, n_mem, W)
        shift0 = jnp.zeros((bp, 1, shift_w), F32)
        s0 = jnp.zeros((bp, a_heads, HEAD_A, HEAD_A), F32)
        yp, hp, sh, st, kv_p, _, converted = trunk_layer(
            yp, hp, bp, tp, mk.astype(BF16), mv.astype(BF16), shift0, s0, None, None, prm, l, depth, kv_p, False)
        next_main = converted.pop("w_in_main_next", None)
        dense.update(converted)
        prm.update(converted)
        prm["w_in_next"] = None
        mem_out[0].append(mk.reshape(bp, n_mem, m_heads, W // m_heads))
        mem_out[1].append(mv.reshape(bp, n_mem, m_heads, W // m_heads))
        rw_p.append(st)
        sh_p.append(sh)
        ys, hs, sh, st, kv_s, cvn, _ = trunk_layer(
            ys, hs, bs_, ts, cache_mem_k[l].reshape(bs_, n_mem, W).astype(BF16),
            cache_mem_v[l].reshape(bs_, n_mem, W).astype(BF16), state_shift[l], state_rwkv[l],
            cache_sb_k, cache_sb_v, prm, l, depth, kv_s, True)
        rw_s.append(st)
        sh_s.append(sh)
        cv_s.append(cvn[0].reshape(bs_, ts, W))
        dense["w_in_main"] = next_main
    heads_p =lambda a: a.reshape(depth, bp, tp, b_heads, HEAD_B)
    heads_s = lambda a: a.reshape(depth, bs_, ts, b_heads, HEAD_B)
    return (yp.reshape(bp, tp, d), ys.reshape(bs_, ts, d), jnp.stack(mem_out[0]), jnp.stack(mem_out[1]),
            heads_p(kv_p[0]), heads_p(kv_p[1]), jnp.stack(rw_p), jnp.stack(sh_p),
            heads_s(kv_s[0]), heads_s(kv_s[1]), jnp.stack(rw_s), jnp.stack(sh_s), jnp.stack(cv_s))
```

```python
import functools
import math

import jax
import jax.numpy as jnp
from jax import lax
from jax.experimental import pallas as pl
from jax.experimental.pallas import tpu as pltpu

F32 = jnp.float32
BF16 = jnp.bfloat16

NORM_EPS = 1e-6
GN_EPS = 64e-5
HEAD_A = 64
GROUP_A = 2
SLAB = HEAD_A * GROUP_A
SEG_LANES = 256
HEAD_B = 128
HEAD_M = 256
GROUP_C = 128
RWKV_CHUNK = 64
VMEM_LIMIT_BYTES = 56 * 1024 * 1024


def _params(*sem):
    return pltpu.CompilerParams(dimension_semantics=sem, vmem_limit_bytes=VMEM_LIMIT_BYTES)


def _split2(x):
    hi = x.astype(BF16)
    lo = (x - hi.astype(F32)).astype(BF16)
    return hi, lo


def _dot(a, b):
    return jnp.dot(a, b, preferred_element_type=F32)


def _dot_nt(a, b):
    return lax.dot_general(a, b, (((1,), (1,)), ((), ())), preferred_element_type=F32)


def _dot_row_halves(dot, a, b):
    half = a.shape[0] // 2
    return jnp.concatenate([dot(a[:half], b), dot(a[half:], b)], axis=0)


def _rmsnorm_kernel(x_ref, g_ref, o_ref):
    x = x_ref[...]
    ms = jnp.mean(x * x, axis=-1, keepdims=True)
    o_ref[...] = (x * lax.rsqrt(ms + NORM_EPS) * g_ref[...]).astype(o_ref.dtype)


def rmsnorm_rows(x, g, tm):
    m, d = x.shape
    return pl.pallas_call(
        _rmsnorm_kernel,
        grid=(m // tm,),
        in_specs=[pl.BlockSpec((tm, d), lambda i: (i, 0)), pl.BlockSpec((1, d), lambda i: (0, 0))],
        out_specs=pl.BlockSpec((tm, d), lambda i: (i, 0)),
        out_shape=jax.ShapeDtypeStruct((m, d), BF16),
        compiler_params=_params("parallel"),
        name="rmsnorm_rows",
    )(x, g.reshape(1, d))


POST_ROWS = 64


def _out_proj_kernel(*refs, tn, with_next):
    if with_next:
        m_ref, w_ref, x_ref, g_ref, gn_ref, o_ref, hn_ref, x_rows, ssq = refs
    else:
        m_ref, w_ref, x_ref, g_ref, o_ref, x_rows, ssq = refs
    j = pl.program_id(1)
    cols = pl.ds(pl.multiple_of(j * tn, tn), tn)
    y = _dot(m_ref[...], w_ref[...])
    sq = jnp.sum(y * y, axis=-1, keepdims=True)
    o_ref[:, cols] = y * g_ref[...]
    x_rows[:, cols] = x_ref[...]

    @pl.when(j == 0)
    def _first():
        ssq[...] = sq

    @pl.when(j > 0)
    def _rest():
        ssq[...] += sq

    @pl.when(j == pl.num_programs(1) - 1)
    def _normalise():
        inv_d = 1.0 / o_ref.shape[1]
        for r0 in range(0, o_ref.shape[0], POST_ROWS):
            rows = slice(r0, r0 + POST_ROWS)
            scale = lax.rsqrt(ssq[rows, :] * inv_d + NORM_EPS)
            x_new = x_rows[rows, :] + o_ref[rows, :] * scale
            o_ref[rows, :] = x_new
            if with_next:
                ms = jnp.mean(x_new * x_new, axis=-1, keepdims=True)
                hn_ref[rows, :] = (x_new * lax.rsqrt(ms + NORM_EPS) * gn_ref[...]).astype(hn_ref.dtype)


def out_proj_norm_residual(merged, w, layer, x, g, g_next, tm, tn):
    m, d = x.shape
    with_next = g_next is not None
    row = pl.BlockSpec((tm, d), lambda i, j: (i, 0))
    gain = pl.BlockSpec((1, d), lambda i, j: (0, 0))
    out = pl.pallas_call(
        functools.partial(_out_proj_kernel, tn=tn, with_next=with_next),
        grid=(m // tm, d // tn),
        in_specs=[row, pl.BlockSpec((None, d, tn), lambda i, j: (layer, 0, j)),
                  pl.BlockSpec((tm, tn), lambda i, j: (i, j)), pl.BlockSpec((1, tn), lambda i, j: (0, j))]
                 + ([gain] if with_next else []),
        out_specs=[row, row] if with_next else [row],
        out_shape=[jax.ShapeDtypeStruct((m, d), F32)] + ([jax.ShapeDtypeStruct((m, d), BF16)] if with_next else []),
        scratch_shapes=[pltpu.VMEM((tm, d), F32), pltpu.VMEM((tm, 1), F32)],
        compiler_params=_params("parallel", "arbitrary"),
        name="out_proj_norm_residual",
    )(merged, w, x, g.reshape(1, d), *([g_next.reshape(1, d)] if with_next else []))
    return out if with_next else (out[0], None)


def _drop_cols_kernel(a_ref, b_ref, o_ref, *, first_shifted, gap):
    j = pl.program_id(2)

    @pl.when(j < first_shifted)
    def _before_gap():
        o_ref[...] = a_ref[...].astype(o_ref.dtype)

    @pl.when(j >= first_shifted)
    def _after_gap():
        o_ref[...] = jnp.concatenate([a_ref[:, gap:], b_ref[...]], axis=1).astype(o_ref.dtype)


def drop_cols_cast(w, n_layers, lo, hi, cw, tk):
    _, k, n = w.shape
    depth = n_layers
    gap = hi - lo
    n_out = n - gap
    return pl.pallas_call(
        functools.partial(_drop_cols_kernel, first_shifted=lo // cw, gap=gap),
        grid=(depth, k // tk, n_out // cw),
        in_specs=[pl.BlockSpec((None, tk, cw), lambda l, i, j: (l, i, j)),
                  pl.BlockSpec((None, tk, gap), lambda l, i, j: (l, i, (j + 1) * (cw // gap)))],
        out_specs=pl.BlockSpec((None, tk, cw), lambda l, i, j: (l, i, j)),
        out_shape=jax.ShapeDtypeStruct((depth, k, n_out), BF16),
        compiler_params=_params("parallel", "parallel", "parallel"),
        name="drop_cols_cast",
    )(w, w)
def _matmul_kernel(x_ref, w_ref, o_ref):
    o_ref[...] = _dot(x_ref[...], w_ref[...]).astype(o_ref.dtype)


def matmul(x, w, layer, tm, tn, out_dtype=F32):
    m, k = x.shape
    n = w.shape[2]
    return pl.pallas_call(
        _matmul_kernel,
        grid=(n // tn, m // tm),
        in_specs=[pl.BlockSpec((tm, k), lambda j, i: (i, 0)),
                  pl.BlockSpec((None, k, tn), lambda j, i: (layer, 0, j))],
        out_specs=pl.BlockSpec((tm, tn), lambda j, i: (i, j)),
        out_shape=jax.ShapeDtypeStruct((m, n), out_dtype),
        compiler_params=_params("parallel", "parallel"),
        name="matmul",
    )(x, w)


def _merge_kernel(h_ref, oa_ref, ob_ref, oc_ref, om_ref, wg_ref, bg_ref, wbr_ref, out_ref):
    h = h_ref[...]
    acc = None
    for n, o_ref in enumerate((oa_ref, ob_ref, oc_ref, om_ref)):
        gate = jax.nn.sigmoid(_dot(h, wg_ref[n]) + bg_ref[n])
        term = gate * _dot(o_ref[...], wbr_ref[n])
        acc = term if acc is None else acc + term
    out_ref[...] = acc.astype(out_ref.dtype)


def gated_merge(h, branches, wg, bg, wbr, layer, tm, tn):
    m, d = h.shape
    _, nb, w, _ = wbr.shape
    resident = dict(pipeline_mode=pl.Buffered(1)) if m > tm else {}
    o_spec = pl.BlockSpec((tm, w), lambda j, i: (i, 0))
    return pl.pallas_call(
        _merge_kernel,
        grid=(d // tn, m // tm),
        in_specs=[pl.BlockSpec((tm, d), lambda j, i: (i, 0)), o_spec, o_spec, o_spec, o_spec,
                  pl.BlockSpec((None, nb, d, tn), lambda j, i: (layer, 0, 0, j), **resident),
                  pl.BlockSpec((nb, 1, tn), lambda j, i: (0, 0, j)),
                  pl.BlockSpec((None, nb, w, tn), lambda j, i: (layer, 0, 0, j), **resident)],
        out_specs=pl.BlockSpec((tm, tn), lambda j, i: (i, j)),
        out_shape=jax.ShapeDtypeStruct((m, d), BF16),
        compiler_params=_params("parallel", "parallel"),
        name="gated_merge",
    )(h, *branches, wg, bg.reshape(nb, 1, d), wbr)


def _memattn_kernel(q_ref, mk_ref, mv_ref, o_ref, *, heads):
    scale = 1.0 / math.sqrt(HEAD_M)
    for hd in range(heads):
        sl = slice(hd * HEAD_M, (hd + 1) * HEAD_M)
        q = q_ref[:, sl].astype(BF16)
        s = _dot_nt(q, mk_ref[0, :, sl]) * scale
        s = s - jnp.max(s, axis=-1, keepdims=True)
        p = jnp.exp(s)
        p = p * (1.0 / jnp.sum(p, axis=-1, keepdims=True))
        o_ref[:, sl] = _dot(p.astype(BF16), mv_ref[0, :, sl]).astype(o_ref.dtype)


def memory_attention(proj, q_col, mk, mv, nseq, t, tq):
    n_mem, w = mk.shape[1:]
    per_seq = t // tq
    mem_spec = pl.BlockSpec((1, n_mem, w), lambda i: (i // per_seq, 0, 0))
    return pl.pallas_call(
        functools.partial(_memattn_kernel, heads=w // HEAD_M),
        grid=(nseq * per_seq,),
        in_specs=[pl.BlockSpec((tq, w), lambda i: (i, q_col)), mem_spec, mem_spec],
        out_specs=pl.BlockSpec((tq, w), lambda i: (i, 0)),
        out_shape=jax.ShapeDtypeStruct((nseq * t, w), BF16),
        compiler_params=_params("parallel"),
        name="memory_attention",
    )(proj, mk, mv)


def _cmlp_kernel(u_ref, v_ref, z_ref, ws_ref, bs_ref, lnw_ref, lnb_ref, o_ref, *maybe_vn_ref, clen, chunks):
    v = v_ref[...]
    mu = jnp.mean(v, axis=-1, keepdims=True)
    var = jnp.mean(jnp.square(v - mu), axis=-1, keepdims=True)
    vn = (v - mu) * lax.rsqrt(var + NORM_EPS) * lnw_ref[...] + lnb_ref[...]
    for vn_ref in maybe_vn_ref:
        vn_ref[...] = vn
    vn16 = vn.astype(BF16)
    groups = vn.shape[1] // GROUP_C
    row = lax.broadcasted_iota(jnp.int32, (clen, clen), 0)
    col = lax.broadcasted_iota(jnp.int32, (clen, clen), 1)
    for g in range(groups):
        wm = jnp.where(row >= col, ws_ref[g], 0.0).astype(BF16)
        cs = slice(g * GROUP_C, (g + 1) * GROUP_C)
        for c in range(chunks):
            rs = slice(c * clen, (c + 1) * clen)
            s = _dot(wm, vn16[rs, cs]) + bs_ref[:, cs]
            o_ref[rs, cs] = (u_ref[rs, cs] * s * jax.nn.silu(z_ref[rs, cs])).astype(o_ref.dtype)


def chunk_mlp(proj, u_col, ws, bs_full, ln_w, ln_b, rows, clen, chunks, want_vn):
    w = ln_w.shape[0]
    tm = clen * chunks
    col = lambda c: pl.BlockSpec((tm, w), lambda i: (i, c))
    const2 = lambda a: pl.BlockSpec(a.shape, lambda i: (0, 0))
    out = pl.BlockSpec((tm, w), lambda i: (i, 0))
    out_dtypes = [BF16, F32] if want_vn else [BF16]
    return pl.pallas_call(
        functools.partial(_cmlp_kernel, clen=clen, chunks=chunks),
        grid=(rows // tm,),
        in_specs=[col(u_col), col(u_col + 1), col(u_col + 2),
                  pl.BlockSpec(ws.shape, lambda i: (0, 0, 0)), const2(bs_full),
                  pl.BlockSpec((1, w), lambda i: (0, 0)), pl.BlockSpec((1, w), lambda i: (0, 0))],
        out_specs=[out] * len(out_dtypes),
        out_shape=[jax.ShapeDtypeStruct((rows, w), dt) for dt in out_dtypes],
        compiler_params=_params("parallel"),
        name="chunk_mlp",
    )(proj, proj, proj, ws, bs_full, ln_w.reshape(1, w), ln_b.reshape(1, w))


SB_HEADS_PER_STEP = 2
SB_BLOCK = 256


def _sb_group(q16, k16, v16, tk, u, masks, carry):
    tq = q16.shape[0]
    nb = k16.shape[0] // tk
    z_all = _dot_nt(q16, k16)
    nl, log_beta = [], []
    for j in range(nb):
        z2 = z_all[:, j * tk:(j + 1) * tk] * (math.log2(math.e) / math.sqrt(HEAD_B))
        neg_abs = lax.bitcast_convert_type(lax.bitcast_convert_type(z2, jnp.uint32) | jnp.uint32(0x80000000), F32)
        nl_j = jnp.maximum(z2, 0.0) + jnp.log2(1.0 + jnp.exp2(neg_abs))
        log_beta.append(z2 - nl_j)
        nl.append(jnp.where(masks[j], nl_j, 0.0) if j in masks else nl_j)
    lhs = jnp.concatenate([jnp.concatenate(_split2(nl_j), axis=1) for nl_j in nl], axis=0)
    cs = _dot_row_halves(_dot, lhs, u)
    att = [None] * nb
    for j in reversed(range(nb)):
        cs_j = cs[j * tq:(j + 1) * tq]
        att_j = jnp.exp2(log_beta[j] - cs_j - carry)
        att[j] = (jnp.where(masks[j], att_j, 0.0) if j in masks else att_j).astype(BF16)
        carry = carry + (cs_j[:, 0:1] + nl[j][:, 0:1])
    return _dot_row_halves(_dot, jnp.concatenate(att, axis=1), v16), carry


def _head_slices(ref):
    return [slice(i * HEAD_B, (i + 1) * HEAD_B) for i in range(ref.shape[1] // HEAD_B)]


def _emit_layer_rows(out_ref, rows, layer):
    if len(out_ref.shape) == 2:
        out_ref[...] = rows
    else:
        for l in range(out_ref.shape[0]):
            out_ref[l] = rows if l == layer else jnp.zeros_like(rows)


def _sb_prompt_kernel(*refs, tq, nq, layer):
    q_ref, z_ref, k_ref, v_ref, u_ref = refs[:5]
    o_ref, kout_ref, vout_ref = refs[-3:]
    _emit_layer_rows(kout_ref, k_ref[...], layer)
    _emit_layer_rows(vout_ref, v_ref[...], layer)
    row = lax.broadcasted_iota(jnp.int32, (tq, tq), 0)
    col = lax.broadcasted_iota(jnp.int32, (tq, tq), 1)
    diagonal = col < row
    k16 = k_ref[...].astype(BF16)
    v16 = v_ref[...].astype(BF16)
    for qi in range(nq):
        rows = slice(qi * tq, (qi + 1) * tq)
        n_keys = (qi + 1) * tq
        for s in _head_slices(q_ref):
            out, _ = _sb_group(q_ref[rows, s].astype(BF16), k16[0:n_keys, s], v16[0:n_keys, s], tq, u_ref[...],
                               {qi: diagonal}, jnp.zeros((tq, 1), F32))
            o_ref[rows, s] = (out * jax.nn.silu(z_ref[rows, s])).astype(o_ref.dtype)


def _sb_sample_kernel(*refs, tk, layer):
    q_ref, z_ref, kn_ref, vn_ref, kp_ref, vp_ref, ud_ref, uf_ref = refs[:8]
    o_ref, kout_ref, vout_ref = refs[-3:]
    _emit_layer_rows(kout_ref, kn_ref[...], layer)
    _emit_layer_rows(vout_ref, vn_ref[...], layer)
    tq = q_ref.shape[0]
    row = lax.broadcasted_iota(jnp.int32, (tq, tq), 0)
    col = lax.broadcasted_iota(jnp.int32, (tq, tq), 1)
    heads = _head_slices(q_ref)
    n_past = kp_ref.shape[0] // len(heads)
    for hd, s in enumerate(heads):
        q16 = q_ref[:, s].astype(BF16)
        out_new, carry = _sb_group(q16, kn_ref[:, s].astype(BF16), vn_ref[:, s].astype(BF16), tq, ud_ref[...],
                                   {0: col < row}, jnp.zeros((tq, 1), F32))
        own_rows = pl.ds(hd, n_past, stride=len(heads))
        out_past, _ = _sb_group(q16, kp_ref[own_rows, :].astype(BF16), vp_ref[own_rows, :].astype(BF16), tk,
                                uf_ref[...], {}, carry)
        o_ref[:, s] = ((out_new + out_past) * jax.nn.silu(z_ref[:, s])).astype(o_ref.dtype)


def _suffix_matrix(tk):
    s_later = lax.broadcasted_iota(jnp.int32, (tk, tk), 0)
    s_here = lax.broadcasted_iota(jnp.int32, (tk, tk), 1)
    u = (s_later > s_here).astype(BF16)
    return jnp.concatenate([u, u], axis=0)


def stick_breaking(proj, q_col, z_col, k_col, v_col, past_k, past_v, nseq, t, heads, layer, depth, kv_out):
    hp = SB_HEADS_PER_STEP if past_k is None else heads
    wide = hp * HEAD_B
    tq = min(SB_BLOCK, t)
    seq = lambda rows, col: pl.BlockSpec((rows, wide), lambda b, h: (b, col // hp + h))
    const = lambda a: pl.BlockSpec(a.shape, lambda b, h: (0, 0))
    if past_k is None:
        u = _suffix_matrix(tq)
        body = functools.partial(_sb_prompt_kernel, tq=tq, nq=t // tq, layer=layer)
        in_specs = [seq(t, q_col), seq(t, z_col), seq(t, k_col), seq(t, v_col), const(u)]
        args = [proj, proj, proj, proj, u]
    else:
        assert t == tq
        n_past = past_k.shape[2]
        past = pl.BlockSpec((None, None, n_past * heads, HEAD_B), lambda b, h: (layer, b, 0, 0))
        past_k, past_v = (a.reshape(depth, nseq, n_past * heads, HEAD_B) for a in (past_k, past_v))
        ud, uf = _suffix_matrix(tq), _suffix_matrix(SB_BLOCK)
        body = functools.partial(_sb_sample_kernel, tk=SB_BLOCK, layer=layer)
        in_specs = [seq(t, q_col), seq(t, z_col), seq(t, k_col), seq(t, v_col), past, past, const(ud), const(uf)]
        args = [proj, proj, proj, proj, past_k, past_v, ud, uf]
    kv_shape = jax.ShapeDtypeStruct((depth, nseq * t, heads * HEAD_B), F32)
    if kv_out is None:
        aliases = {}
        kv_spec = pl.BlockSpec((depth, t, wide), lambda b, h: (0, b, h))
    else:
        aliases = {len(args): 1, len(args) + 1: 2}
        in_specs = in_specs + [pl.BlockSpec(memory_space=pl.ANY)] * 2
        args = args + list(kv_out)
        kv_spec = pl.BlockSpec((None, t, wide), lambda b, h: (layer, b, h))
    o_b, k_all, v_all = pl.pallas_call(
        body,
        grid=(nseq, heads // hp),
        in_specs=in_specs,
        out_specs=[seq(t, 0), kv_spec, kv_spec],
        out_shape=[jax.ShapeDtypeStruct((nseq * t, heads * HEAD_B), BF16), kv_shape, kv_shape],
        input_output_aliases=aliases,
        compiler_params=_params("parallel", "parallel"),
        name="stick_breaking",
    )(*args)
    return o_b, (k_all, v_all)


def _block_stack(x, lane_masks):
    return jnp.concatenate([jnp.where(m, x, 0.0) for m in lane_masks], axis=0)


def _rwkv_kernel(*refs, chunk, width, n_cast, drop):
    (x_ref, lora_ref, z_ref, sh_main_ref, sh_lora_ref, s0_ref,
     mu_main_ref, mu_lora_ref, w0_ref, wup_ref, a0_ref, aup_ref, kk_ref, ka_ref, rk_ref,
     gnw_ref, gnb_ref, e_ref, bd_ref, ltri_ref) = refs[:20]
    xbuf, lbuf, sbd = refs[-3:]
    o_ref, sout_ref, shm_out_ref, shl_out_ref = refs[20 + n_cast:24 + n_cast]
    cast_pairs = list(zip(refs[20:20 + n_cast], refs[24 + n_cast:24 + 2 * n_cast]))
    for i, (cast_in_ref, cast_out_ref) in enumerate(cast_pairs):
        val = cast_in_ref[...]
        if drop is not None and i == n_cast - 1:
            val = jnp.concatenate([val[:, :drop[0]], val[:, drop[1]:]], axis=1)
        cast_out_ref[...] = val.astype(cast_out_ref.dtype)
    c = pl.program_id(1)
    n_chunks = pl.num_programs(1)
    C, W = chunk, width
    n_seq = x_ref.shape[0]
    n_slabs = W // SLAB
    n_heads = W // HEAD_A

    @pl.when(c == 0)
    def _init():
        sbd[...] = jnp.zeros_like(sbd)
        for s in range(n_seq):
            xbuf[s, 7:8, :] = sh_main_ref[s]
            lbuf[s, 7:8, :] = sh_lora_ref[s]
            for hd in range(n_heads):
                g, j = divmod(hd, GROUP_A)
                sbd[s, g, j * HEAD_A:(j + 1) * HEAD_A, j * HEAD_A:(j + 1) * HEAD_A] = s0_ref[s, hd]

    def shifted(ref, buf, mu_ref):
        rows = []
        for s in range(n_seq):
            x = ref[s]
            buf[s, 8:8 + C, :] = x
            rows.append(x + mu_ref[...] * (buf[s, 7:7 + C, :] - x))
            buf[s, 7:8, :] = x[C - 1:C, :]
        return jnp.concatenate(rows, axis=0)

    xs = shifted(x_ref, xbuf, mu_main_ref)
    lo_s = shifted(lora_ref, lbuf, mu_lora_ref)
    seq_rows = [slice(s * C, (s + 1) * C) for s in range(n_seq)]
    slabs = [slice(g * SLAB, (g + 1) * SLAB) for g in range(n_slabs)]

    r, k, v = xs[:, :W], xs[:, W:2 * W], xs[:, 2 * W:]
    w_pre = w0_ref[...] + _dot(jnp.tanh(lo_s).astype(BF16), wup_ref[...])
    ld = -math.exp(-0.5) * jax.nn.sigmoid(w_pre)
    a = jax.nn.sigmoid(a0_ref[...] + _dot(lo_s.astype(BF16), aup_ref[...]))

    e_mat = e_ref[...]

    segs = [slice(g * SEG_LANES, (g + 1) * SEG_LANES) for g in range(W // SEG_LANES)]

    def seg_sum(val):
        n = val.shape[0]
        hi, lo = _split2(val)
        out = _dot(jnp.concatenate([part[:, sl] for part in (hi, lo) for sl in segs], axis=0), e_mat)
        return jnp.concatenate([out[g * n:(g + 1) * n] + out[(len(segs) + g) * n:(len(segs) + g + 1) * n]
                                for g in range(len(segs))], axis=1)

    kk = k * kk_ref[...]
    kk = kk * lax.rsqrt(jnp.maximum(seg_sum(kk * kk), 1e-24))
    kmod = k * (1.0 + (a - 1.0) * ka_ref[...])

    ld_hi = ld.astype(BF16)
    ld_r1 = ld - ld_hi.astype(F32)
    ld_mid = ld_r1.astype(BF16)
    ld_lo = (ld_r1 - ld_mid.astype(F32)).astype(BF16)
    ltri = ltri_ref[...]
    lp = _dot(ltri, ld_hi) + _dot(ltri, ld_mid) + _dot(ltri, ld_lo)
    lp_last = [lp[rs.stop - 1:rs.stop, :] for rs in seq_rows]
    lp_end = jnp.concatenate([jnp.broadcast_to(row, (C, W)) for row in lp_last], axis=0)
    e_neg = jnp.exp(-lp)
    kka = kk * a
    kap = kk * jnp.exp(lp - ld)
    bet = kka * e_neg
    kt = kmod * e_neg
    rt = r * jnp.exp(lp)
    e_end = jnp.exp(lp_end - lp)
    kt_end = kmod * e_end
    bet_end = kka * e_end
    dec_end = [jnp.exp(row) for row in lp_last]

    lane = lax.broadcasted_iota(jnp.int32, (1, SLAB), 1)
    lane_masks = [(lane >= j * HEAD_A) & (lane < (j + 1) * HEAD_A) for j in range(GROUP_A)]
    t_row = lax.broadcasted_iota(jnp.int32, (C, GROUP_A * C), 0)
    s_col = lax.broadcasted_iota(jnp.int32, (C, GROUP_A * C), 1) & (C - 1)
    strict, incl = s_col < t_row, s_col <= t_row
    stack = lambda val16: _block_stack(val16, lane_masks)
    nh = GROUP_A * C
    units = [(s, g) for s in range(n_seq) for g in range(n_slabs)]
    per_slab = lambda fn: [fn(u, seq_rows[s], slabs[g]) for u, (s, g) in enumerate(units)]

    kap16, rt16, bet16, kt16, v16 = (val.astype(BF16) for val in (kap, rt, bet, kt, v))
    s_old = [sbd[s, g] for s, g in units]
    lhs = per_slab(lambda g, rs, sl: jnp.concatenate([kap16[rs, sl], rt16[rs, sl]], axis=0))
    rhs = per_slab(lambda g, rs, sl: jnp.concatenate([stack(bet16[rs, sl]), stack(kt16[rs, sl])], axis=0))
    sc = per_slab(lambda g, rs, sl: _dot_nt(lhs[g], rhs[g]))
    ls = per_slab(lambda g, rs, sl: _dot_nt(lhs[g], s_old[g].astype(BF16)))
    v_stack = per_slab(lambda g, rs, sl: stack(v16[rs, sl]))
    p16 = per_slab(lambda g, rs, sl: jnp.where(strict, -sc[g][:C, :nh], 0.0).astype(BF16))
    xw = per_slab(lambda g, rs, sl: ls[g][:C]
                  + _dot(jnp.where(strict, sc[g][:C, nh:], 0.0).astype(BF16), v_stack[g]))
    for step in range(6):
        xw = per_slab(lambda g, rs, sl: xw[g] + _dot(p16[g], stack(xw[g].astype(BF16))))
        if step < 5:
            p16 = per_slab(lambda g, rs, sl: _dot(p16[g], stack(p16[g])).astype(BF16))
    ab_inc = per_slab(lambda g, rs, sl: jnp.concatenate(
        [jnp.where(incl, sc[g][C:, :nh], 0.0), jnp.where(incl, sc[g][C:, nh:], 0.0)], axis=1).astype(BF16))
    y = per_slab(lambda g, rs, sl: ls[g][C:] + _dot(
        ab_inc[g], jnp.concatenate([stack((-xw[g]).astype(BF16)), v_stack[g]], axis=0)))
    upd = per_slab(lambda g, rs, sl: _dot(
        jnp.concatenate([v[rs, sl], -xw[g]], axis=0).T.astype(BF16),
        jnp.concatenate([kt_end[rs, sl], bet_end[rs, sl]], axis=0).astype(BF16)))
    for u, (s, g) in enumerate(units):
        sbd[s, g] = s_old[u] * dec_end[s][:, slabs[g]] + upd[u] * bd_ref[...]

    y = jnp.concatenate([jnp.concatenate(y[s * n_slabs:(s + 1) * n_slabs], axis=1) for s in range(n_seq)], axis=0)
    inv_n = 1.0 / HEAD_A
    mean = seg_sum(y) * inv_n
    yc = y - mean
    var = seg_sum(yc * yc) * inv_n
    y = yc * lax.rsqrt(var + GN_EPS) * gnw_ref[...] + gnb_ref[...]
    y = y + seg_sum(r * kmod * rk_ref[...]) * v
    for s, rs in enumerate(seq_rows):
        o_ref[s] = (y[rs] * jax.nn.silu(z_ref[s])).astype(o_ref.dtype)

    @pl.when(c == n_chunks - 1)
    def _fin():
        for s in range(n_seq):
            shm_out_ref[s] = xbuf[s, 7:8, :]
            shl_out_ref[s] = lbuf[s, 7:8, :]
            for hd in range(n_heads):
                g, j = divmod(hd, GROUP_A)
                sout_ref[s, hd] = sbd[s, g, j * HEAD_A:(j + 1) * HEAD_A, j * HEAD_A:(j + 1) * HEAD_A]


RWKV_SEQS_PER_STEP = 2


def rwkv7(proj, lora, z_col, shift_main, shift_lora, s0, prm, nseq, t, cast_f32=(), cast_drop=None):
    W = prm["a_w0"].shape[0]
    C = RWKV_CHUNK
    S = RWKV_SEQS_PER_STEP
    nc = t // C
    n_heads = W // HEAD_A
    lw = lora.shape[1]
    half = lw // 2
    zpad = jnp.zeros((half, W), F32)
    wup = jnp.concatenate([prm["a_w_up"], zpad], axis=0).astype(BF16)
    aup = jnp.concatenate([zpad, prm["a_a_up"]], axis=0).astype(BF16)
    same_head = lambda n: (jnp.arange(n)[:, None] // HEAD_A) == (jnp.arange(n)[None, :] // HEAD_A)
    ti = jnp.arange(S * C)
    ltri = ((ti[:, None] >= ti[None, :]) & (ti[:, None] // C == ti[None, :] // C)).astype(BF16)
    row1 = lambda a: a.reshape(1, -1)
    vec = lambda n: pl.BlockSpec((1, n), lambda b, c: (0, 0))
    full = lambda a: pl.BlockSpec(a.shape, lambda b, c: (0, 0))
    rows = lambda n, col: pl.BlockSpec((S, C, n), lambda b, c: (b, c, col))
    per_seq = lambda n: pl.BlockSpec((S, 1, n), lambda b, c: (b, 0, 0))
    state = pl.BlockSpec((S, n_heads, HEAD_A, HEAD_A), lambda b, c: (b, 0, 0, 0))
    e_mat, bd_mask = same_head(SEG_LANES).astype(BF16), same_head(SLAB).astype(F32)
    proj3 = proj.reshape(nseq, t, proj.shape[1])
    in_specs = [rows(3 * W, 0), rows(lw, 0), rows(W, z_col),
                per_seq(3 * W), per_seq(lw), state,
                vec(3 * W), vec(lw), vec(W), full(wup), vec(W), full(aup), vec(W), vec(W), vec(W), vec(W), vec(W),
                full(e_mat), full(bd_mask), full(ltri)]
    args = [proj3, lora.reshape(nseq, t, lw), proj3, shift_main, shift_lora, s0,
            row1(prm["a_mu"][:3 * W]), row1(prm["a_mu"][3 * W:]), row1(prm["a_w0"]), wup, row1(prm["a_a0"]), aup,
            row1(prm["a_k_k"]), row1(prm["a_k_a"]), row1(prm["a_r_k"]), row1(prm["a_gn_w"]), row1(prm["a_gn_b"]),
            e_mat, bd_mask, ltri]
    out_specs = [rows(W, 0), state, per_seq(3 * W), per_seq(lw)]
    out_shape = [jax.ShapeDtypeStruct((nseq, t, W), BF16),
                 jax.ShapeDtypeStruct((nseq, n_heads, HEAD_A, HEAD_A), F32),
                 jax.ShapeDtypeStruct((nseq, 1, 3 * W), F32),
                 jax.ShapeDtypeStruct((nseq, 1, lw), F32)]
    steps = (nseq // S) * nc
    for arr in cast_f32:
        flat = arr.reshape(-1, arr.shape[-1])
        slab = pl.BlockSpec((flat.shape[0] // steps, flat.shape[1]), lambda b, c: (b * nc + c, 0))
        in_specs, args = in_specs + [slab], args + [flat]
        out_specs, out_shape = out_specs + [slab], out_shape + [jax.ShapeDtypeStruct(flat.shape, BF16)]
    n_cast = len(cast_f32)
    if cast_drop is not None:
        w_all, w_layer, lo, hi = cast_drop
        _, k_rows, n_cols = w_all.shape
        in_specs = in_specs + [pl.BlockSpec((None, k_rows // steps, n_cols), lambda b, c: (w_layer, b * nc + c, 0))]
        out_specs = out_specs + [pl.BlockSpec((k_rows // steps, n_cols - (hi - lo)), lambda b, c: (b * nc + c, 0))]
        out_shape = out_shape + [jax.ShapeDtypeStruct((k_rows, n_cols - (hi - lo)), BF16)]
        args = args + [w_all]
        n_cast += 1
    o_a, s_new, sh_main, sh_lora, *cast16 = pl.pallas_call(
        functools.partial(_rwkv_kernel, chunk=C, width=W, n_cast=n_cast,
                          drop=None if cast_drop is None else (lo, hi)),
        grid=(nseq // S, nc),
        in_specs=in_specs,
        out_specs=out_specs,
        out_shape=out_shape,
        scratch_shapes=[pltpu.VMEM((S, C + 8, 3 * W), F32), pltpu.VMEM((S, C + 8, lw), F32),
                        pltpu.VMEM((S, W // SLAB, SLAB, SLAB), F32)],
        compiler_params=_params("parallel", "arbitrary"),
        name="rwkv7",
    )(*args)
    converted = [c16.reshape(arr.shape) for c16, arr in zip(cast16, cast_f32)]
    if cast_drop is not None:
        converted.append(cast16[-1][None])
    return o_a.reshape(nseq * t, W), s_new, sh_main, sh_lora, converted


COL_AZ, COL_BQ, COL_BK, COL_BV, COL_BZ, COL_CU, COL_MQ = 3, 4, 5, 6, 7, 8, 11


def trunk_layer(x, h, nseq, t, mk, mv, shift_prev, s_prev, past_k, past_v, prm, layer, depth, kv_out, want_vn):
    m, d = x.shape
    W = d // 4
    tm = min(512, m)
    if h is None:
        h = rmsnorm_rows(x, prm["g_pre"], min(256, m))
    proj = matmul(h, prm["w_in_main"], prm["w_in_main_layer"], min(1024, m), 1024)
    lora = matmul(h, prm["w_in_lora"], layer, min(1024, m), prm["w_in_lora"].shape[2])

    to_cast = [name for name in ("w_gate", "w_br", "w_out") if prm[name].dtype == F32]
    o_a, s_new, sh_main, sh_lora, cast16 = rwkv7(
        proj, lora, COL_AZ, shift_prev[..., :3 * W], shift_prev[..., 3 * W:], s_prev, prm, nseq, t,
        cast_f32=[prm[name] for name in to_cast], cast_drop=prm["w_in_next"])
    converted = dict(zip(to_cast + ["w_in_main_next"], cast16))
    weight = lambda name: converted.get(name, prm[name])
    shift_new = jnp.concatenate([sh_main, sh_lora], axis=-1)

    heads_b = W // HEAD_B
    hb = lambda col: col * heads_b
    o_b, kv_out = stick_breaking(proj, hb(COL_BQ), hb(COL_BZ), hb(COL_BK), hb(COL_BV), past_k, past_v,
                                 nseq, t, heads_b, layer, depth, kv_out)

    clen = min(t, prm["c_ws"].shape[1])
    groups = prm["c_ws"].shape[0]
    bs_full = jnp.repeat(prm["c_bs"][:, :clen].T, W // groups, axis=1)
    o_c, *vn_c = chunk_mlp(proj, COL_CU, prm["c_ws"][:, :clen, :clen], bs_full, prm["c_ln_w"], prm["c_ln_b"],
                           m, clen, max(1, min(512, t) // clen), want_vn)

    o_m = memory_attention(proj, COL_MQ, mk, mv, nseq, t, min(512, t))

    merged = gated_merge(h, (o_a, o_b, o_c, o_m), weight("w_gate"), prm["b_gate"], weight("w_br"), layer, tm,
                         512 if m > tm else 256)
    x_new, h_next = out_proj_norm_residual(merged, weight("w_out"), layer, x, prm["g_post"], prm["g_pre_next"], tm,
                                           512)
    return x_new, h_next, shift_new, s_new, kv_out, vn_c, converted


def kernel(x_prompt, x_sample, cache_mem_k, cache_mem_v, cache_sb_k, cache_sb_v, state_rwkv, state_shift, mem_prompt, g_pre, g_post, w_in, a_mu, a_w0, a_w_up, a_a0, a_a_up, a_k_k, a_k_a, a_r_k, a_gn_w, a_gn_b, c_ws, c_bs, c_ln_w, c_ln_b, g_mem, w_mem_kv, w_gate, b_gate, w_br, w_out):
    bp, tp, d = x_prompt.shape
    bs_, ts, _ = x_sample.shape
    depth = w_in.shape[0]
    W = d // 4
    n_mem = mem_prompt.shape[1]
    m_heads = cache_mem_k.shape[3]
    b_heads = cache_sb_k.shape[3]
    a_heads = state_rwkv.shape[2]
    shift_w = state_shift.shape[-1]
    n_past = cache_sb_k.shape[2]
    lora_lo, lora_hi = 3 * W, shift_w

    yp = x_prompt.reshape(bp * tp, d)
    ys = x_sample.reshape(bs_ * ts, d)
    mem_rows = mem_prompt.reshape(bp * n_mem, d)
    mem_out, rw_p, sh_p, rw_s, sh_s, cv_s = [[], []], [], [], [], [], []
    kv_p = kv_s = hp = hs = None
    dense = {
        "w_in_main": drop_cols_cast(w_in, 1, lora_lo, lora_hi, W, 1024), "w_in_main_layer": 0,
        "w_in_lora": w_in[:, :, lora_lo:lora_hi].astype(BF16),
        "w_gate": w_gate, "w_br": w_br, "w_out": w_out,
    }
    w_mem16 = w_mem_kv.astype(BF16)
    for l in range(depth):
        prm = dict(dense)
        prm.update({
            "w_in_next": (w_in, l + 1, lora_lo, lora_hi) if l + 1 < depth else None,
            "g_pre": g_pre[l], "g_post": g_post[l], "g_pre_next": g_pre[l + 1] if l + 1 < depth else None,
            "a_mu": a_mu[l], "a_w0": a_w0[l], "a_w_up": a_w_up[l], "a_a0": a_a0[l], "a_a_up": a_a_up[l],
            "a_k_k": a_k_k[l].reshape(-1), "a_k_a": a_k_a[l].reshape(-1), "a_r_k": a_r_k[l].reshape(-1),
            "a_gn_w": a_gn_w[l].reshape(-1), "a_gn_b": a_gn_b[l].reshape(-1),
            "c_ws": c_ws[l], "c_bs": c_bs[l], "c_ln_w": c_ln_w[l], "c_ln_b": c_ln_b[l], "b_gate": b_gate[l],
        })
        kv = matmul(rmsnorm_rows(mem_rows, g_mem[l], 256), w_mem16, l, min(1024, bp * n_mem), 1024)
        mk = kv[:, :W].reshape(bp, n_mem, W)
        mv = kv[:, W:].reshape(bp, n_mem, W)
        shift0 = jnp.zeros((bp, 1, shift_w), F32)
        s0 = jnp.zeros((bp, a_heads, HEAD_A, HEAD_A), F32)
        yp, hp, sh, st, kv_p, _, converted = trunk_layer(
            yp, hp, bp, tp, mk.astype(BF16), mv.astype(BF16), shift0, s0, None, None, prm, l, depth, kv_p, False)
        next_main = converted.pop("w_in_main_next", None)
        dense.update(converted)
        prm.update(converted)
        prm["w_in_next"] = None
        mem_out[0].append(mk.reshape(bp, n_mem, m_heads, W // m_heads))
        mem_out[1].append(mv.reshape(bp, n_mem, m_heads, W // m_heads))
        rw_p.append(st)
        sh_p.append(sh)
        ys, hs, sh, st, kv_s, cvn, _ = trunk_layer(
            ys, hs, bs_, ts, cache_mem_k[l].reshape(bs_, n_mem, W).astype(BF16),
            cache_mem_v[l].reshape(bs_, n_mem, W).astype(BF16), state_shift[l], state_rwkv[l],
            cache_sb_k, cache_sb_v, prm, l, depth, kv_s, True)
        rw_s.append(st)
        sh_s.append(sh)
        cv_s.append(cvn[0].reshape(bs_, ts, W))
        dense["w_in_main"] = next_main
    heads_p =lambda a: a.reshape(depth, bp, tp, b_heads, HEAD_B)
    heads_s = lambda a: a.reshape(depth, bs_, ts, b_heads, HEAD_B)
    return (yp.reshape(bp, tp, d), ys.reshape(bs_, ts, d), jnp.stack(mem_out[0]), jnp.stack(mem_out[1]),
            heads_p(kv_p[0]), heads_p(kv_p[1]), jnp.stack(rw_p), jnp.stack(sh_p),
            heads_s(kv_s[0]), heads_s(kv_s[1]), jnp.stack(rw_s), jnp.stack(sh_s), jnp.stack(cv_s))
```

```python
import functools
import math

import jax
import jax.numpy as jnp
from jax import lax
from jax.experimental import pallas as pl
from jax.experimental.pallas import tpu as pltpu

F32 = jnp.float32
BF16 = jnp.bfloat16

NORM_EPS = 1e-6
GN_EPS = 64e-5
HEAD_A = 64
GROUP_A = 2
SLAB = HEAD_A * GROUP_A
SEG_LANES = 256
HEAD_B = 128
HEAD_M = 256
GROUP_C = 128
RWKV_CHUNK = 64
VMEM_LIMIT_BYTES = 56 * 1024 * 1024


def _params(*sem):
    return pltpu.CompilerParams(dimension_semantics=sem, vmem_limit_bytes=VMEM_LIMIT_BYTES)


def _split2(x):
    hi = x.astype(BF16)
    lo = (x - hi.astype(F32)).astype(BF16)
    return hi, lo


def _dot(a, b):
    return jnp.dot(a, b, preferred_element_type=F32)


def _dot_nt(a, b):
    return lax.dot_general(a, b, (((1,), (1,)), ((), ())), preferred_element_type=F32)


def _dot_row_halves(dot, a, b):
    half = a.shape[0] // 2
    return jnp.concatenate([dot(a[:half], b), dot(a[half:], b)], axis=0)


def _rmsnorm_kernel(x_ref, g_ref, o_ref):
    x = x_ref[...]
    ms = jnp.mean(x * x, axis=-1, keepdims=True)
    o_ref[...] = (x * lax.rsqrt(ms + NORM_EPS) * g_ref[...]).astype(o_ref.dtype)


def rmsnorm_rows(x, g, tm):
    m, d = x.shape
    return pl.pallas_call(
        _rmsnorm_kernel,
        grid=(m // tm,),
        in_specs=[pl.BlockSpec((tm, d), lambda i: (i, 0)), pl.BlockSpec((1, d), lambda i: (0, 0))],
        out_specs=pl.BlockSpec((tm, d), lambda i: (i, 0)),
        out_shape=jax.ShapeDtypeStruct((m, d), BF16),
        compiler_params=_params("parallel"),
        name="rmsnorm_rows",
    )(x, g.reshape(1, d))


POST_ROWS = 64


def _out_proj_kernel(*refs, tn, with_next):
    if with_next:
        m_ref, w_ref, x_ref, g_ref, gn_ref, o_ref, hn_ref, x_rows, ssq = refs
    else:
        m_ref, w_ref, x_ref, g_ref, o_ref, x_rows, ssq = refs
    j = pl.program_id(1)
    cols = pl.ds(pl.multiple_of(j * tn, tn), tn)
    y = _dot(m_ref[...], w_ref[...])
    sq = jnp.sum(y * y, axis=-1, keepdims=True)
    o_ref[:, cols] = y * g_ref[...]
    x_rows[:, cols] = x_ref[...]

    @pl.when(j == 0)
    def _first():
        ssq[...] = sq

    @pl.when(j > 0)
    def _rest():
        ssq[...] += sq

    @pl.when(j == pl.num_programs(1) - 1)
    def _normalise():
        inv_d = 1.0 / o_ref.shape[1]
        for r0 in range(0, o_ref.shape[0], POST_ROWS):
            rows = slice(r0, r0 + POST_ROWS)
            scale = lax.rsqrt(ssq[rows, :] * inv_d + NORM_EPS)
            x_new = x_rows[rows, :] + o_ref[rows, :] * scale
            o_ref[rows, :] = x_new
            if with_next:
                ms = jnp.mean(x_new * x_new, axis=-1, keepdims=True)
                hn_ref[rows, :] = (x_new * lax.rsqrt(ms + NORM_EPS) * gn_ref[...]).astype(hn_ref.dtype)


def out_proj_norm_residual(merged, w, layer, x, g, g_next, tm, tn):
    m, d = x.shape
    with_next = g_next is not None
    row = pl.BlockSpec((tm, d), lambda i, j: (i, 0))
    gain = pl.BlockSpec((1, d), lambda i, j: (0, 0))
    out = pl.pallas_call(
        functools.partial(_out_proj_kernel, tn=tn, with_next=with_next),
        grid=(m // tm, d // tn),
        in_specs=[row, pl.BlockSpec((None, d, tn), lambda i, j: (layer, 0, j)),
                  pl.BlockSpec((tm, tn), lambda i, j: (i, j)), pl.BlockSpec((1, tn), lambda i, j: (0, j))]
                 + ([gain] if with_next else []),
        out_specs=[row, row] if with_next else [row],
        out_shape=[jax.ShapeDtypeStruct((m, d), F32)] + ([jax.ShapeDtypeStruct((m, d), BF16)] if with_next else []),
        scratch_shapes=[pltpu.VMEM((tm, d), F32), pltpu.VMEM((tm, 1), F32)],
        compiler_params=_params("parallel", "arbitrary"),
        name="out_proj_norm_residual",
    )(merged, w, x, g.reshape(1, d), *([g_next.reshape(1, d)] if with_next else []))
    return out if with_next else (out[0], None)


def _drop_cols_kernel(a_ref, b_ref, o_ref, *, first_shifted, gap):
    j = pl.program_id(2)

    @pl.when(j < first_shifted)
    def _before_gap():
        o_ref[...] = a_ref[...].astype(o_ref.dtype)

    @pl.when(j >= first_shifted)
    def _after_gap():
        o_ref[...] = jnp.concatenate([a_ref[:, gap:], b_ref[...]], axis=1).astype(o_ref.dtype)


def drop_cols_cast(w, n_layers, lo, hi, cw, tk):
    _, k, n = w.shape
    depth = n_layers
    gap = hi - lo
    n_out = n - gap
    return pl.pallas_call(
        functools.partial(_drop_cols_kernel, first_shifted=lo // cw, gap=gap),
        grid=(depth, k // tk, n_out // cw),
        in_specs=[pl.BlockSpec((None, tk, cw), lambda l, i, j: (l, i, j)),
                  pl.BlockSpec((None, tk, gap), lambda l, i, j: (l, i, (j + 1) * (cw // gap)))],
        out_specs=pl.BlockSpec((None, tk, cw), lambda l, i, j: (l, i, j)),
        out_shape=jax.ShapeDtypeStruct((depth, k, n_out), BF16),
        compiler_params=_params("parallel", "parallel", "parallel"),
        name="drop_cols_cast",
    )(w, w)
def _matmul_kernel(x_ref, w_ref, o_ref):
    o_ref[...] = _dot(x_ref[...], w_ref[...]).astype(o_ref.dtype)


def matmul(x, w, layer, tm, tn, out_dtype=F32):
    m, k = x.shape
    n = w.shape[2]
    return pl.pallas_call(
        _matmul_kernel,
        grid=(n // tn, m // tm),
        in_specs=[pl.BlockSpec((tm, k), lambda j, i: (i, 0)),
                  pl.BlockSpec((None, k, tn), lambda j, i: (layer, 0, j))],
        out_specs=pl.BlockSpec((tm, tn), lambda j, i: (i, j)),
        out_shape=jax.ShapeDtypeStruct((m, n), out_dtype),
        compiler_params=_params("parallel", "parallel"),
        name="matmul",
    )(x, w)


def _merge_kernel(h_ref, oa_ref, ob_ref, oc_ref, om_ref, wg_ref, bg_ref, wbr_ref, out_ref):
    h = h_ref[...]
    acc = None
    for n, o_ref in enumerate((oa_ref, ob_ref, oc_ref, om_ref)):
        gate = jax.nn.sigmoid(_dot(h, wg_ref[n]) + bg_ref[n])
        term = gate * _dot(o_ref[...], wbr_ref[n])
        acc = term if acc is None else acc + term
    out_ref[...] = acc.astype(out_ref.dtype)


def gated_merge(h, branches, wg, bg, wbr, layer, tm, tn):
    m, d = h.shape
    _, nb, w, _ = wbr.shape
    resident = dict(pipeline_mode=pl.Buffered(1)) if m > tm else {}
    o_spec = pl.BlockSpec((tm, w), lambda j, i: (i, 0))
    return pl.pallas_call(
        _merge_kernel,
        grid=(d // tn, m // tm),
        in_specs=[pl.BlockSpec((tm, d), lambda j, i: (i, 0)), o_spec, o_spec, o_spec, o_spec,
                  pl.BlockSpec((None, nb, d, tn), lambda j, i: (layer, 0, 0, j), **resident),
                  pl.BlockSpec((nb, 1, tn), lambda j, i: (0, 0, j)),
                  pl.BlockSpec((None, nb, w, tn), lambda j, i: (layer, 0, 0, j), **resident)],
        out_specs=pl.BlockSpec((tm, tn), lambda j, i: (i, j)),
        out_shape=jax.ShapeDtypeStruct((m, d), BF16),
        compiler_params=_params("parallel", "parallel"),
        name="gated_merge",
    )(h, *branches, wg, bg.reshape(nb, 1, d), wbr)


def _memattn_kernel(q_ref, mk_ref, mv_ref, o_ref, *, heads):
    scale = 1.0 / math.sqrt(HEAD_M)
    lanes = [slice(hd * HEAD_M, (hd + 1) * HEAD_M) for hd in range(heads)]
    scores = [_dot_row_halves(_dot_nt, q_ref[:, sl].astype(BF16), mk_ref[0, :, sl]) * scale for sl in lanes]
    probs = []
    for s in scores:
        p = jnp.exp(s - jnp.max(s, axis=-1, keepdims=True))
        probs.append((p * (1.0 / jnp.sum(p, axis=-1, keepdims=True))).astype(BF16))
    for sl, p in zip(lanes, probs):
        o_ref[:, sl] = _dot_row_halves(_dot, p, mv_ref[0, :, sl]).astype(o_ref.dtype)


def memory_attention(proj, q_col, mk, mv, nseq, t, tq):
    n_mem, w = mk.shape[1:]
    per_seq = t // tq
    mem_spec = pl.BlockSpec((1, n_mem, w), lambda i: (i // per_seq, 0, 0))
    return pl.pallas_call(
        functools.partial(_memattn_kernel, heads=w // HEAD_M),
        grid=(nseq * per_seq,),
        in_specs=[pl.BlockSpec((tq, w), lambda i: (i, q_col)), mem_spec, mem_spec],
        out_specs=pl.BlockSpec((tq, w), lambda i: (i, 0)),
        out_shape=jax.ShapeDtypeStruct((nseq * t, w), BF16),
        compiler_params=_params("parallel"),
        name="memory_attention",
    )(proj, mk, mv)


def _cmlp_kernel(u_ref, v_ref, z_ref, ws_ref, bs_ref, lnw_ref, lnb_ref, o_ref, *maybe_vn_ref, clen, chunks):
    v = v_ref[...]
    mu = jnp.mean(v, axis=-1, keepdims=True)
    var = jnp.mean(jnp.square(v - mu), axis=-1, keepdims=True)
    vn = (v - mu) * lax.rsqrt(var + NORM_EPS) * lnw_ref[...] + lnb_ref[...]
    for vn_ref in maybe_vn_ref:
        vn_ref[...] = vn
    vn16 = vn.astype(BF16)
    groups = vn.shape[1] // GROUP_C
    row = lax.broadcasted_iota(jnp.int32, (clen, clen), 0)
    col = lax.broadcasted_iota(jnp.int32, (clen, clen), 1)
    for g in range(groups):
        wm = jnp.where(row >= col, ws_ref[g], 0.0).astype(BF16)
        cs = slice(g * GROUP_C, (g + 1) * GROUP_C)
        for c in range(chunks):
            rs = slice(c * clen, (c + 1) * clen)
            s = _dot(wm, vn16[rs, cs]) + bs_ref[:, cs]
            o_ref[rs, cs] = (u_ref[rs, cs] * s * jax.nn.silu(z_ref[rs, cs])).astype(o_ref.dtype)


def chunk_mlp(proj, u_col, ws, bs_full, ln_w, ln_b, rows, clen, chunks, want_vn):
    w = ln_w.shape[0]
    tm = clen * chunks
    col = lambda c: pl.BlockSpec((tm, w), lambda i: (i, c))
    const2 = lambda a: pl.BlockSpec(a.shape, lambda i: (0, 0))
    out = pl.BlockSpec((tm, w), lambda i: (i, 0))
    out_dtypes = [BF16, F32] if want_vn else [BF16]
    return pl.pallas_call(
        functools.partial(_cmlp_kernel, clen=clen, chunks=chunks),
        grid=(rows // tm,),
        in_specs=[col(u_col), col(u_col + 1), col(u_col + 2),
                  pl.BlockSpec(ws.shape, lambda i: (0, 0, 0)), const2(bs_full),
                  pl.BlockSpec((1, w), lambda i: (0, 0)), pl.BlockSpec((1, w), lambda i: (0, 0))],
        out_specs=[out] * len(out_dtypes),
        out_shape=[jax.ShapeDtypeStruct((rows, w), dt) for dt in out_dtypes],
        compiler_params=_params("parallel"),
        name="chunk_mlp",
    )(proj, proj, proj, ws, bs_full, ln_w.reshape(1, w), ln_b.reshape(1, w))


SB_HEADS_PER_STEP = 2
SB_BLOCK = 256


def _sb_group(q16, k16, v16, tk, u, masks, carry):
    tq = q16.shape[0]
    nb = k16.shape[0] // tk
    z_all = _dot_nt(q16, k16)
    nl, log_beta = [], []
    for j in range(nb):
        z2 = z_all[:, j * tk:(j + 1) * tk] * (math.log2(math.e) / math.sqrt(HEAD_B))
        neg_abs = lax.bitcast_convert_type(lax.bitcast_convert_type(z2, jnp.uint32) | jnp.uint32(0x80000000), F32)
        nl_j = jnp.maximum(z2, 0.0) + jnp.log2(1.0 + jnp.exp2(neg_abs))
        log_beta.append(z2 - nl_j)
        nl.append(jnp.where(masks[j], nl_j, 0.0) if j in masks else nl_j)
    lhs = jnp.concatenate([jnp.concatenate(_split2(nl_j), axis=1) for nl_j in nl], axis=0)
    cs = _dot_row_halves(_dot, lhs, u)
    att = [None] * nb
    for j in reversed(range(nb)):
        cs_j = cs[j * tq:(j + 1) * tq]
        att_j = jnp.exp2(log_beta[j] - cs_j - carry)
        att[j] = (jnp.where(masks[j], att_j, 0.0) if j in masks else att_j).astype(BF16)
        carry = carry + (cs_j[:, 0:1] + nl[j][:, 0:1])
    return _dot_row_halves(_dot, jnp.concatenate(att, axis=1), v16), carry


def _head_slices(ref):
    return [slice(i * HEAD_B, (i + 1) * HEAD_B) for i in range(ref.shape[1] // HEAD_B)]


def _emit_layer_rows(out_ref, rows, layer):
    if len(out_ref.shape) == 2:
        out_ref[...] = rows
    else:
        for l in range(out_ref.shape[0]):
            out_ref[l] = rows if l == layer else jnp.zeros_like(rows)


def _sb_prompt_kernel(*refs, tq, nq, layer):
    q_ref, z_ref, k_ref, v_ref, u_ref = refs[:5]
    o_ref, kout_ref, vout_ref = refs[-3:]
    _emit_layer_rows(kout_ref, k_ref[...], layer)
    _emit_layer_rows(vout_ref, v_ref[...], layer)
    row = lax.broadcasted_iota(jnp.int32, (tq, tq), 0)
    col = lax.broadcasted_iota(jnp.int32, (tq, tq), 1)
    diagonal = col < row
    k16 = k_ref[...].astype(BF16)
    v16 = v_ref[...].astype(BF16)
    for qi in range(nq):
        rows = slice(qi * tq, (qi + 1) * tq)
        n_keys = (qi + 1) * tq
        for s in _head_slices(q_ref):
            out, _ = _sb_group(q_ref[rows, s].astype(BF16), k16[0:n_keys, s], v16[0:n_keys, s], tq, u_ref[...],
                               {qi: diagonal}, jnp.zeros((tq, 1), F32))
            o_ref[rows, s] = (out * jax.nn.silu(z_ref[rows, s])).astype(o_ref.dtype)


def _sb_sample_kernel(*refs, tk, layer):
    q_ref, z_ref, kn_ref, vn_ref, kp_ref, vp_ref, ud_ref, uf_ref = refs[:8]
    o_ref, kout_ref, vout_ref = refs[-3:]
    _emit_layer_rows(kout_ref, kn_ref[...], layer)
    _emit_layer_rows(vout_ref, vn_ref[...], layer)
    tq = q_ref.shape[0]
    row = lax.broadcasted_iota(jnp.int32, (tq, tq), 0)
    col = lax.broadcasted_iota(jnp.int32, (tq, tq), 1)
    heads = _head_slices(q_ref)
    n_past = kp_ref.shape[0] // len(heads)
    for hd, s in enumerate(heads):
        q16 = q_ref[:, s].astype(BF16)
        out_new, carry = _sb_group(q16, kn_ref[:, s].astype(BF16), vn_ref[:, s].astype(BF16), tq, ud_ref[...],
                                   {0: col < row}, jnp.zeros((tq, 1), F32))
        own_rows = pl.ds(hd, n_past, stride=len(heads))
        out_past, _ = _sb_group(q16, kp_ref[own_rows, :].astype(BF16), vp_ref[own_rows, :].astype(BF16), tk,
                                uf_ref[...], {}, carry)
        o_ref[:, s] = ((out_new + out_past) * jax.nn.silu(z_ref[:, s])).astype(o_ref.dtype)


def _suffix_matrix(tk):
    s_later = lax.broadcasted_iota(jnp.int32, (tk, tk), 0)
    s_here = lax.broadcasted_iota(jnp.int32, (tk, tk), 1)
    u = (s_later > s_here).astype(BF16)
    return jnp.concatenate([u, u], axis=0)


def stick_breaking(proj, q_col, z_col, k_col, v_col, past_k, past_v, nseq, t, heads, layer, depth, kv_out):
    hp = SB_HEADS_PER_STEP if past_k is None else heads
    wide = hp * HEAD_B
    tq = min(SB_BLOCK, t)
    seq = lambda rows, col: pl.BlockSpec((rows, wide), lambda b, h: (b, col // hp + h))
    const = lambda a: pl.BlockSpec(a.shape, lambda b, h: (0, 0))
    if past_k is None:
        u = _suffix_matrix(tq)
        body = functools.partial(_sb_prompt_kernel, tq=tq, nq=t // tq, layer=layer)
        in_specs = [seq(t, q_col), seq(t, z_col), seq(t, k_col), seq(t, v_col), const(u)]
        args = [proj, proj, proj, proj, u]
    else:
        assert t == tq
        n_past = past_k.shape[2]
        past = pl.BlockSpec((None, None, n_past * heads, HEAD_B), lambda b, h: (layer, b, 0, 0))
        past_k, past_v = (a.reshape(depth, nseq, n_past * heads, HEAD_B) for a in (past_k, past_v))
        ud, uf = _suffix_matrix(tq), _suffix_matrix(SB_BLOCK)
        body = functools.partial(_sb_sample_kernel, tk=SB_BLOCK, layer=layer)
        in_specs = [seq(t, q_col), seq(t, z_col), seq(t, k_col), seq(t, v_col), past, past, const(ud), const(uf)]
        args = [proj, proj, proj, proj, past_k, past_v, ud, uf]
    kv_shape = jax.ShapeDtypeStruct((depth, nseq * t, heads * HEAD_B), F32)
    if kv_out is None:
        aliases = {}
        kv_spec = pl.BlockSpec((depth, t, wide), lambda b, h: (0, b, h))
    else:
        aliases = {len(args): 1, len(args) + 1: 2}
        in_specs = in_specs + [pl.BlockSpec(memory_space=pl.ANY)] * 2
        args = args + list(kv_out)
        kv_spec = pl.BlockSpec((None, t, wide), lambda b, h: (layer, b, h))
    o_b, k_all, v_all = pl.pallas_call(
        body,
        grid=(nseq, heads // hp),
        in_specs=in_specs,
        out_specs=[seq(t, 0), kv_spec, kv_spec],
        out_shape=[jax.ShapeDtypeStruct((nseq * t, heads * HEAD_B), BF16), kv_shape, kv_shape],
        input_output_aliases=aliases,
        compiler_params=_params("parallel", "parallel"),
        name="stick_breaking",
    )(*args)
    return o_b, (k_all, v_all)


def _block_stack(x, lane_masks):
    return jnp.concatenate([jnp.where(m, x, 0.0) for m in lane_masks], axis=0)


def _rwkv_kernel(*refs, chunk, width, n_cast, drop):
    (x_ref, lora_ref, z_ref, sh_main_ref, sh_lora_ref, s0_ref,
     mu_main_ref, mu_lora_ref, w0_ref, wup_ref, a0_ref, aup_ref, kk_ref, ka_ref, rk_ref,
     gnw_ref, gnb_ref, e_ref, bd_ref, ltri_ref) = refs[:20]
    xbuf, lbuf, sbd = refs[-3:]
    o_ref, sout_ref, shm_out_ref, shl_out_ref = refs[20 + n_cast:24 + n_cast]
    cast_pairs = list(zip(refs[20:20 + n_cast], refs[24 + n_cast:24 + 2 * n_cast]))
    for i, (cast_in_ref, cast_out_ref) in enumerate(cast_pairs):
        val = cast_in_ref[...]
        if drop is not None and i == n_cast - 1:
            val = jnp.concatenate([val[:, :drop[0]], val[:, drop[1]:]], axis=1)
        cast_out_ref[...] = val.astype(cast_out_ref.dtype)
    c = pl.program_id(1)
    n_chunks = pl.num_programs(1)
    C, W = chunk, width
    n_seq = x_ref.shape[0]
    n_slabs = W // SLAB
    n_heads = W // HEAD_A

    @pl.when(c == 0)
    def _init():
        sbd[...] = jnp.zeros_like(sbd)
        for s in range(n_seq):
            xbuf[s, 7:8, :] = sh_main_ref[s]
            lbuf[s, 7:8, :] = sh_lora_ref[s]
            for hd in range(n_heads):
                g, j = divmod(hd, GROUP_A)
                sbd[s, g, j * HEAD_A:(j + 1) * HEAD_A, j * HEAD_A:(j + 1) * HEAD_A] = s0_ref[s, hd]

    def shifted(ref, buf, mu_ref):
        rows = []
        for s in range(n_seq):
            x = ref[s]
            buf[s, 8:8 + C, :] = x
            rows.append(x + mu_ref[...] * (buf[s, 7:7 + C, :] - x))
            buf[s, 7:8, :] = x[C - 1:C, :]
        return jnp.concatenate(rows, axis=0)

    xs = shifted(x_ref, xbuf, mu_main_ref)
    lo_s = shifted(lora_ref, lbuf, mu_lora_ref)
    seq_rows = [slice(s * C, (s + 1) * C) for s in range(n_seq)]
    slabs = [slice(g * SLAB, (g + 1) * SLAB) for g in range(n_slabs)]

    r, k, v = xs[:, :W], xs[:, W:2 * W], xs[:, 2 * W:]
    w_pre = w0_ref[...] + _dot(jnp.tanh(lo_s).astype(BF16), wup_ref[...])
    ld = -math.exp(-0.5) * jax.nn.sigmoid(w_pre)
    a = jax.nn.sigmoid(a0_ref[...] + _dot(lo_s.astype(BF16), aup_ref[...]))

    e_mat = e_ref[...]

    segs = [slice(g * SEG_LANES, (g + 1) * SEG_LANES) for g in range(W // SEG_LANES)]

    def seg_sum(val):
        n = val.shape[0]
        hi, lo = _split2(val)
        out = _dot(jnp.concatenate([part[:, sl] for part in (hi, lo) for sl in segs], axis=0), e_mat)
        return jnp.concatenate([out[g * n:(g + 1) * n] + out[(len(segs) + g) * n:(len(segs) + g + 1) * n]
                                for g in range(len(segs))], axis=1)

    kk = k * kk_ref[...]
    kk = kk * lax.rsqrt(jnp.maximum(seg_sum(kk * kk), 1e-24))
    kmod = k * (1.0 + (a - 1.0) * ka_ref[...])

    ld_hi = ld.astype(BF16)
    ld_r1 = ld - ld_hi.astype(F32)
    ld_mid = ld_r1.astype(BF16)
    ld_lo = (ld_r1 - ld_mid.astype(F32)).astype(BF16)
    ltri = ltri_ref[...]
    lp = _dot(ltri, ld_hi) + _dot(ltri, ld_mid) + _dot(ltri, ld_lo)
    lp_last = [lp[rs.stop - 1:rs.stop, :] for rs in seq_rows]
    lp_end = jnp.concatenate([jnp.broadcast_to(row, (C, W)) for row in lp_last], axis=0)
    e_neg = jnp.exp(-lp)
    kka = kk * a
    kap = kk * jnp.exp(lp - ld)
    bet = kka * e_neg
    kt = kmod * e_neg
    rt = r * jnp.exp(lp)
    e_end = jnp.exp(lp_end - lp)
    kt_end = kmod * e_end
    bet_end = kka * e_end
    dec_end = [jnp.exp(row) for row in lp_last]

    lane = lax.broadcasted_iota(jnp.int32, (1, SLAB), 1)
    lane_masks = [(lane >= j * HEAD_A) & (lane < (j + 1) * HEAD_A) for j in range(GROUP_A)]
    t_row = lax.broadcasted_iota(jnp.int32, (C, GROUP_A * C), 0)
    s_col = lax.broadcasted_iota(jnp.int32, (C, GROUP_A * C), 1) & (C - 1)
    strict, incl = s_col < t_row, s_col <= t_row
    stack = lambda val16: _block_stack(val16, lane_masks)
    nh = GROUP_A * C
    units = [(s, g) for s in range(n_seq) for g in range(n_slabs)]
    per_slab = lambda fn: [fn(u, seq_rows[s], slabs[g]) for u, (s, g) in enumerate(units)]

    kap16, rt16, bet16, kt16, v16 = (val.astype(BF16) for val in (kap, rt, bet, kt, v))
    s_old = [sbd[s, g] for s, g in units]
    lhs = per_slab(lambda g, rs, sl: jnp.concatenate([kap16[rs, sl], rt16[rs, sl]], axis=0))
    rhs = per_slab(lambda g, rs, sl: jnp.concatenate([stack(bet16[rs, sl]), stack(kt16[rs, sl])], axis=0))
    sc = per_slab(lambda g, rs, sl: _dot_nt(lhs[g], rhs[g]))
    ls = per_slab(lambda g, rs, sl: _dot_nt(lhs[g], s_old[g].astype(BF16)))
    v_stack = per_slab(lambda g, rs, sl: stack(v16[rs, sl]))
    p16 = per_slab(lambda g, rs, sl: jnp.where(strict, -sc[g][:C, :nh], 0.0).astype(BF16))
    xw = per_slab(lambda g, rs, sl: ls[g][:C]
                  + _dot(jnp.where(strict, sc[g][:C, nh:], 0.0).astype(BF16), v_stack[g]))
    for step in range(6):
        xw = per_slab(lambda g, rs, sl: xw[g] + _dot(p16[g], stack(xw[g].astype(BF16))))
        if step < 5:
            p16 = per_slab(lambda g, rs, sl: _dot(p16[g], stack(p16[g])).astype(BF16))
    ab_inc = per_slab(lambda g, rs, sl: jnp.concatenate(
        [jnp.where(incl, sc[g][C:, :nh], 0.0), jnp.where(incl, sc[g][C:, nh:], 0.0)], axis=1).astype(BF16))
    y = per_slab(lambda g, rs, sl: ls[g][C:] + _dot(
        ab_inc[g], jnp.concatenate([stack((-xw[g]).astype(BF16)), v_stack[g]], axis=0)))
    upd = per_slab(lambda g, rs, sl: _dot(
        jnp.concatenate([v[rs, sl], -xw[g]], axis=0).T.astype(BF16),
        jnp.concatenate([kt_end[rs, sl], bet_end[rs, sl]], axis=0).astype(BF16)))
    for u, (s, g) in enumerate(units):
        sbd[s, g] = s_old[u] * dec_end[s][:, slabs[g]] + upd[u] * bd_ref[...]

    y = jnp.concatenate([jnp.concatenate(y[s * n_slabs:(s + 1) * n_slabs], axis=1) for s in range(n_seq)], axis=0)
    inv_n = 1.0 / HEAD_A
    mean = seg_sum(y) * inv_n
    yc = y - mean
    var = seg_sum(yc * yc) * inv_n
    y = yc * lax.rsqrt(var + GN_EPS) * gnw_ref[...] + gnb_ref[...]
    y = y + seg_sum(r * kmod * rk_ref[...]) * v
    for s, rs in enumerate(seq_rows):
        o_ref[s] = (y[rs] * jax.nn.silu(z_ref[s])).astype(o_ref.dtype)

    @pl.when(c == n_chunks - 1)
    def _fin():
        for s in range(n_seq):
            shm_out_ref[s] = xbuf[s, 7:8, :]
            shl_out_ref[s] = lbuf[s, 7:8, :]
            for hd in range(n_heads):
                g, j = divmod(hd, GROUP_A)
                sout_ref[s, hd] = sbd[s, g, j * HEAD_A:(j + 1) * HEAD_A, j * HEAD_A:(j + 1) * HEAD_A]


RWKV_SEQS_PER_STEP = 2


def rwkv7(proj, lora, z_col, shift_main, shift_lora, s0, prm, nseq, t, cast_f32=(), cast_drop=None):
    W = prm["a_w0"].shape[0]
    C = RWKV_CHUNK
    S = RWKV_SEQS_PER_STEP
    nc = t // C
    n_heads = W // HEAD_A
    lw = lora.shape[1]
    half = lw // 2
    zpad = jnp.zeros((half, W), F32)
    wup = jnp.concatenate([prm["a_w_up"], zpad], axis=0).astype(BF16)
    aup = jnp.concatenate([zpad, prm["a_a_up"]], axis=0).astype(BF16)
    same_head = lambda n: (jnp.arange(n)[:, None] // HEAD_A) == (jnp.arange(n)[None, :] // HEAD_A)
    ti = jnp.arange(S * C)
    ltri = ((ti[:, None] >= ti[None, :]) & (ti[:, None] // C == ti[None, :] // C)).astype(BF16)
    row1 = lambda a: a.reshape(1, -1)
    vec = lambda n: pl.BlockSpec((1, n), lambda b, c: (0, 0))
    full = lambda a: pl.BlockSpec(a.shape, lambda b, c: (0, 0))
    rows = lambda n, col: pl.BlockSpec((S, C, n), lambda b, c: (b, c, col))
    per_seq = lambda n: pl.BlockSpec((S, 1, n), lambda b, c: (b, 0, 0))
    state = pl.BlockSpec((S, n_heads, HEAD_A, HEAD_A), lambda b, c: (b, 0, 0, 0))
    e_mat, bd_mask = same_head(SEG_LANES).astype(BF16), same_head(SLAB).astype(F32)
    proj3 = proj.reshape(nseq, t, proj.shape[1])
    in_specs = [rows(3 * W, 0), rows(lw, 0), rows(W, z_col),
                per_seq(3 * W), per_seq(lw), state,
                vec(3 * W), vec(lw), vec(W), full(wup), vec(W), full(aup), vec(W), vec(W), vec(W), vec(W), vec(W),
                full(e_mat), full(bd_mask), full(ltri)]
    args = [proj3, lora.reshape(nseq, t, lw), proj3, shift_main, shift_lora, s0,
            row1(prm["a_mu"][:3 * W]), row1(prm["a_mu"][3 * W:]), row1(prm["a_w0"]), wup, row1(prm["a_a0"]), aup,
            row1(prm["a_k_k"]), row1(prm["a_k_a"]), row1(prm["a_r_k"]), row1(prm["a_gn_w"]), row1(prm["a_gn_b"]),
            e_mat, bd_mask, ltri]
    out_specs = [rows(W, 0), state, per_seq(3 * W), per_seq(lw)]
    out_shape = [jax.ShapeDtypeStruct((nseq, t, W), BF16),
                 jax.ShapeDtypeStruct((nseq, n_heads, HEAD_A, HEAD_A), F32),
                 jax.ShapeDtypeStruct((nseq, 1, 3 * W), F32),
                 jax.ShapeDtypeStruct((nseq, 1, lw), F32)]
    steps = (nseq // S) * nc
    for arr in cast_f32:
        flat = arr.reshape(-1, arr.shape[-1])
        slab = pl.BlockSpec((flat.shape[0] // steps, flat.shape[1]), lambda b, c: (b * nc + c, 0))
        in_specs, args = in_specs + [slab], args + [flat]
        out_specs, out_shape = out_specs + [slab], out_shape + [jax.ShapeDtypeStruct(flat.shape, BF16)]
    n_cast = len(cast_f32)
    if cast_drop is not None:
        w_all, w_layer, lo, hi = cast_drop
        _, k_rows, n_cols = w_all.shape
        in_specs = in_specs + [pl.BlockSpec((None, k_rows // steps, n_cols), lambda b, c: (w_layer, b * nc + c, 0))]
        out_specs = out_specs + [pl.BlockSpec((k_rows // steps, n_cols - (hi - lo)), lambda b, c: (b * nc + c, 0))]
        out_shape = out_shape + [jax.ShapeDtypeStruct((k_rows, n_cols - (hi - lo)), BF16)]
        args = args + [w_all]
        n_cast += 1
    o_a, s_new, sh_main, sh_lora, *cast16 = pl.pallas_call(
        functools.partial(_rwkv_kernel, chunk=C, width=W, n_cast=n_cast,
                          drop=None if cast_drop is None else (lo, hi)),
        grid=(nseq // S, nc),
        in_specs=in_specs,
        out_specs=out_specs,
        out_shape=out_shape,
        scratch_shapes=[pltpu.VMEM((S, C + 8, 3 * W), F32), pltpu.VMEM((S, C + 8, lw), F32),
                        pltpu.VMEM((S, W // SLAB, SLAB, SLAB), F32)],
        compiler_params=_params("parallel", "arbitrary"),
        name="rwkv7",
    )(*args)
    converted = [c16.reshape(arr.shape) for c16, arr in zip(cast16, cast_f32)]
    if cast_drop is not None:
        converted.append(cast16[-1][None])
    return o_a.reshape(nseq * t, W), s_new, sh_main, sh_lora, converted


COL_AZ, COL_BQ, COL_BK, COL_BV, COL_BZ, COL_CU, COL_MQ = 3, 4, 5, 6, 7, 8, 11


def trunk_layer(x, h, nseq, t, mk, mv, shift_prev, s_prev, past_k, past_v, prm, layer, depth, kv_out, want_vn):
    m, d = x.shape
    W = d // 4
    tm = min(512, m)
    if h is None:
        h = rmsnorm_rows(x, prm["g_pre"], min(256, m))
    proj = matmul(h, prm["w_in_main"], prm["w_in_main_layer"], min(1024, m), 1024)
    lora = matmul(h, prm["w_in_lora"], layer, min(1024, m), prm["w_in_lora"].shape[2])

    to_cast = [name for name in ("w_gate", "w_br", "w_out") if prm[name].dtype == F32]
    o_a, s_new, sh_main, sh_lora, cast16 = rwkv7(
        proj, lora, COL_AZ, shift_prev[..., :3 * W], shift_prev[..., 3 * W:], s_prev, prm, nseq, t,
        cast_f32=[prm[name] for name in to_cast], cast_drop=prm["w_in_next"])
    converted = dict(zip(to_cast + ["w_in_main_next"], cast16))
    weight = lambda name: converted.get(name, prm[name])
    shift_new = jnp.concatenate([sh_main, sh_lora], axis=-1)

    heads_b = W // HEAD_B
    hb = lambda col: col * heads_b
    o_b, kv_out = stick_breaking(proj, hb(COL_BQ), hb(COL_BZ), hb(COL_BK), hb(COL_BV), past_k, past_v,
                                 nseq, t, heads_b, layer, depth, kv_out)

    clen = min(t, prm["c_ws"].shape[1])
    groups = prm["c_ws"].shape[0]
    bs_full = jnp.repeat(prm["c_bs"][:, :clen].T, W // groups, axis=1)
    o_c, *vn_c = chunk_mlp(proj, COL_CU, prm["c_ws"][:, :clen, :clen], bs_full, prm["c_ln_w"], prm["c_ln_b"],
                           m, clen, max(1, min(512, t) // clen), want_vn)

    o_m = memory_attention(proj, COL_MQ, mk, mv, nseq, t, min(512, t))

    merged = gated_merge(h, (o_a, o_b, o_c, o_m), weight("w_gate"), prm["b_gate"], weight("w_br"), layer, tm,
                         512 if m > tm else 256)
    x_new, h_next = out_proj_norm_residual(merged, weight("w_out"), layer, x, prm["g_post"], prm["g_pre_next"], tm,
                                           512)
    return x_new, h_next, shift_new, s_new, kv_out, vn_c, converted


def kernel(x_prompt, x_sample, cache_mem_k, cache_mem_v, cache_sb_k, cache_sb_v, state_rwkv, state_shift, mem_prompt, g_pre, g_post, w_in, a_mu, a_w0, a_w_up, a_a0, a_a_up, a_k_k, a_k_a, a_r_k, a_gn_w, a_gn_b, c_ws, c_bs, c_ln_w, c_ln_b, g_mem, w_mem_kv, w_gate, b_gate, w_br, w_out):
    bp, tp, d = x_prompt.shape
    bs_, ts, _ = x_sample.shape
    depth = w_in.shape[0]
    W = d // 4
    n_mem = mem_prompt.shape[1]
    m_heads = cache_mem_k.shape[3]
    b_heads = cache_sb_k.shape[3]
    a_heads = state_rwkv.shape[2]
    shift_w = state_shift.shape[-1]
    n_past = cache_sb_k.shape[2]
    lora_lo, lora_hi = 3 * W, shift_w

    yp = x_prompt.reshape(bp * tp, d)
    ys = x_sample.reshape(bs_ * ts, d)
    mem_rows = mem_prompt.reshape(bp * n_mem, d)
    mem_out, rw_p, sh_p, rw_s, sh_s, cv_s = [[], []], [], [], [], [], []
    kv_p = kv_s = hp = hs = None
    dense = {
        "w_in_main": drop_cols_cast(w_in, 1, lora_lo, lora_hi, W, 1024), "w_in_main_layer": 0,
        "w_in_lora": w_in[:, :, lora_lo:lora_hi].astype(BF16),
        "w_gate": w_gate, "w_br": w_br, "w_out": w_out,
    }
    w_mem16 = w_mem_kv.astype(BF16)
    for l in range(depth):
        prm = dict(dense)
        prm.update({
            "w_in_next": (w_in, l + 1, lora_lo, lora_hi) if l + 1 < depth else None,
            "g_pre": g_pre[l], "g_post": g_post[l], "g_pre_next": g_pre[l + 1] if l + 1 < depth else None,
            "a_mu": a_mu[l], "a_w0": a_w0[l], "a_w_up": a_w_up[l], "a_a0": a_a0[l], "a_a_up": a_a_up[l],
            "a_k_k": a_k_k[l].reshape(-1), "a_k_a": a_k_a[l].reshape(-1), "a_r_k": a_r_k[l].reshape(-1),
            "a_gn_w": a_gn_w[l].reshape(-1), "a_gn_b": a_gn_b[l].reshape(-1),
            "c_ws": c_ws[l], "c_bs": c_bs[l], "c_ln_w": c_ln_w[l], "c_ln_b": c_ln_b[l], "b_gate": b_gate[l],
        })
        kv = matmul(rmsnorm_rows(mem_rows, g_mem[l], 256), w_mem16, l, min(1024, bp * n_mem), 1024)
        mk = kv[:, :W].reshape(bp, n_mem, W)
        mv = kv[:, W:].reshape(bp, n_mem, W)
        shift0 = jnp.zeros((bp, 1, shift_w), F32)
        s0 = jnp.zeros((bp, a_heads, HEAD_A, HEAD_A), F32)
        yp, hp, sh, st, kv_p, _, converted = trunk_layer(
            yp, hp, bp, tp, mk.astype(BF16), mv.astype(BF16), shift0, s0, None, None, prm, l, depth, kv_p, False)
        next_main = converted.pop("w_in_main_next", None)
        dense.update(converted)
        prm.update(converted)
        prm["w_in_next"] = None
        mem_out[0].append(mk.reshape(bp, n_mem, m_heads, W // m_heads))
        mem_out[1].append(mv.reshape(bp, n_mem, m_heads, W // m_heads))
        rw_p.append(st)
        sh_p.append(sh)
        ys, hs, sh, st, kv_s, cvn, _ = trunk_layer(
            ys, hs, bs_, ts, cache_mem_k[l].reshape(bs_, n_mem, W).astype(BF16),
            cache_mem_v[l].reshape(bs_, n_mem, W).astype(BF16), state_shift[l], state_rwkv[l],
            cache_sb_k, cache_sb_v, prm, l, depth, kv_s, True)
        rw_s.append(st)
        sh_s.append(sh)
        cv_s.append(cvn[0].reshape(bs_, ts, W))
        dense["w_in_main"] = next_main
    heads_p =lambda a: a.reshape(depth, bp, tp, b_heads, HEAD_B)
    heads_s = lambda a: a.reshape(depth, bs_, ts, b_heads, HEAD_B)
    return (yp.reshape(bp, tp, d), ys.reshape(bs_, ts, d), jnp.stack(mem_out[0]), jnp.stack(mem_out[1]),
            heads_p(kv_p[0]), heads_p(kv_p[1]), jnp.stack(rw_p), jnp.stack(sh_p),
            heads_s(kv_s[0]), heads_s(kv_s[1]), jnp.stack(rw_s), jnp.stack(sh_s), jnp.stack(cv_s))
```
